```python
import jax, jax.numpy as jnp
from jax import lax
import numpy as np

D_MODEL = 1024
BATCH = 16
SEQ = 2048
DEPTH = 2
DEC_BATCH = 32
DEC_SEQ = 32
PAST_LEN = 4096

CHUNK = 64
Q_BLOCK = 128
D_PLE = 256
SB_HEADS = 8
SB_HEAD_DIM = 64
MLA_HEADS = 8
MLA_NOPE = 64
MLA_ROPE = 32
MLA_V = 64
KV_LORA = 128
D_FF = 2816
CONV_W = 3
ROPE_THETA = 10000.0
EPS = 1e-6

SB_W = SB_HEADS * SB_HEAD_DIM
MLA_QW = MLA_HEADS * (MLA_NOPE + MLA_ROPE)
MLA_OW = MLA_HEADS * MLA_V
IN_COLS = 3 * SB_W + MLA_QW + KV_LORA + MLA_ROPE + 2 * D_MODEL
SPLIT_POINTS = (SB_W, 2 * SB_W, 3 * SB_W, 3 * SB_W + MLA_QW, 3 * SB_W + MLA_QW + KV_LORA,
                3 * SB_W + MLA_QW + KV_LORA + MLA_ROPE, 3 * SB_W + MLA_QW + KV_LORA + MLA_ROPE + D_MODEL)
SB_SCALE = SB_HEAD_DIM ** -0.5
MLA_SCALE = (MLA_NOPE + MLA_ROPE) ** -0.5

kernel_name = 'stickbreak_mla_convffn_stream_step'


def rmsnorm(x, g):
    xf = x.astype(jnp.float32)
    y = xf * lax.rsqrt(jnp.mean(xf * xf, axis=-1, keepdims=True) + EPS)
    return (y * g.astype(jnp.float32)).astype(x.dtype)


def rope(x, pos):
    half = x.shape[-1] // 2
    freqs = jnp.power(ROPE_THETA, -jnp.arange(half, dtype=jnp.float32) / half)
    ang = pos.astype(jnp.float32)[:, None] * freqs[None, :]
    shape = (1, pos.shape[0]) + (1,) * (x.ndim - 3) + (half,)
    cos, sin = jnp.cos(ang).reshape(shape), jnp.sin(ang).reshape(shape)
    xf = x.astype(jnp.float32)
    x1, x2 = xf[..., :half], xf[..., half:]
    return jnp.concatenate([x1 * cos - x2 * sin, x1 * sin + x2 * cos], axis=-1).astype(x.dtype)


def sb_attend(q, k, v, q_pos, k_pos):
    z = jnp.einsum('bqhd,bkhd->bhqk', q, k).astype(jnp.float32) * SB_SCALE
    visible = k_pos[None, :] < q_pos[:, None]
    log_1mb = jnp.where(visible, jax.nn.log_sigmoid(-z), 0.0)
    suffix = lax.cumsum(log_1mb, axis=3, reverse=True) - log_1mb
    a = jnp.where(visible, jnp.exp(jax.nn.log_sigmoid(z) + suffix), 0.0)
    return jnp.einsum('bhqk,bkhd->bqhd', a.astype(v.dtype), v)


def mla_attend(q_lat, q_rope, c_kv, k_rope, q_pos, k_pos):
    s = (jnp.einsum('bqhl,bkl->bhqk', q_lat, c_kv).astype(jnp.float32)
         + jnp.einsum('bqhr,bkr->bhqk', q_rope, k_rope).astype(jnp.float32)) * MLA_SCALE
    visible = (k_pos[None, :] // CHUNK) <= (q_pos[:, None] // CHUNK)
    p = jax.nn.softmax(jnp.where(visible, s, -jnp.inf), axis=-1)
    return jnp.einsum('bhqk,bkl->bqhl', p.astype(c_kv.dtype), c_kv)


def sweep_query_blocks(fn, qs, q_pos):
    B, S = qs[0].shape[:2]
    nb = S // Q_BLOCK
    def to_blocks(a):
        return jnp.moveaxis(a.reshape((B, nb, Q_BLOCK) + a.shape[2:]), 1, 0)
    out = lax.map(lambda args: fn(args[0], args[1]),
                  (tuple(to_blocks(a) for a in qs), q_pos.reshape(nb, Q_BLOCK)))
    out = jnp.moveaxis(out, 0, 1)
    return out.reshape((B, S) + out.shape[3:])


def trunk_layer(x, p, pos, past, lw):
    (w_in, ckv_gain, w_uk, w_uv, w_proj_sb, w_proj_mla, w_out, g_pre_mix, g_post_mix,
     g_pre_ffn, g_post_ffn, w_up, conv_w, conv_b, w_down, w_ple_gate, w_ple_proj) = lw
    B, T, _ = x.shape
    h = rmsnorm(x, g_pre_mix)
    q_sb, k_sb, v_sb, q_mla, ckv, k_rope, gate_a, gate_b = jnp.split(h @ w_in, SPLIT_POINTS, axis=-1)
    q_sb = q_sb.reshape(B, T, SB_HEADS, SB_HEAD_DIM)
    k_sb = k_sb.reshape(B, T, SB_HEADS, SB_HEAD_DIM)
    v_sb = v_sb.reshape(B, T, SB_HEADS, SB_HEAD_DIM)
    q_mla = q_mla.reshape(B, T, MLA_HEADS, MLA_NOPE + MLA_ROPE)
    q_nope, q_rope = q_mla[..., :MLA_NOPE], rope(q_mla[..., MLA_NOPE:], pos)
    ckv = rmsnorm(ckv, ckv_gain)
    k_rope = rope(k_rope, pos)
    q_lat = jnp.einsum('bthn,lhn->bthl', q_nope, w_uk)
    if past is None:
        k_pos = pos
        conv_past = jnp.zeros((B, CONV_W - 1, 2 * D_FF), x.dtype)
        o_sb = sweep_query_blocks(lambda qs, qp: sb_attend(qs[0], k_sb, v_sb, qp, k_pos), (q_sb,), pos)
        o_lat = sweep_query_blocks(lambda qs, qp: mla_attend(qs[0], qs[1], ckv, k_rope, qp, k_pos),
                                   (q_lat, q_rope), pos)
    else:
        pk, pv, pc, pr, conv_past = past
        k_all = jnp.concatenate([pk, k_sb], axis=1)
        v_all = jnp.concatenate([pv, v_sb], axis=1)
        c_all = jnp.concatenate([pc, ckv], axis=1)
        r_all = jnp.concatenate([pr, k_rope], axis=1)
        k_pos = jnp.arange(pk.shape[1] + T, dtype=jnp.int32)
        o_sb = sb_attend(q_sb, k_all, v_all, pos, k_pos)
        o_lat = mla_attend(q_lat, q_rope, c_all, r_all, pos, k_pos)
    o_mla = jnp.einsum('bthl,lhv->bthv', o_lat, w_uv)
    y_a = o_sb.reshape(B, T, SB_W) @ w_proj_sb
    y_b = o_mla.reshape(B, T, MLA_OW) @ w_proj_mla
    mix = (jax.nn.sigmoid(gate_a) * y_a + jax.nn.sigmoid(gate_b) * y_b) @ w_out
    x = x + rmsnorm(mix, g_post_mix)
    u = rmsnorm(x, g_pre_ffn) @ w_up
    u_ext = jnp.concatenate([conv_past, u], axis=1)
    uc = conv_b
    for i in range(CONV_W):
        uc = uc + u_ext[:, i:i + T] * conv_w[i]
    gate, val = uc[..., :D_FF], uc[..., D_FF:]
    f = (jax.nn.gelu(gate, approximate=True) * val) @ w_down
    x = x + rmsnorm(f, g_post_ffn)
    x = x + jax.nn.sigmoid(x @ w_ple_gate) * (p @ w_ple_proj)
    new_conv = u_ext[:, -(CONV_W - 1):]
    return x, (k_sb, v_sb, ckv, k_rope, new_conv)


def setup_inputs(seed: int = 0) -> dict:
    key = jax.random.key(seed)
    ks = jax.random.split(key, 32)
    def nrm(k, shape, scale=1.0):
        return jax.random.normal(k, shape, jnp.float32) * scale
    def gain(k, n):
        return 1.0 + 0.01 * jax.random.normal(k, (DEPTH, n), jnp.float32)
    return {
        'x_prompt': nrm(ks[0], (BATCH, SEQ, D_MODEL)),
        'x_sample': nrm(ks[1], (DEC_BATCH, DEC_SEQ, D_MODEL)),
        'p_prompt': nrm(ks[2], (DEPTH, BATCH, SEQ, D_PLE)),
        'p_sample': nrm(ks[3], (DEPTH, DEC_BATCH, DEC_SEQ, D_PLE)),
        'cache_sb_k': nrm(ks[4], (DEPTH, DEC_BATCH, PAST_LEN, SB_HEADS, SB_HEAD_DIM)),
        'cache_sb_v': nrm(ks[5], (DEPTH, DEC_BATCH, PAST_LEN, SB_HEADS, SB_HEAD_DIM)),
        'cache_mla_ckv': nrm(ks[6], (DEPTH, DEC_BATCH, PAST_LEN, KV_LORA)),
        'cache_mla_krope': nrm(ks[7], (DEPTH, DEC_BATCH, PAST_LEN, MLA_ROPE)),
        'state_ffn_conv': nrm(ks[8], (DEPTH, DEC_BATCH, CONV_W - 1, 2 * D_FF)),
        'w_in': nrm(ks[9], (DEPTH, D_MODEL, IN_COLS), D_MODEL ** -0.5),
        'ckv_gain': gain(ks[10], KV_LORA),
        'w_uk': nrm(ks[11], (DEPTH, KV_LORA, MLA_HEADS, MLA_NOPE), KV_LORA ** -0.5),
        'w_uv': nrm(ks[12], (DEPTH, KV_LORA, MLA_HEADS, MLA_V), KV_LORA ** -0.5),
        'w_proj_sb': nrm(ks[13], (DEPTH, SB_W, D_MODEL), SB_W ** -0.5),
        'w_proj_mla': nrm(ks[14], (DEPTH, MLA_OW, D_MODEL), MLA_OW ** -0.5),
        'w_out': nrm(ks[15], (DEPTH, D_MODEL, D_MODEL), D_MODEL ** -0.5),
        'g_pre_mix': gain(ks[16], D_MODEL),
        'g_post_mix': gain(ks[17], D_MODEL),
        'g_pre_ffn': gain(ks[18], D_MODEL),
        'g_post_ffn': gain(ks[19], D_MODEL),
        'w_up': nrm(ks[20], (DEPTH, D_MODEL, 2 * D_FF), D_MODEL ** -0.5),
        'conv_w': nrm(ks[21], (DEPTH, CONV_W, 2 * D_FF), CONV_W ** -0.5),
        'conv_b': nrm(ks[22], (DEPTH, 2 * D_FF), 0.01),
        'w_down': nrm(ks[23], (DEPTH, D_FF, D_MODEL), D_FF ** -0.5),
        'w_ple_gate': nrm(ks[24], (DEPTH, D_MODEL, D_MODEL), D_MODEL ** -0.5),
        'w_ple_proj': nrm(ks[25], (DEPTH, D_PLE, D_MODEL), D_PLE ** -0.5),
    }


def reference(x_prompt, x_sample, p_prompt, p_sample, cache_sb_k, cache_sb_v, cache_mla_ckv,
              cache_mla_krope, state_ffn_conv, w_in, ckv_gain, w_uk, w_uv, w_proj_sb, w_proj_mla,
              w_out, g_pre_mix, g_post_mix, g_pre_ffn, g_post_ffn, w_up, conv_w, conv_b, w_down,
              w_ple_gate, w_ple_proj):
    T_p = x_prompt.shape[1]
    T_s = x_sample.shape[1]
    past_len = cache_sb_k.shape[2]
    pos_p = jnp.arange(T_p, dtype=jnp.int32)
    pos_s = past_len + jnp.arange(T_s, dtype=jnp.int32)
    xp, xs = x_prompt, x_sample
    st_p, st_s = [], []
    for i in range(DEPTH):
        lw = (w_in[i], ckv_gain[i], w_uk[i], w_uv[i], w_proj_sb[i], w_proj_mla[i], w_out[i],
              g_pre_mix[i], g_post_mix[i], g_pre_ffn[i], g_post_ffn[i], w_up[i], conv_w[i],
              conv_b[i], w_down[i], w_ple_gate[i], w_ple_proj[i])
        xp, sp = trunk_layer(xp, p_prompt[i], pos_p, None, lw)
        past = (cache_sb_k[i], cache_sb_v[i], cache_mla_ckv[i], cache_mla_krope[i], state_ffn_conv[i])
        xs, ss = trunk_layer(xs, p_sample[i], pos_s, past, lw)
        st_p.append(sp)
        st_s.append(ss)
    sbk_p = jnp.stack([s[0] for s in st_p])
    sbv_p = jnp.stack([s[1] for s in st_p])
    ckv_p = jnp.stack([s[2] for s in st_p])
    kr_p = jnp.stack([s[3] for s in st_p])
    conv_p = jnp.stack([s[4] for s in st_p])
    sbk_s = jnp.stack([s[0] for s in st_s])
    sbv_s = jnp.stack([s[1] for s in st_s])
    ckv_s = jnp.stack([s[2] for s in st_s])
    kr_s = jnp.stack([s[3] for s in st_s])
    conv_s = jnp.stack([s[4] for s in st_s])
    return (xp, xs, sbk_p, sbv_p, ckv_p, kr_p, conv_p, sbk_s, sbv_s, ckv_s, kr_s, conv_s)
```

```python
import functools
import math

import jax
import jax.numpy as jnp
import numpy as np
from jax import lax
from jax.experimental import pallas as pl
from jax.experimental.pallas import tpu as pltpu

D_MODEL = 1024
N_HEADS = 8
SB_HEAD_DIM = 64
SB_W = N_HEADS * SB_HEAD_DIM
MLA_NOPE = 64
MLA_ROPE = 32
ROPE_HALF = MLA_ROPE // 2
KV_LORA = 128
MLA_V = 64
D_FF = 2816
CONV_W = 3
CHUNK = 64
D_PLE = 256
ROPE_THETA = 10000.0
EPS = 1e-6
SB_SCALE = SB_HEAD_DIM ** -0.5
MLA_SCALE = (MLA_NOPE + MLA_ROPE) ** -0.5
LOG2E = math.log2(math.e)

LANES = 128
SUBLANES = 8
MXU_DIM = 256
VMEM_LIMIT_BYTES = 56 * 1024 * 1024

BLK = MXU_DIM
FF_CHUNK = MXU_DIM
N_FF_CHUNKS = D_FF // FF_CHUNK
QM_W = 2 * LANES
DEC_KEYS = 512

BF16 = jnp.bfloat16
F32 = jnp.float32


def _dot(a, b):
    return jnp.dot(a, b, preferred_element_type=F32)


def _dot_nt(a, b):
    return lax.dot_general(a, b, (((1,), (1,)), ((), ())), preferred_element_type=F32)


def _rmsnorm(x, g):
    ms = jnp.mean(x * x, axis=-1, keepdims=True)
    return x * lax.rsqrt(ms + EPS) * g


def _sigmoid(x):
    return 1.0 / (1.0 + jnp.exp(-x))


def _softplus(z):
    return jnp.maximum(z, 0.0) + jnp.log(1.0 + jnp.exp(-jnp.abs(z)))


def _gelu_tanh(x):
    c = math.sqrt(2.0 / math.pi)
    return 0.5 * x * (1.0 + jnp.tanh(c * (x + 0.044715 * (x * x * x))))


def _split_bf16(x):
    hi = x.astype(BF16)
    lo = (x - hi.astype(F32)).astype(BF16)
    return hi, lo


def _params(*sem):
    return pltpu.CompilerParams(dimension_semantics=sem, vmem_limit_bytes=VMEM_LIMIT_BYTES)


def _const_spec(shape):
    return pl.BlockSpec(shape, lambda *_: (0,) * len(shape), pipeline_mode=pl.Buffered(1))


def _proj_kernel(x_ref, g_ref, cos_ref, sin_ref, wqkv_ref, wqn_ref, wqr_ref, wckv_ref, wkx_ref,
                 gain_ref, wa_ref, wb_ref, eye_ref, *out_refs, transposed):
    if transposed:
        (q_ref, kst_ref, kbf_ref, vst_ref, vt_ref, qm_ref, cm_ref, ct_ref, cst_ref, rst_ref) = out_refs
    else:
        (q_ref, kst_ref, kbf_ref, vst_ref, qm_ref, cm_ref, cst_ref, rst_ref) = out_refs
    hb = _rmsnorm(x_ref[...], g_ref[...]).astype(BF16)

    qkv = _dot(hb, wqkv_ref[...])
    q_ref[...] = (qkv[:, :SB_W] * SB_SCALE).astype(BF16)
    k = qkv[:, SB_W:2 * SB_W]
    v = qkv[:, 2 * SB_W:]
    kst_ref[...] = k
    vst_ref[...] = v
    kbf_ref[...] = k.astype(BF16)
    if transposed:
        vb = v.astype(BF16)
        for p in range(SB_W // LANES):
            vt_ref[0, 0, p * LANES:(p + 1) * LANES, :] = _dot_nt(
                eye_ref[...], vb[:, p * LANES:(p + 1) * LANES]).astype(BF16)

    cos = cos_ref[...]
    sin = sin_ref[...]
    qn = _dot(hb, wqn_ref[...]).astype(BF16)
    qr = _dot(hb, wqr_ref[...])
    x1, x2 = qr[:, :LANES], qr[:, LANES:]
    r12 = jnp.concatenate([x1 * cos - x2 * sin, x1 * sin + x2 * cos], axis=1).astype(BF16)
    for h in range(N_HEADS):
        p = h // 2
        qm_ref[h] = (_dot(qn[:, p * LANES:(p + 1) * LANES], wa_ref[h]) + _dot(r12, wb_ref[h])).astype(BF16)

    ckv = _rmsnorm(_dot(hb, wckv_ref[...]), gain_ref[...])
    cst_ref[...] = ckv
    kx = _dot(hb, wkx_ref[...])
    k1, k2 = kx[:, :LANES], kx[:, LANES:]
    kr1 = k1 * cos - k2 * sin
    kr2 = k1 * sin + k2 * cos
    lane = lax.broadcasted_iota(jnp.int32, kr1.shape, 1)
    kr = jnp.where(lane < ROPE_HALF, kr1, pltpu.roll(kr2, ROPE_HALF, axis=1))
    rst_ref[...] = kr[:, :MLA_ROPE]
    ckv_b = ckv.astype(BF16)
    cm_ref[...] = jnp.concatenate([ckv_b, kr.astype(BF16)], axis=1)
    if transposed:
        ct_ref[0, 0] = _dot_nt(eye_ref[...], ckv_b).astype(BF16)


def _proj(x2d, seq_len, tabs, lw, *, transposed):
    n = x2d.shape[0]
    tm = BLK
    nt = n // tm
    cos, sin = tabs
    tab_blocks = cos.shape[0] // tm
    tok = lambda w: pl.BlockSpec((tm, w), lambda i: (i, 0))
    tab = pl.BlockSpec((tm, LANES), lambda i: (i % tab_blocks, 0))
    in_specs = [tok(D_MODEL), _const_spec((1, D_MODEL)), tab, tab,
                _const_spec(lw["w_qkv"].shape), _const_spec(lw["w_qn"].shape), _const_spec(lw["w_qr"].shape),
                _const_spec(lw["w_ckv"].shape), _const_spec(lw["w_kx"].shape), _const_spec((1, KV_LORA)),
                _const_spec(lw["wa"].shape), _const_spec(lw["wb"].shape), _const_spec((LANES, LANES))]
    out_shape = [jax.ShapeDtypeStruct((n, SB_W), BF16), jax.ShapeDtypeStruct((n, SB_W), F32),
                 jax.ShapeDtypeStruct((n, SB_W), BF16), jax.ShapeDtypeStruct((n, SB_W), F32)]
    out_specs = [tok(SB_W), tok(SB_W), tok(SB_W), tok(SB_W)]
    if transposed:
        assert seq_len % tm == 0
        nb = seq_len // tm
        out_shape.append(jax.ShapeDtypeStruct((n // seq_len, nb, SB_W, tm), BF16))
        out_specs.append(pl.BlockSpec((1, 1, SB_W, tm), lambda i: (i // nb, i % nb, 0, 0)))
    out_shape += [jax.ShapeDtypeStruct((N_HEADS, n, QM_W), BF16), jax.ShapeDtypeStruct((n, QM_W), BF16)]
    out_specs += [pl.BlockSpec((N_HEADS, tm, QM_W), lambda i: (0, i, 0)), tok(QM_W)]
    if transposed:
        out_shape.append(jax.ShapeDtypeStruct((n // seq_len, nb, KV_LORA, tm), BF16))
        out_specs.append(pl.BlockSpec((1, 1, KV_LORA, tm), lambda i: (i // nb, i % nb, 0, 0)))
    out_shape += [jax.ShapeDtypeStruct((n, KV_LORA), F32), jax.ShapeDtypeStruct((n, MLA_ROPE), F32)]
    out_specs += [tok(KV_LORA), tok(MLA_ROPE)]
    return pl.pallas_call(
        functools.partial(_proj_kernel, transposed=transposed),
        grid=(nt,), in_specs=in_specs, out_specs=out_specs, out_shape=out_shape,
        compiler_params=_params("parallel"), name="proj_t" if transposed else "proj",
    )(x2d, lw["g_pre_mix"], cos, sin, lw["w_qkv"], lw["w_qn"], lw["w_qr"], lw["w_ckv"], lw["w_kx"],
      lw["ckv_gain"], lw["wa"], lw["wb"], lw["eye"])


def _sb_block(q2, k2, vt2, acc_ref, car_ref, umat, vis):
    lane = lax.broadcasted_iota(jnp.int32, (1, LANES), 1)
    for hh in range(2):
        head_lanes = (lane >= hh * SB_HEAD_DIM) & (lane < (hh + 1) * SB_HEAD_DIM)
        kh = jnp.where(head_lanes, k2, jnp.zeros_like(k2))
        z = _dot_nt(kh, q2)
        sp = _softplus(z)
        if vis is not None:
            sp = jnp.where(vis, sp, 0.0)
        hi, lo = _split_bf16(sp)
        mm = _dot(umat, hi) + _dot(umat, lo)
        car = car_ref[hh:hh + 1, :]
        a = jnp.exp(z + mm + car)
        if vis is not None:
            a = jnp.where(vis, a, 0.0)
        rows = slice(hh * SB_HEAD_DIM, (hh + 1) * SB_HEAD_DIM)
        acc_ref[rows, :] += _dot(vt2[rows, :], a.astype(BF16))
        car_ref[hh:hh + 1, :] = car + mm[0:1, :]


def _sb_kernel(q_ref, k_ref, vt_ref, o_ref, acc_ref, car_ref, *, nblk):
    kidx = lax.broadcasted_iota(jnp.int32, (BLK, BLK), 0)
    qidx = lax.broadcasted_iota(jnp.int32, (BLK, BLK), 1)
    umat = jnp.where(qidx >= kidx, -1.0, 0.0).astype(BF16)
    vis = kidx < qidx

    def q_block(i, carry):
        acc_ref[...] = jnp.zeros_like(acc_ref)
        car_ref[...] = jnp.zeros_like(car_ref)
        q2 = q_ref[0, pl.ds(pl.multiple_of(i * BLK, BLK), BLK), :]
        _sb_block(q2, k_ref[0, pl.ds(pl.multiple_of(i * BLK, BLK), BLK), :], vt_ref[0, i],
                  acc_ref, car_ref, umat, vis)

        def k_block(jj, c):
            j = i - 1 - jj
            _sb_block(q2, k_ref[0, pl.ds(pl.multiple_of(j * BLK, BLK), BLK), :], vt_ref[0, j],
                      acc_ref, car_ref, umat, None)
            return c

        lax.fori_loop(0, i, k_block, 0)
        o_ref[0, pl.ds(pl.multiple_of(i * BLK, BLK), BLK), :] = acc_ref[...].T.astype(BF16)
        return carry

    lax.fori_loop(0, nblk, q_block, 0)


def _sb_prompt(q, kbf, vt, batch, seq_len):
    nblk = seq_len // BLK
    q3 = q.reshape(batch, seq_len, SB_W)
    k3 = kbf.reshape(batch, seq_len, SB_W)
    pair = pl.BlockSpec((1, seq_len, LANES), lambda b, p: (b, 0, p))
    out = pl.pallas_call(
        functools.partial(_sb_kernel, nblk=nblk),
        grid=(batch, SB_W // LANES),
        in_specs=[pair, pair, pl.BlockSpec((1, nblk, LANES, BLK), lambda b, p: (b, 0, p, 0))],
        out_specs=pair,
        out_shape=jax.ShapeDtypeStruct((batch, seq_len, SB_W), BF16),
        scratch_shapes=[pltpu.VMEM((LANES, BLK), F32), pltpu.VMEM((SUBLANES, BLK), F32)],
        compiler_params=_params("parallel", "parallel"), name="sb_prompt",
    )(q3, k3, vt)
    return out.reshape(batch * seq_len, SB_W)


def _mla_block(q, c, ct, acc_ref, st_ref, vis):
    s = _dot_nt(c, q)
    if vis is not None:
        s = jnp.where(vis, s, -jnp.inf)
    m_old = st_ref[0:1, :]
    m_new = jnp.maximum(m_old, jnp.max(s, axis=0, keepdims=True))
    p = jnp.exp2((s - m_new) * (MLA_SCALE * LOG2E))
    alpha = jnp.exp2((m_old - m_new) * (MLA_SCALE * LOG2E))
    st_ref[1:2, :] = alpha * st_ref[1:2, :] + jnp.sum(p, axis=0, keepdims=True)
    st_ref[0:1, :] = m_new
    acc_ref[...] = alpha * acc_ref[...] + _dot(ct, p.astype(BF16))


def _mla_kernel(q_ref, c_ref, ct_ref, o_ref, acc_ref, st_ref, *, nblk):
    kidx = lax.broadcasted_iota(jnp.int32, (BLK, BLK), 0)
    qidx = lax.broadcasted_iota(jnp.int32, (BLK, BLK), 1)
    vis = (kidx // CHUNK) <= (qidx // CHUNK)

    def q_block(i, carry):
        acc_ref[...] = jnp.zeros_like(acc_ref)
        st_ref[0:1, :] = jnp.full((1, BLK), -jnp.inf, F32)
        st_ref[1:2, :] = jnp.zeros((1, BLK), F32)
        rows = pl.ds(pl.multiple_of(i * BLK, BLK), BLK)
        q = q_ref[0, rows, :]
        _mla_block(q, c_ref[rows, :], ct_ref[0, i], acc_ref, st_ref, vis)

        def k_block(j, c):
            _mla_block(q, c_ref[pl.ds(pl.multiple_of(j * BLK, BLK), BLK), :], ct_ref[0, j], acc_ref, st_ref, None)
            return c

        lax.fori_loop(0, i, k_block, 0)
        o = acc_ref[...] * (1.0 / st_ref[1:2, :])
        o_ref[rows, :] = o.T.astype(BF16)
        return carry

    lax.fori_loop(0, nblk, q_block, 0)


def _mla_prompt(qm, cm, ct, batch, seq_len):
    nblk = seq_len // BLK
    return pl.pallas_call(
        functools.partial(_mla_kernel, nblk=nblk),
        grid=(batch, N_HEADS),
        in_specs=[pl.BlockSpec((1, seq_len, QM_W), lambda b, h: (h, b, 0)),
                  pl.BlockSpec((seq_len, QM_W), lambda b, h: (b, 0)),
                  pl.BlockSpec((1, nblk, KV_LORA, BLK), lambda b, h: (b, 0, 0, 0))],
        out_specs=pl.BlockSpec((seq_len, KV_LORA), lambda b, h: (b, h)),
        out_shape=jax.ShapeDtypeStruct((batch * seq_len, N_HEADS * KV_LORA), BF16),
        scratch_shapes=[pltpu.VMEM((KV_LORA, BLK), F32), pltpu.VMEM((SUBLANES, BLK), F32)],
        compiler_params=_params("parallel", "parallel"), name="mla_prompt",
    )(qm, cm, ct)


def _sbdec_block(q_ref, kblk, vblk, acc_ref, car_ref, wn, vis):
    lane = lax.broadcasted_iota(jnp.int32, (1, LANES), 1)
    for h in range(N_HEADS):
        p, hh = h // 2, h % 2
        cols = slice(p * LANES, (p + 1) * LANES)
        head_lanes = (lane >= hh * SB_HEAD_DIM) & (lane < (hh + 1) * SB_HEAD_DIM)
        kh = jnp.where(head_lanes, kblk[:, cols], 0.0).astype(BF16)
        vh = jnp.where(head_lanes, vblk[:, cols], 0.0).astype(BF16)
        z = _dot_nt(q_ref[:, cols], kh)
        sp = _softplus(z)
        if vis is not None:
            sp = jnp.where(vis, sp, 0.0)
        hi, lo = _split_bf16(sp)
        mm = _dot(hi, wn) + _dot(lo, wn)
        car = car_ref[:, h * LANES:h * LANES + 1]
        a = jnp.exp(z + mm + car)
        if vis is not None:
            a = jnp.where(vis, a, 0.0)
        acc_ref[:, cols] += _dot(a.astype(BF16), vh)
        car_ref[:, h * LANES:(h + 1) * LANES] = jnp.broadcast_to(car + mm[:, 0:1], (car.shape[0], LANES))


def _sbdec_kernel(q_ref, kn_ref, vn_ref, ck_ref, cv_ref, o_ref, acc_ref, car_ref, *, t_new):
    j = pl.program_id(1)

    @pl.when(j == 0)
    def _():
        acc_ref[...] = jnp.zeros_like(acc_ref)
        car_ref[...] = jnp.zeros_like(car_ref)
        qi = lax.broadcasted_iota(jnp.int32, (t_new, t_new), 0)
        ki = lax.broadcasted_iota(jnp.int32, (t_new, t_new), 1)
        wn_new = jnp.where(qi >= ki, -1.0, 0.0).astype(BF16)
        _sbdec_block(q_ref, kn_ref[...], vn_ref[...], acc_ref, car_ref, wn_new, ki < qi)

    kp = lax.broadcasted_iota(jnp.int32, (BLK, BLK), 0)
    kc = lax.broadcasted_iota(jnp.int32, (BLK, BLK), 1)
    wn = jnp.where(kp >= kc, -1.0, 0.0).astype(BF16)
    for sub in reversed(range(DEC_KEYS // BLK)):
        rows = slice(sub * BLK, (sub + 1) * BLK)
        _sbdec_block(q_ref, ck_ref[0, rows, :], cv_ref[0, rows, :], acc_ref, car_ref, wn, None)

    @pl.when(j == pl.num_programs(1) - 1)
    def _():
        o_ref[...] = acc_ref[...].astype(BF16)


def _sb_sample(q, k_new, v_new, cache_k, cache_v, batch, t_new):
    past = cache_k.shape[1]
    assert past % DEC_KEYS == 0
    nkb = past // DEC_KEYS
    tok = lambda: pl.BlockSpec((t_new, SB_W), lambda b, j: (b, 0))
    cache = lambda: pl.BlockSpec((1, DEC_KEYS, SB_W), lambda b, j: (b, nkb - 1 - j, 0))
    return pl.pallas_call(
        functools.partial(_sbdec_kernel, t_new=t_new),
        grid=(batch, nkb),
        in_specs=[tok(), tok(), tok(), cache(), cache()],
        out_specs=tok(),
        out_shape=jax.ShapeDtypeStruct((batch * t_new, SB_W), BF16),
        scratch_shapes=[pltpu.VMEM((t_new, SB_W), F32), pltpu.VMEM((t_new, N_HEADS * LANES), F32)],
        compiler_params=_params("parallel", "arbitrary"), name="sb_sample",
    )(q, k_new, v_new, cache_k, cache_v)


def _mladec_kernel(q_ref, cn_ref, ck_ref, kr_ref, o_ref, *, t_new, past, new_visible):
    q = q_ref[...].reshape(N_HEADS * t_new, QM_W)
    q_lat = q[:, :KV_LORA]
    q_rope = q[:, KV_LORA:KV_LORA + MLA_ROPE]
    c2 = MLA_SCALE * LOG2E

    cn = cn_ref[...]
    s = _dot_nt(q, cn)
    if not new_visible:
        qi = past + lax.broadcasted_iota(jnp.int32, s.shape, 0) % t_new
        ki = past + lax.broadcasted_iota(jnp.int32, s.shape, 1)
        s = jnp.where((ki // CHUNK) <= (qi // CHUNK), s, -jnp.inf)
    m = jnp.max(s, axis=1, keepdims=True)
    p = jnp.exp2((s - m) * c2)
    l = jnp.sum(p, axis=1, keepdims=True)
    acc = _dot(p.astype(BF16), cn[:, :KV_LORA])

    def chunk(c, carry):
        m, l, acc = carry
        rows = pl.ds(pl.multiple_of(c * DEC_KEYS, DEC_KEYS), DEC_KEYS)
        ck = ck_ref[0, rows, :].astype(BF16)
        kr = kr_ref[0, rows, :].astype(BF16)
        s = _dot_nt(q_lat, ck) + _dot_nt(q_rope, kr)
        m_new = jnp.maximum(m, jnp.max(s, axis=1, keepdims=True))
        p = jnp.exp2((s - m_new) * c2)
        alpha = jnp.exp2((m - m_new) * c2)
        return (m_new, alpha * l + jnp.sum(p, axis=1, keepdims=True), alpha * acc + _dot(p.astype(BF16), ck))

    m, l, acc = lax.fori_loop(0, past // DEC_KEYS, chunk, (m, l, acc))
    o = (acc * (1.0 / l)).astype(BF16)
    for h in range(N_HEADS):
        o_ref[:, h * KV_LORA:(h + 1) * KV_LORA] = o[h * t_new:(h + 1) * t_new, :]


def _mla_sample(qm, cm, cache_ckv, cache_kr, batch, t_new):
    past = cache_ckv.shape[1]
    assert past % DEC_KEYS == 0 and past % CHUNK == 0
    q_pos = past + np.arange(t_new)
    new_visible = bool(((q_pos[None, :] // CHUNK) <= (q_pos[:, None] // CHUNK)).all())
    return pl.pallas_call(
        functools.partial(_mladec_kernel, t_new=t_new, past=past, new_visible=new_visible),
        grid=(batch,),
        in_specs=[pl.BlockSpec((N_HEADS, t_new, QM_W), lambda b: (0, b, 0)),
                  pl.BlockSpec((t_new, QM_W), lambda b: (b, 0)),
                  pl.BlockSpec((1, past, KV_LORA), lambda b: (b, 0, 0)),
                  pl.BlockSpec((1, past, MLA_ROPE), lambda b: (b, 0, 0))],
        out_specs=pl.BlockSpec((t_new, N_HEADS * KV_LORA), lambda b: (b, 0)),
        out_shape=jax.ShapeDtypeStruct((batch * t_new, N_HEADS * KV_LORA), BF16),
        compiler_params=_params("parallel"), name="mla_sample",
    )(qm, cm, cache_ckv, cache_kr)


def _mix_kernel(x_ref, gpre_ref, osb_ref, olat_ref, wg_ref, wpsb_ref, wuv_ref, wpmla_ref, wout_ref,
                gpost_ref, o_ref):
    x = x_ref[...]
    hb = _rmsnorm(x, gpre_ref[...]).astype(BF16)
    gates = _dot(hb, wg_ref[...])
    y_a = _dot(osb_ref[...], wpsb_ref[...])
    o_mla = _dot(olat_ref[...], wuv_ref[...]).astype(BF16)
    y_b = _dot(o_mla, wpmla_ref[...])
    mixin = _sigmoid(gates[:, :D_MODEL]) * y_a + _sigmoid(gates[:, D_MODEL:]) * y_b
    mix = _dot(mixin.astype(BF16), wout_ref[...])
    o_ref[...] = x + _rmsnorm(mix, gpost_ref[...])


def _mix(x2d, o_sb, o_lat, lw):
    n = x2d.shape[0]
    tm = BLK
    tok = lambda w: pl.BlockSpec((tm, w), lambda i: (i, 0))
    return pl.pallas_call(
        _mix_kernel, grid=(n // tm,),
        in_specs=[tok(D_MODEL), _const_spec((1, D_MODEL)), tok(SB_W), tok(N_HEADS * KV_LORA),
                  _const_spec(lw["w_gates"].shape), _const_spec(lw["w_proj_sb"].shape),
                  _const_spec(lw["w_uv_bd"].shape), _const_spec(lw["w_proj_mla"].shape),
                  _const_spec(lw["w_out"].shape), _const_spec((1, D_MODEL))],
        out_specs=tok(D_MODEL), out_shape=jax.ShapeDtypeStruct((n, D_MODEL), F32),
        compiler_params=_params("parallel"), name="mix",
    )(x2d, lw["g_pre_mix"], o_sb, o_lat, lw["w_gates"], lw["w_proj_sb"], lw["w_uv_bd"], lw["w_proj_mla"],
      lw["w_out"], lw["g_post_mix"])


def _ffn_kernel(x_ref, gpre_ref, wup_ref, cw_ref, past_ref, wdown_ref, gpost_ref, wpg_ref, p_ref, wpp_ref,
                o_ref, conv_ref, u_scr, f_scr, *, tm):
    t = pl.program_id(1)

    @pl.when(t == 0)
    def _():
        u_scr[:, SUBLANES - (CONV_W - 1):SUBLANES, :] = past_ref[0]

    x = x_ref[...]
    hb = _rmsnorm(x, gpre_ref[...]).astype(BF16)
    f_scr[...] = jnp.zeros_like(f_scr)

    def conv(c):
        u_scr[c, SUBLANES:, :] = _dot(hb, wup_ref[c])
        cw = cw_ref[c]
        uc = cw[3:4, :]
        for i in range(CONV_W):
            uc = uc + u_scr[c, SUBLANES - (CONV_W - 1) + i:SUBLANES - (CONV_W - 1) + i + tm, :] * cw[i:i + 1, :]
        conv_ref[0, c] = u_scr[c, SUBLANES + tm - (CONV_W - 1):SUBLANES + tm, :]
        u_scr[c, 0:SUBLANES, :] = u_scr[c, tm:tm + SUBLANES, :]
        return uc

    def chunk(c, carry):
        act = _gelu_tanh(conv(c)) * conv(N_FF_CHUNKS + c)
        f_scr[...] += _dot(act.astype(BF16), wdown_ref[c])
        return carry

    lax.fori_loop(0, N_FF_CHUNKS, chunk, 0)
    x2 = x + _rmsnorm(f_scr[...], gpost_ref[...])
    ple = _sigmoid(_dot(x2.astype(BF16), wpg_ref[...])) * _dot(p_ref[...].astype(BF16), wpp_ref[...])
    o_ref[...] = x2 + ple


def _ffn(x2d, p2d, conv_past, batch, seq_len, lw):
    tm = min(BLK, seq_len)
    nt = seq_len // tm
    nch = 2 * N_FF_CHUNKS
    tok = lambda w: pl.BlockSpec((tm, w), lambda b, t: (b * nt + t, 0))
    state = pl.BlockSpec((1, nch, CONV_W - 1, FF_CHUNK), lambda b, t: (b, 0, 0, 0))
    return pl.pallas_call(
        functools.partial(_ffn_kernel, tm=tm),
        grid=(batch, nt),
        in_specs=[tok(D_MODEL), _const_spec((1, D_MODEL)), _const_spec(lw["w_up"].shape),
                  _const_spec(lw["conv"].shape), state, _const_spec(lw["w_down"].shape),
                  _const_spec((1, D_MODEL)), _const_spec(lw["w_ple_gate"].shape), tok(D_PLE),
                  _const_spec(lw["w_ple_proj"].shape)],
        out_specs=[tok(D_MODEL), state],
        out_shape=[jax.ShapeDtypeStruct(x2d.shape, F32), jax.ShapeDtypeStruct(conv_past.shape, F32)],
        scratch_shapes=[pltpu.VMEM((nch, SUBLANES + tm, FF_CHUNK), F32), pltpu.VMEM((tm, D_MODEL), F32)],
        compiler_params=_params("arbitrary", "arbitrary"), name="ffn",
    )(x2d, lw["g_pre_ffn"], lw["w_up"], lw["conv"], conv_past, lw["w_down"], lw["g_post_ffn"],
      lw["w_ple_gate"], p2d, lw["w_ple_proj"])


def _prep_layer(w_in, ckv_gain, w_uk, w_uv, w_proj_sb, w_proj_mla, w_out, g_pre_mix, g_post_mix, g_pre_ffn,
                g_post_ffn, w_up, conv_w, conv_b, w_down, w_ple_gate, w_ple_proj):
    o_qm = 3 * SB_W
    o_ckv = o_qm + N_HEADS * (MLA_NOPE + MLA_ROPE)
    o_kr = o_ckv + KV_LORA
    o_g = o_kr + MLA_ROPE
    wqm = w_in[:, o_qm:o_ckv].reshape(D_MODEL, N_HEADS, MLA_NOPE + MLA_ROPE)
    w_qn = wqm[:, :, :MLA_NOPE].reshape(D_MODEL, N_HEADS * MLA_NOPE)
    w_qr = jnp.concatenate([wqm[:, :, MLA_NOPE:MLA_NOPE + ROPE_HALF].reshape(D_MODEL, LANES),
                            wqm[:, :, MLA_NOPE + ROPE_HALF:].reshape(D_MODEL, LANES)], axis=1)
    w_kr = w_in[:, o_kr:o_g]
    w_kx = jnp.zeros((D_MODEL, 2 * LANES), F32)
    w_kx = w_kx.at[:, :ROPE_HALF].set(w_kr[:, :ROPE_HALF]).at[:, LANES:LANES + ROPE_HALF].set(w_kr[:, ROPE_HALF:])
    wa = jnp.zeros((N_HEADS, LANES, QM_W), F32)
    wb = np.zeros((N_HEADS, 2 * LANES, QM_W), np.float32)
    for h in range(N_HEADS):
        r0 = (h % 2) * MLA_NOPE
        wa = wa.at[h, r0:r0 + MLA_NOPE, :KV_LORA].set(w_uk[:, h, :].T)
        for i in range(ROPE_HALF):
            wb[h, h * ROPE_HALF + i, KV_LORA + i] = 1.0
            wb[h, LANES + h * ROPE_HALF + i, KV_LORA + ROPE_HALF + i] = 1.0
    w_uv_bd = jnp.zeros((N_HEADS * KV_LORA, N_HEADS * MLA_V), F32)
    for h in range(N_HEADS):
        w_uv_bd = w_uv_bd.at[h * KV_LORA:(h + 1) * KV_LORA, h * MLA_V:(h + 1) * MLA_V].set(w_uv[:, h, :])
    nch = 2 * N_FF_CHUNKS
    conv = jnp.concatenate([conv_w, conv_b[None, :], jnp.zeros((SUBLANES - CONV_W - 1, 2 * D_FF), F32)], axis=0)
    row = lambda g: g.reshape(1, -1)
    return dict(
        w_qkv=w_in[:, :o_qm].astype(BF16), w_qn=w_qn.astype(BF16), w_qr=w_qr.astype(BF16),
        w_ckv=w_in[:, o_ckv:o_kr].astype(BF16), w_kx=w_kx.astype(BF16), ckv_gain=row(ckv_gain),
        wa=wa.astype(BF16), wb=jnp.asarray(wb, BF16), eye=jnp.eye(LANES, dtype=BF16),
        w_gates=w_in[:, o_g:].astype(BF16), w_proj_sb=w_proj_sb.astype(BF16), w_uv_bd=w_uv_bd.astype(BF16),
        w_proj_mla=w_proj_mla.astype(BF16), w_out=w_out.astype(BF16),
        g_pre_mix=row(g_pre_mix), g_post_mix=row(g_post_mix), g_pre_ffn=row(g_pre_ffn), g_post_ffn=row(g_post_ffn),
        w_up=w_up.reshape(D_MODEL, nch, FF_CHUNK).transpose(1, 0, 2).astype(BF16),
        conv=conv.reshape(SUBLANES, nch, FF_CHUNK).transpose(1, 0, 2),
        w_down=w_down.reshape(N_FF_CHUNKS, FF_CHUNK, D_MODEL).astype(BF16),
        w_ple_gate=w_ple_gate.astype(BF16), w_ple_proj=w_ple_proj.astype(BF16))


def _rope_tables(pos, rows):
    freqs = jnp.power(ROPE_THETA, -jnp.arange(ROPE_HALF, dtype=F32) / ROPE_HALF)
    ang = pos.astype(F32)[:, None] * freqs[None, :]
    reps = max(1, rows // pos.shape[0])
    tile = lambda a: jnp.tile(a, (reps, LANES // ROPE_HALF))
    return tile(jnp.cos(ang)), tile(jnp.sin(ang))


def _conv_to_chunks(state):
    b = state.shape[0]
    return state.reshape(b, CONV_W - 1, 2 * N_FF_CHUNKS, FF_CHUNK).transpose(0, 2, 1, 3)


def _conv_from_chunks(state):
    b = state.shape[0]
    return state.transpose(0, 2, 1, 3).reshape(b, CONV_W - 1, 2 * D_FF)


def kernel(x_prompt, x_sample, p_prompt, p_sample, cache_sb_k, cache_sb_v, cache_mla_ckv, cache_mla_krope,
           state_ffn_conv, w_in, ckv_gain, w_uk, w_uv, w_proj_sb, w_proj_mla, w_out, g_pre_mix, g_post_mix,
           g_pre_ffn, g_post_ffn, w_up, conv_w, conv_b, w_down, w_ple_gate, w_ple_proj):
    bp, tp, _ = x_prompt.shape
    bs, ts, _ = x_sample.shape
    depth = w_in.shape[0]
    past = cache_sb_k.shape[2]
    assert tp % BLK == 0 and (bs * ts) % BLK == 0 and BLK % ts == 0
    tabs_p = _rope_tables(jnp.arange(tp, dtype=jnp.int32), BLK)
    tabs_s = _rope_tables(past + jnp.arange(ts, dtype=jnp.int32), BLK)
    xp = x_prompt.reshape(bp * tp, D_MODEL)
    xs = x_sample.reshape(bs * ts, D_MODEL)
    zero_conv = jnp.zeros((bp, 2 * N_FF_CHUNKS, CONV_W - 1, FF_CHUNK), F32)
    st_p, st_s = [], []
    for i in range(depth):
        lw = _prep_layer(w_in[i], ckv_gain[i], w_uk[i], w_uv[i], w_proj_sb[i], w_proj_mla[i], w_out[i],
                         g_pre_mix[i], g_post_mix[i], g_pre_ffn[i], g_post_ffn[i], w_up[i], conv_w[i],
                         conv_b[i], w_down[i], w_ple_gate[i], w_ple_proj[i])
        q, k_st, k_bf, v_st, vt, qm, cm, ct, c_st, r_st = _proj(xp, tp, tabs_p, lw, transposed=True)
        o_sb = _sb_prompt(q, k_bf, vt, bp, tp)
        o_lat = _mla_prompt(qm, cm, ct, bp, tp)
        x1 = _mix(xp, o_sb, o_lat, lw)
        xp, conv_p = _ffn(x1, p_prompt[i].reshape(bp * tp, D_PLE), zero_conv, bp, tp, lw)
        st_p.append((k_st.reshape(bp, tp, N_HEADS, SB_HEAD_DIM), v_st.reshape(bp, tp, N_HEADS, SB_HEAD_DIM),
                     c_st.reshape(bp, tp, KV_LORA), r_st.reshape(bp, tp, MLA_ROPE), _conv_from_chunks(conv_p)))
        q, k_st, _, v_st, qm, cm, c_st, r_st = _proj(xs, ts, tabs_s, lw, transposed=False)
        o_sb = _sb_sample(q, k_st, v_st, cache_sb_k[i].reshape(bs, past, SB_W),
                          cache_sb_v[i].reshape(bs, past, SB_W), bs, ts)
        o_lat = _mla_sample(qm, cm, cache_mla_ckv[i], cache_mla_krope[i], bs, ts)
        x1 = _mix(xs, o_sb, o_lat, lw)
        xs, conv_s = _ffn(x1, p_sample[i].reshape(bs * ts, D_PLE), _conv_to_chunks(state_ffn_conv[i]), bs, ts, lw)
        st_s.append((k_st.reshape(bs, ts, N_HEADS, SB_HEAD_DIM), v_st.reshape(bs, ts, N_HEADS, SB_HEAD_DIM),
                     c_st.reshape(bs, ts, KV_LORA), r_st.reshape(bs, ts, MLA_ROPE), _conv_from_chunks(conv_s)))
    stack = lambda sts, k: jnp.stack([s[k] for s in sts])
    return ((xp.reshape(bp, tp, D_MODEL), xs.reshape(bs, ts, D_MODEL))
            + tuple(stack(st_p, k) for k in range(5)) + tuple(stack(st_s, k) for k in range(5)))
```

```python
import functools
import math

import jax
import jax.numpy as jnp
import numpy as np
from jax import lax
from jax.experimental import pallas as pl
from jax.experimental.pallas import tpu as pltpu

D_MODEL = 1024
N_HEADS = 8
SB_HEAD_DIM = 64
SB_W = N_HEADS * SB_HEAD_DIM
MLA_NOPE = 64
MLA_ROPE = 32
ROPE_HALF = MLA_ROPE // 2
KV_LORA = 128
MLA_V = 64
D_FF = 2816
CONV_W = 3
CHUNK = 64
D_PLE = 256
ROPE_THETA = 10000.0
EPS = 1e-6
SB_SCALE = SB_HEAD_DIM ** -0.5
MLA_SCALE = (MLA_NOPE + MLA_ROPE) ** -0.5
LOG2E = math.log2(math.e)

LANES = 128
SUBLANES = 8
MXU_DIM = 256
VMEM_LIMIT_BYTES = 56 * 1024 * 1024

BLK = MXU_DIM
FF_CHUNK = MXU_DIM
N_FF_CHUNKS = D_FF // FF_CHUNK
FF_DOWN_GROUP = 4
QM_W = 2 * LANES
DEC_KEYS = 1024
MLA_DEC_KEYS = 512
HEAD_GROUP = MXU_DIM // SB_HEAD_DIM
N_GROUPS = N_HEADS // HEAD_GROUP

BF16 = jnp.bfloat16
F32 = jnp.float32


def _dot(a, b):
    return jnp.dot(a, b, preferred_element_type=F32)


def _dot_nt(a, b):
    return lax.dot_general(a, b, (((1,), (1,)), ((), ())), preferred_element_type=F32)


def _rmsnorm(x, g):
    ms = jnp.mean(x * x, axis=-1, keepdims=True)
    return x * lax.rsqrt(ms + EPS) * g


def _sigmoid(x):
    return 1.0 / (1.0 + jnp.exp(-x))


def _softplus(z):
    return jnp.maximum(z, 0.0) + jnp.log(1.0 + jnp.exp(-jnp.abs(z)))


def _gelu_tanh(x):
    c = math.sqrt(2.0 / math.pi)
    return 0.5 * x * (1.0 + jnp.tanh(c * (x + 0.044715 * (x * x * x))))


def _split_bf16(x):
    hi = x.astype(BF16)
    lo = (x - hi.astype(F32)).astype(BF16)
    return hi, lo


def _params(*sem):
    return pltpu.CompilerParams(dimension_semantics=sem, vmem_limit_bytes=VMEM_LIMIT_BYTES)


def _const_spec(shape):
    return pl.BlockSpec(shape, lambda *_: (0,) * len(shape), pipeline_mode=pl.Buffered(1))


def _proj_kernel(x_ref, g_ref, cos_ref, sin_ref, wqkv_ref, wqn_ref, wqr_ref, wckv_ref, wkx_ref,
                 gain_ref, wa_ref, wb_ref, eye_ref, *out_refs, transposed):
    if transposed:
        (q_ref, kst_ref, kbf_ref, vst_ref, vt_ref, qm_ref, cm_ref, ct_ref, cst_ref, rst_ref) = out_refs
    else:
        (q_ref, kst_ref, kbf_ref, vst_ref, qm_ref, cm_ref, cst_ref, rst_ref) = out_refs
    hb = _rmsnorm(x_ref[...], g_ref[...]).astype(BF16)

    qkv = _dot(hb, wqkv_ref[...])
    q_ref[...] = (qkv[:, :SB_W] * SB_SCALE).astype(BF16)
    k = qkv[:, SB_W:2 * SB_W]
    v = qkv[:, 2 * SB_W:]
    kst_ref[...] = k
    vst_ref[...] = v
    kbf_ref[...] = k.astype(BF16)
    if transposed:
        vb = v.astype(BF16)
        for p in range(SB_W // LANES):
            vt_ref[0, 0, p * LANES:(p + 1) * LANES, :] = _dot_nt(
                eye_ref[...], vb[:, p * LANES:(p + 1) * LANES]).astype(BF16)

    cos = cos_ref[...]
    sin = sin_ref[...]
    qn = _dot(hb, wqn_ref[...]).astype(BF16)
    qr = _dot(hb, wqr_ref[...])
    x1, x2 = qr[:, :LANES], qr[:, LANES:]
    r12 = jnp.concatenate([x1 * cos - x2 * sin, x1 * sin + x2 * cos], axis=1).astype(BF16)
    for h in range(N_HEADS):
        p = h // 2
        qm_ref[h] = (_dot(qn[:, p * LANES:(p + 1) * LANES], wa_ref[h]) + _dot(r12, wb_ref[h])).astype(BF16)

    ckv = _rmsnorm(_dot(hb, wckv_ref[...]), gain_ref[...])
    cst_ref[...] = ckv
    kx = _dot(hb, wkx_ref[...])
    k1, k2 = kx[:, :LANES], kx[:, LANES:]
    kr1 = k1 * cos - k2 * sin
    kr2 = k1 * sin + k2 * cos
    lane = lax.broadcasted_iota(jnp.int32, kr1.shape, 1)
    kr = jnp.where(lane < ROPE_HALF, kr1, pltpu.roll(kr2, ROPE_HALF, axis=1))
    rst_ref[...] = kr[:, :MLA_ROPE]
    ckv_b = ckv.astype(BF16)
    cm_ref[...] = jnp.concatenate([ckv_b, kr.astype(BF16)], axis=1)
    if transposed:
        ct_ref[0, 0] = _dot_nt(eye_ref[...], ckv_b).astype(BF16)


def _proj(x2d, seq_len, tabs, lw, *, transposed):
    n = x2d.shape[0]
    tm = BLK
    nt = n // tm
    cos, sin = tabs
    tab_blocks = cos.shape[0] // tm
    tok = lambda w: pl.BlockSpec((tm, w), lambda i: (i, 0))
    tab = pl.BlockSpec((tm, LANES), lambda i: (i % tab_blocks, 0))
    in_specs = [tok(D_MODEL), _const_spec((1, D_MODEL)), tab, tab,
                _const_spec(lw["w_qkv"].shape), _const_spec(lw["w_qn"].shape), _const_spec(lw["w_qr"].shape),
                _const_spec(lw["w_ckv"].shape), _const_spec(lw["w_kx"].shape), _const_spec((1, KV_LORA)),
                _const_spec(lw["wa"].shape), _const_spec(lw["wb"].shape), _const_spec((LANES, LANES))]
    out_shape = [jax.ShapeDtypeStruct((n, SB_W), BF16), jax.ShapeDtypeStruct((n, SB_W), F32),
                 jax.ShapeDtypeStruct((n, SB_W), BF16), jax.ShapeDtypeStruct((n, SB_W), F32)]
    out_specs = [tok(SB_W), tok(SB_W), tok(SB_W), tok(SB_W)]
    if transposed:
        assert seq_len % tm == 0
        nb = seq_len // tm
        out_shape.append(jax.ShapeDtypeStruct((n // seq_len, nb, SB_W, tm), BF16))
        out_specs.append(pl.BlockSpec((1, 1, SB_W, tm), lambda i: (i // nb, i % nb, 0, 0)))
    out_shape += [jax.ShapeDtypeStruct((N_HEADS, n, QM_W), BF16), jax.ShapeDtypeStruct((n, QM_W), BF16)]
    out_specs += [pl.BlockSpec((N_HEADS, tm, QM_W), lambda i: (0, i, 0)), tok(QM_W)]
    if transposed:
        out_shape.append(jax.ShapeDtypeStruct((n // seq_len, nb, KV_LORA, tm), BF16))
        out_specs.append(pl.BlockSpec((1, 1, KV_LORA, tm), lambda i: (i // nb, i % nb, 0, 0)))
    out_shape += [jax.ShapeDtypeStruct((n, KV_LORA), F32), jax.ShapeDtypeStruct((n, MLA_ROPE), F32)]
    out_specs += [tok(KV_LORA), tok(MLA_ROPE)]
    return pl.pallas_call(
        functools.partial(_proj_kernel, transposed=transposed),
        grid=(nt,), in_specs=in_specs, out_specs=out_specs, out_shape=out_shape,
        compiler_params=_params("parallel"), name="proj_t" if transposed else "proj",
    )(x2d, lw["g_pre_mix"], cos, sin, lw["w_qkv"], lw["w_qn"], lw["w_qr"], lw["w_ckv"], lw["w_kx"],
      lw["ckv_gain"], lw["wa"], lw["wb"], lw["eye"])


def _sb_block(q_ref, k_ref, vt_ref, qrows, krows, j, acc_ref, car_ref, umat, vis):
    lane = lax.broadcasted_iota(jnp.int32, (1, LANES), 1)
    heads = range(N_HEADS)
    zs = []
    for h in heads:
        p, hh = h // 2, h % 2
        cols = slice(p * LANES, (p + 1) * LANES)
        k2 = k_ref[0, krows, cols]
        head_lanes = (lane >= hh * SB_HEAD_DIM) & (lane < (hh + 1) * SB_HEAD_DIM)
        kh = jnp.where(head_lanes, k2, jnp.zeros_like(k2))
        zs.append(_dot_nt(kh, q_ref[0, qrows, cols]))
    mms = []
    for h in heads:
        sp = _softplus(zs[h])
        if vis is not None:
            sp = jnp.where(vis, sp, 0.0)
        hi, lo = _split_bf16(sp)
        mms.append(_dot(umat, hi) + _dot(umat, lo))
    for h in heads:
        car = car_ref[h, 0:1, :]
        a = jnp.exp(zs[h] + mms[h] + car)
        if vis is not None:
            a = jnp.where(vis, a, 0.0)
        rows = slice(h * SB_HEAD_DIM, (h + 1) * SB_HEAD_DIM)
        acc_ref[rows, :] += _dot(vt_ref[0, j, rows, :], a.astype(BF16))
        car_ref[h, 0:1, :] = car + mms[h][0:1, :]


def _sb_kernel(q_ref, k_ref, vt_ref, o_ref, acc_ref, car_ref, *, nblk):
    kidx = lax.broadcasted_iota(jnp.int32, (BLK, BLK), 0)
    qidx = lax.broadcasted_iota(jnp.int32, (BLK, BLK), 1)
    umat = jnp.where(qidx >= kidx, -1.0, 0.0).astype(BF16)
    vis = kidx < qidx

    def q_block(i, carry):
        acc_ref[...] = jnp.zeros_like(acc_ref)
        car_ref[...] = jnp.zeros_like(car_ref)
        qrows = pl.ds(pl.multiple_of(i * BLK, BLK), BLK)
        _sb_block(q_ref, k_ref, vt_ref, qrows, qrows, i, acc_ref, car_ref, umat, vis)

        def k_block(jj, c):
            j = i - 1 - jj
            krows = pl.ds(pl.multiple_of(j * BLK, BLK), BLK)
            _sb_block(q_ref, k_ref, vt_ref, qrows, krows, j, acc_ref, car_ref, umat, None)
            return c

        lax.fori_loop(0, i, k_block, 0)
        o_ref[0, qrows, :] = acc_ref[...].T.astype(BF16)
        return carry

    lax.fori_loop(0, nblk, q_block, 0)


def _sb_prompt(q, kbf, vt, batch, seq_len):
    nblk = seq_len // BLK
    q3 = q.reshape(batch, seq_len, SB_W)
    k3 = kbf.reshape(batch, seq_len, SB_W)
    row = pl.BlockSpec((1, seq_len, SB_W), lambda b: (b, 0, 0))
    out = pl.pallas_call(
        functools.partial(_sb_kernel, nblk=nblk),
        grid=(batch,),
        in_specs=[row, row, pl.BlockSpec((1, nblk, SB_W, BLK), lambda b: (b, 0, 0, 0))],
        out_specs=row,
        out_shape=jax.ShapeDtypeStruct((batch, seq_len, SB_W), BF16),
        scratch_shapes=[pltpu.VMEM((SB_W, BLK), F32), pltpu.VMEM((N_HEADS, SUBLANES, BLK), F32)],
        compiler_params=_params("parallel"), name="sb_prompt",
    )(q3, k3, vt)
    return out.reshape(batch * seq_len, SB_W)


def _mla_block(q_ref, c_ref, ct_ref, qrows, krows, j, acc_ref, st_ref, vis):
    c = c_ref[krows, :]
    ct = ct_ref[0, j]
    ss = [_dot_nt(c, q_ref[h, qrows, :]) for h in range(N_HEADS)]
    for h in range(N_HEADS):
        s = ss[h]
        if vis is not None:
            s = jnp.where(vis, s, -jnp.inf)
        m_old = st_ref[h, 0:1, :]
        m_new = jnp.maximum(m_old, jnp.max(s, axis=0, keepdims=True))
        p = jnp.exp2((s - m_new) * (MLA_SCALE * LOG2E))
        alpha = jnp.exp2((m_old - m_new) * (MLA_SCALE * LOG2E))
        st_ref[h, 1:2, :] = alpha * st_ref[h, 1:2, :] + jnp.sum(p, axis=0, keepdims=True)
        st_ref[h, 0:1, :] = m_new
        acc_ref[h] = alpha * acc_ref[h] + _dot(ct, p.astype(BF16))


def _mla_kernel(q_ref, c_ref, ct_ref, o_ref, acc_ref, st_ref, *, nblk):
    kidx = lax.broadcasted_iota(jnp.int32, (BLK, BLK), 0)
    qidx = lax.broadcasted_iota(jnp.int32, (BLK, BLK), 1)
    vis = (kidx // CHUNK) <= (qidx // CHUNK)

    def q_block(i, carry):
        acc_ref[...] = jnp.zeros_like(acc_ref)
        for h in range(N_HEADS):
            st_ref[h, 0:1, :] = jnp.full((1, BLK), -jnp.inf, F32)
            st_ref[h, 1:2, :] = jnp.zeros((1, BLK), F32)
        rows = pl.ds(pl.multiple_of(i * BLK, BLK), BLK)
        _mla_block(q_ref, c_ref, ct_ref, rows, rows, i, acc_ref, st_ref, vis)

        def k_block(j, c):
            krows = pl.ds(pl.multiple_of(j * BLK, BLK), BLK)
            _mla_block(q_ref, c_ref, ct_ref, rows, krows, j, acc_ref, st_ref, None)
            return c

        lax.fori_loop(0, i, k_block, 0)
        for h in range(N_HEADS):
            o = acc_ref[h] * (1.0 / st_ref[h, 1:2, :])
            o_ref[rows, h * KV_LORA:(h + 1) * KV_LORA] = o.T.astype(BF16)
        return carry

    lax.fori_loop(0, nblk, q_block, 0)


def _mla_prompt(qm, cm, ct, batch, seq_len):
    nblk = seq_len // BLK
    return pl.pallas_call(
        functools.partial(_mla_kernel, nblk=nblk),
        grid=(batch,),
        in_specs=[pl.BlockSpec((N_HEADS, seq_len, QM_W), lambda b: (0, b, 0)),
                  pl.BlockSpec((seq_len, QM_W), lambda b: (b, 0)),
                  pl.BlockSpec((1, nblk, KV_LORA, BLK), lambda b: (b, 0, 0, 0))],
        out_specs=pl.BlockSpec((seq_len, N_HEADS * KV_LORA), lambda b: (b, 0)),
        out_shape=jax.ShapeDtypeStruct((batch * seq_len, N_HEADS * KV_LORA), BF16),
        scratch_shapes=[pltpu.VMEM((N_HEADS, KV_LORA, BLK), F32), pltpu.VMEM((N_HEADS, SUBLANES, BLK), F32)],
        compiler_params=_params("parallel"), name="mla_prompt",
    )(qm, cm, ct)


def _sbdec_blocks(qbd_ref, k_of, v_of, nblocks, acc_ref, car_ref, wn, vis):
    work = [(b, g) for b in range(nblocks) for g in range(N_GROUPS)]
    zs = {w: _dot_nt(qbd_ref[w[1]], k_of(*w).astype(BF16)) for w in work}
    mms = {}
    for w in work:
        sp = _softplus(zs[w])
        if vis is not None:
            sp = jnp.where(vis, sp, 0.0)
        hi, lo = _split_bf16(sp)
        mm2 = _dot(jnp.concatenate([hi, lo], axis=0), wn)
        nrow = sp.shape[0]
        mms[w] = mm2[:nrow] + mm2[nrow:]
    for g in range(N_GROUPS):
        car = car_ref[g][:, 0:1]
        acc = acc_ref[g]
        for b in range(nblocks):
            a = jnp.exp(zs[b, g] + mms[b, g] + car)
            if vis is not None:
                a = jnp.where(vis, a, 0.0)
            acc = acc + _dot(a.astype(BF16), v_of(b, g).astype(BF16))
            car = car + mms[b, g][:, 0:1]
        acc_ref[g] = acc
        car_ref[g] = jnp.broadcast_to(car, (car.shape[0], LANES))


def _sbdec_kernel(q_ref, kn_ref, vn_ref, ck_ref, cv_ref, o_ref, qbd_ref, acc_ref, car_ref, *, t_new):
    j = pl.program_id(1)
    nrow = HEAD_GROUP * t_new
    gcols = lambda g: slice(g * MXU_DIM, (g + 1) * MXU_DIM)

    @pl.when(j == 0)
    def _():
        acc_ref[...] = jnp.zeros_like(acc_ref)
        car_ref[...] = jnp.zeros_like(car_ref)
        row_head = lax.broadcasted_iota(jnp.int32, (nrow, MXU_DIM), 0) // t_new
        col_head = lax.broadcasted_iota(jnp.int32, (nrow, MXU_DIM), 1) // SB_HEAD_DIM
        for g in range(N_GROUPS):
            qg = jnp.concatenate([q_ref[:, gcols(g)]] * HEAD_GROUP, axis=0)
            qbd_ref[g] = jnp.where(row_head == col_head, qg, jnp.zeros_like(qg))
        kp = lax.broadcasted_iota(jnp.int32, (t_new, t_new), 0)
        kc = lax.broadcasted_iota(jnp.int32, (t_new, t_new), 1)
        wn_new = jnp.where(kp >= kc, -1.0, 0.0).astype(BF16)
        qi = lax.rem(lax.broadcasted_iota(jnp.int32, (nrow, t_new), 0), t_new)
        ki = lax.broadcasted_iota(jnp.int32, (nrow, t_new), 1)
        _sbdec_blocks(qbd_ref, lambda b, g: kn_ref[:, gcols(g)], lambda b, g: vn_ref[:, gcols(g)], 1,
                      acc_ref, car_ref, wn_new, ki < qi)

    kp = lax.broadcasted_iota(jnp.int32, (BLK, BLK), 0)
    kc = lax.broadcasted_iota(jnp.int32, (BLK, BLK), 1)
    wn = jnp.where(kp >= kc, -1.0, 0.0).astype(BF16)
    nsub = DEC_KEYS // BLK
    rows = lambda b: slice((nsub - 1 - b) * BLK, (nsub - b) * BLK)
    _sbdec_blocks(qbd_ref, lambda b, g: ck_ref[0, rows(b), gcols(g)], lambda b, g: cv_ref[0, rows(b), gcols(g)],
                  nsub, acc_ref, car_ref, wn, None)

    @pl.when(j == pl.num_programs(1) - 1)
    def _():
        col_head = lax.broadcasted_iota(jnp.int32, (t_new, MXU_DIM), 1) // SB_HEAD_DIM
        for g in range(N_GROUPS):
            out = jnp.zeros((t_new, MXU_DIM), F32)
            for hh in range(HEAD_GROUP):
                out = jnp.where(col_head == hh, acc_ref[g, hh * t_new:(hh + 1) * t_new, :], out)
            o_ref[:, gcols(g)] = out.astype(BF16)


def _sb_sample(q, k_new, v_new, cache_k, cache_v, batch, t_new):
    past = cache_k.shape[1]
    assert past % DEC_KEYS == 0
    nkb = past // DEC_KEYS
    nrow = HEAD_GROUP * t_new
    tok = lambda: pl.BlockSpec((t_new, SB_W), lambda b, j: (b, 0))
    cache = lambda: pl.BlockSpec((1, DEC_KEYS, SB_W), lambda b, j: (b, nkb - 1 - j, 0))
    return pl.pallas_call(
        functools.partial(_sbdec_kernel, t_new=t_new),
        grid=(batch, nkb),
        in_specs=[tok(), tok(), tok(), cache(), cache()],
        out_specs=tok(),
        out_shape=jax.ShapeDtypeStruct((batch * t_new, SB_W), BF16),
        scratch_shapes=[pltpu.VMEM((N_GROUPS, nrow, MXU_DIM), BF16), pltpu.VMEM((N_GROUPS, nrow, MXU_DIM), F32),
                        pltpu.VMEM((N_GROUPS, nrow, LANES), F32)],
        compiler_params=_params("parallel", "arbitrary"), name="sb_sample",
    )(q, k_new, v_new, cache_k, cache_v)


def _mladec_kernel(q_ref, cn_ref, ck_ref, kr_ref, o_ref, *, t_new, past, new_visible):
    q = q_ref[...].reshape(N_HEADS * t_new, QM_W)
    q_lat = q[:, :KV_LORA]
    q_rope = q[:, KV_LORA:KV_LORA + MLA_ROPE]
    c2 = MLA_SCALE * LOG2E

    cn = cn_ref[...]
    s = _dot_nt(q, cn)
    if not new_visible:
        qi = past + lax.broadcasted_iota(jnp.int32, s.shape, 0) % t_new
        ki = past + lax.broadcasted_iota(jnp.int32, s.shape, 1)
        s = jnp.where((ki // CHUNK) <= (qi // CHUNK), s, -jnp.inf)
    m = jnp.max(s, axis=1, keepdims=True)
    p = jnp.exp2((s - m) * c2)
    l = jnp.sum(p, axis=1, keepdims=True)
    acc = _dot(p.astype(BF16), cn[:, :KV_LORA])

    def chunk(c, carry):
        m, l, acc = carry
        rows = pl.ds(pl.multiple_of(c * MLA_DEC_KEYS, MLA_DEC_KEYS), MLA_DEC_KEYS)
        ck = ck_ref[0, rows, :].astype(BF16)
        kr = kr_ref[0, rows, :].astype(BF16)
        s = _dot_nt(q_lat, ck) + _dot_nt(q_rope, kr)
        m_new = jnp.maximum(m, jnp.max(s, axis=1, keepdims=True))
        p = jnp.exp2((s - m_new) * c2)
        alpha = jnp.exp2((m - m_new) * c2)
        return (m_new, alpha * l + jnp.sum(p, axis=1, keepdims=True), alpha * acc + _dot(p.astype(BF16), ck))

    m, l, acc = lax.fori_loop(0, past // MLA_DEC_KEYS, chunk, (m, l, acc))
    o = (acc * (1.0 / l)).astype(BF16)
    for h in range(N_HEADS):
        o_ref[:, h * KV_LORA:(h + 1) * KV_LORA] = o[h * t_new:(h + 1) * t_new, :]


def _mla_sample(qm, cm, cache_ckv, cache_kr, batch, t_new):
    past = cache_ckv.shape[1]
    assert past % MLA_DEC_KEYS == 0 and past % CHUNK == 0
    q_pos = past + np.arange(t_new)
    new_visible = bool(((q_pos[None, :] // CHUNK) <= (q_pos[:, None] // CHUNK)).all())
    return pl.pallas_call(
        functools.partial(_mladec_kernel, t_new=t_new, past=past, new_visible=new_visible),
        grid=(batch,),
        in_specs=[pl.BlockSpec((N_HEADS, t_new, QM_W), lambda b: (0, b, 0)),
                  pl.BlockSpec((t_new, QM_W), lambda b: (b, 0)),
                  pl.BlockSpec((1, past, KV_LORA), lambda b: (b, 0, 0)),
                  pl.BlockSpec((1, past, MLA_ROPE), lambda b: (b, 0, 0))],
        out_specs=pl.BlockSpec((t_new, N_HEADS * KV_LORA), lambda b: (b, 0)),
        out_shape=jax.ShapeDtypeStruct((batch * t_new, N_HEADS * KV_LORA), BF16),
        compiler_params=_params("parallel"), name="mla_sample",
    )(qm, cm, cache_ckv, cache_kr)


def _mix_kernel(x_ref, gpre_ref, osb_ref, olat_ref, wg_ref, wpsb_ref, wuv_ref, wpmla_ref, wout_ref,
                gpost_ref, o_ref):
    x = x_ref[...]
    hb = _rmsnorm(x, gpre_ref[...]).astype(BF16)
    gates = _dot(hb, wg_ref[...])
    y_a = _dot(osb_ref[...], wpsb_ref[...])
    o_mla = _dot(olat_ref[...], wuv_ref[...]).astype(BF16)
    y_b = _dot(o_mla, wpmla_ref[...])
    mixin = _sigmoid(gates[:, :D_MODEL]) * y_a + _sigmoid(gates[:, D_MODEL:]) * y_b
    mix = _dot(mixin.astype(BF16), wout_ref[...])
    o_ref[...] = x + _rmsnorm(mix, gpost_ref[...])


def _mix(x2d, o_sb, o_lat, lw):
    n = x2d.shape[0]
    tm = BLK
    tok = lambda w: pl.BlockSpec((tm, w), lambda i: (i, 0))
    return pl.pallas_call(
        _mix_kernel, grid=(n // tm,),
        in_specs=[tok(D_MODEL), _const_spec((1, D_MODEL)), tok(SB_W), tok(N_HEADS * KV_LORA),
                  _const_spec(lw["w_gates"].shape), _const_spec(lw["w_proj_sb"].shape),
                  _const_spec(lw["w_uv_bd"].shape), _const_spec(lw["w_proj_mla"].shape),
                  _const_spec(lw["w_out"].shape), _const_spec((1, D_MODEL))],
        out_specs=tok(D_MODEL), out_shape=jax.ShapeDtypeStruct((n, D_MODEL), F32),
        compiler_params=_params("parallel"), name="mix",
    )(x2d, lw["g_pre_mix"], o_sb, o_lat, lw["w_gates"], lw["w_proj_sb"], lw["w_uv_bd"], lw["w_proj_mla"],
      lw["w_out"], lw["g_post_mix"])


def _ffn_kernel(x_ref, gpre_ref, wup_ref, cw_ref, past_ref, wdown_ref, gpost_ref, wpg_ref, p_ref, wpp_ref,
                o_ref, conv_ref, u_scr, act_scr, *, tm):
    t = pl.program_id(1)

    @pl.when(t == 0)
    def _():
        u_scr[:, SUBLANES - (CONV_W - 1):SUBLANES, :] = past_ref[0]

    x = x_ref[...]
    hb = _rmsnorm(x, gpre_ref[...]).astype(BF16)

    def up(c):
        for cc in (c, N_FF_CHUNKS + c):
            u_scr[cc, SUBLANES:, :] = _dot(hb, wup_ref[cc])

    def conv(c):
        cw = cw_ref[c]
        uc = cw[3:4, :]
        for i in range(CONV_W):
            uc = uc + u_scr[c, SUBLANES - (CONV_W - 1) + i:SUBLANES - (CONV_W - 1) + i + tm, :] * cw[i:i + 1, :]
        conv_ref[0, c] = u_scr[c, SUBLANES + tm - (CONV_W - 1):SUBLANES + tm, :]
        u_scr[c, 0:SUBLANES, :] = u_scr[c, tm:tm + SUBLANES, :]
        return uc

    up(0)
    f = None
    for c in range(N_FF_CHUNKS):
        if c + 1 < N_FF_CHUNKS:
            up(c + 1)
        act = _gelu_tanh(conv(c)) * conv(N_FF_CHUNKS + c)
        act_scr[:, c * FF_CHUNK:(c + 1) * FF_CHUNK] = act.astype(BF16)
        if (c + 1) % FF_DOWN_GROUP == 0 or c + 1 == N_FF_CHUNKS:
            k0 = (c // FF_DOWN_GROUP) * FF_DOWN_GROUP * FF_CHUNK
            k1 = (c + 1) * FF_CHUNK
            part = _dot(act_scr[:, k0:k1], wdown_ref[k0:k1, :])
            f = part if f is None else f + part
    x2 = x + _rmsnorm(f, gpost_ref[...])
    ple = _sigmoid(_dot(x2.astype(BF16), wpg_ref[...])) * _dot(p_ref[...].astype(BF16), wpp_ref[...])
    o_ref[...] = x2 + ple


def _ffn(x2d, p2d, conv_past, batch, seq_len, lw):
    tm = min(BLK, seq_len)
    nt = seq_len // tm
    nch = 2 * N_FF_CHUNKS
    tok = lambda w: pl.BlockSpec((tm, w), lambda b, t: (b * nt + t, 0))
    state = pl.BlockSpec((1, nch, CONV_W - 1, FF_CHUNK), lambda b, t: (b, 0, 0, 0))
    return pl.pallas_call(
        functools.partial(_ffn_kernel, tm=tm),
        grid=(batch, nt),
        in_specs=[tok(D_MODEL), _const_spec((1, D_MODEL)), _const_spec(lw["w_up"].shape),
                  _const_spec(lw["conv"].shape), state, _const_spec(lw["w_down"].shape),
                  _const_spec((1, D_MODEL)), _const_spec(lw["w_ple_gate"].shape), tok(D_PLE),
                  _const_spec(lw["w_ple_proj"].shape)],
        out_specs=[tok(D_MODEL), state],
        out_shape=[jax.ShapeDtypeStruct(x2d.shape, F32), jax.ShapeDtypeStruct(conv_past.shape, F32)],
        scratch_shapes=[pltpu.VMEM((nch, SUBLANES + tm, FF_CHUNK), F32), pltpu.VMEM((tm, D_FF), BF16)],
        compiler_params=_params("arbitrary", "arbitrary"), name="ffn",
    )(x2d, lw["g_pre_ffn"], lw["w_up"], lw["conv"], conv_past, lw["w_down"], lw["g_post_ffn"],
      lw["w_ple_gate"], p2d, lw["w_ple_proj"])


def _prep_layer(w_in, ckv_gain, w_uk, w_uv, w_proj_sb, w_proj_mla, w_out, g_pre_mix, g_post_mix, g_pre_ffn,
                g_post_ffn, w_up, conv_w, conv_b, w_down, w_ple_gate, w_ple_proj):
    o_qm = 3 * SB_W
    o_ckv = o_qm + N_HEADS * (MLA_NOPE + MLA_ROPE)
    o_kr = o_ckv + KV_LORA
    o_g = o_kr + MLA_ROPE
    wqm = w_in[:, o_qm:o_ckv].reshape(D_MODEL, N_HEADS, MLA_NOPE + MLA_ROPE)
    w_qn = wqm[:, :, :MLA_NOPE].reshape(D_MODEL, N_HEADS * MLA_NOPE)
    w_qr = jnp.concatenate([wqm[:, :, MLA_NOPE:MLA_NOPE + ROPE_HALF].reshape(D_MODEL, LANES),
                            wqm[:, :, MLA_NOPE + ROPE_HALF:].reshape(D_MODEL, LANES)], axis=1)
    w_kr = w_in[:, o_kr:o_g]
    w_kx = jnp.zeros((D_MODEL, 2 * LANES), F32)
    w_kx = w_kx.at[:, :ROPE_HALF].set(w_kr[:, :ROPE_HALF]).at[:, LANES:LANES + ROPE_HALF].set(w_kr[:, ROPE_HALF:])
    wa = jnp.zeros((N_HEADS, LANES, QM_W), F32)
    wb = np.zeros((N_HEADS, 2 * LANES, QM_W), np.float32)
    for h in range(N_HEADS):
        r0 = (h % 2) * MLA_NOPE
        wa = wa.at[h, r0:r0 + MLA_NOPE, :KV_LORA].set(w_uk[:, h, :].T)
        for i in range(ROPE_HALF):
            wb[h, h * ROPE_HALF + i, KV_LORA + i] = 1.0
            wb[h, LANES + h * ROPE_HALF + i, KV_LORA + ROPE_HALF + i] = 1.0
    w_uv_bd = jnp.zeros((N_HEADS * KV_LORA, N_HEADS * MLA_V), F32)
    for h in range(N_HEADS):
        w_uv_bd = w_uv_bd.at[h * KV_LORA:(h + 1) * KV_LORA, h * MLA_V:(h + 1) * MLA_V].set(w_uv[:, h, :])
    nch = 2 * N_FF_CHUNKS
    conv = jnp.concatenate([conv_w, conv_b[None, :], jnp.zeros((SUBLANES - CONV_W - 1, 2 * D_FF), F32)], axis=0)
    row = lambda g: g.reshape(1, -1)
    return dict(
        w_qkv=w_in[:, :o_qm].astype(BF16), w_qn=w_qn.astype(BF16), w_qr=w_qr.astype(BF16),
        w_ckv=w_in[:, o_ckv:o_kr].astype(BF16), w_kx=w_kx.astype(BF16), ckv_gain=row(ckv_gain),
        wa=wa.astype(BF16), wb=jnp.asarray(wb, BF16), eye=jnp.eye(LANES, dtype=BF16),
        w_gates=w_in[:, o_g:].astype(BF16), w_proj_sb=w_proj_sb.astype(BF16), w_uv_bd=w_uv_bd.astype(BF16),
        w_proj_mla=w_proj_mla.astype(BF16), w_out=w_out.astype(BF16),
        g_pre_mix=row(g_pre_mix), g_post_mix=row(g_post_mix), g_pre_ffn=row(g_pre_ffn), g_post_ffn=row(g_post_ffn),
        w_up=w_up.reshape(D_MODEL, nch, FF_CHUNK).transpose(1, 0, 2).astype(BF16),
        conv=conv.reshape(SUBLANES, nch, FF_CHUNK).transpose(1, 0, 2),
        w_down=w_down.astype(BF16),
        w_ple_gate=w_ple_gate.astype(BF16), w_ple_proj=w_ple_proj.astype(BF16))


def _rope_tables(pos, rows):
    freqs = jnp.power(ROPE_THETA, -jnp.arange(ROPE_HALF, dtype=F32) / ROPE_HALF)
    ang = pos.astype(F32)[:, None] * freqs[None, :]
    reps = max(1, rows // pos.shape[0])
    tile = lambda a: jnp.tile(a, (reps, LANES // ROPE_HALF))
    return tile(jnp.cos(ang)), tile(jnp.sin(ang))


def _conv_to_chunks(state):
    b = state.shape[0]
    return state.reshape(b, CONV_W - 1, 2 * N_FF_CHUNKS, FF_CHUNK).transpose(0, 2, 1, 3)


def _conv_from_chunks(state):
    b = state.shape[0]
    return state.transpose(0, 2, 1, 3).reshape(b, CONV_W - 1, 2 * D_FF)


def kernel(x_prompt, x_sample, p_prompt, p_sample, cache_sb_k, cache_sb_v, cache_mla_ckv, cache_mla_krope,
           state_ffn_conv, w_in, ckv_gain, w_uk, w_uv, w_proj_sb, w_proj_mla, w_out, g_pre_mix, g_post_mix,
           g_pre_ffn, g_post_ffn, w_up, conv_w, conv_b, w_down, w_ple_gate, w_ple_proj):
    bp, tp, _ = x_prompt.shape
    bs, ts, _ = x_sample.shape
    depth = w_in.shape[0]
    past = cache_sb_k.shape[2]
    assert tp % BLK == 0 and (bs * ts) % BLK == 0 and BLK % ts == 0
    tabs_p = _rope_tables(jnp.arange(tp, dtype=jnp.int32), BLK)
    tabs_s = _rope_tables(past + jnp.arange(ts, dtype=jnp.int32), BLK)
    xp = x_prompt.reshape(bp * tp, D_MODEL)
    xs = x_sample.reshape(bs * ts, D_MODEL)
    zero_conv = jnp.zeros((bp, 2 * N_FF_CHUNKS, CONV_W - 1, FF_CHUNK), F32)
    st_p, st_s = [], []
    for i in range(depth):
        lw = _prep_layer(w_in[i], ckv_gain[i], w_uk[i], w_uv[i], w_proj_sb[i], w_proj_mla[i], w_out[i],
                         g_pre_mix[i], g_post_mix[i], g_pre_ffn[i], g_post_ffn[i], w_up[i], conv_w[i],
                         conv_b[i], w_down[i], w_ple_gate[i], w_ple_proj[i])
        q, k_st, k_bf, v_st, vt, qm, cm, ct, c_st, r_st = _proj(xp, tp, tabs_p, lw, transposed=True)
        o_sb = _sb_prompt(q, k_bf, vt, bp, tp)
        o_lat = _mla_prompt(qm, cm, ct, bp, tp)
        x1 = _mix(xp, o_sb, o_lat, lw)
        xp, conv_p = _ffn(x1, p_prompt[i].reshape(bp * tp, D_PLE), zero_conv, bp, tp, lw)
        st_p.append((k_st.reshape(bp, tp, N_HEADS, SB_HEAD_DIM), v_st.reshape(bp, tp, N_HEADS, SB_HEAD_DIM),
                     c_st.reshape(bp, tp, KV_LORA), r_st.reshape(bp, tp, MLA_ROPE), _conv_from_chunks(conv_p)))
        q, k_st, _, v_st, qm, cm, c_st, r_st = _proj(xs, ts, tabs_s, lw, transposed=False)
        o_sb = _sb_sample(q, k_st, v_st, cache_sb_k[i].reshape(bs, past, SB_W),
                          cache_sb_v[i].reshape(bs, past, SB_W), bs, ts)
        o_lat = _mla_sample(qm, cm, cache_mla_ckv[i], cache_mla_krope[i], bs, ts)
        x1 = _mix(xs, o_sb, o_lat, lw)
        xs, conv_s = _ffn(x1, p_sample[i].reshape(bs * ts, D_PLE), _conv_to_chunks(state_ffn_conv[i]), bs, ts, lw)
        st_s.append((k_st.reshape(bs, ts, N_HEADS, SB_HEAD_DIM), v_st.reshape(bs, ts, N_HEADS, SB_HEAD_DIM),
                     c_st.reshape(bs, ts, KV_LORA), r_st.reshape(bs, ts, MLA_ROPE), _conv_from_chunks(conv_s)))
    stack = lambda sts, k: jnp.stack([s[k] for s in sts])
    return ((xp.reshape(bp, tp, D_MODEL), xs.reshape(bs, ts, D_MODEL))
            + tuple(stack(st_p, k) for k in range(5)) + tuple(stack(st_s, k) for k in range(5)))
```

```python
import functools
import math

import jax
import jax.numpy as jnp
import numpy as np
from jax import lax
from jax.experimental import pallas as pl
from jax.experimental.pallas import tpu as pltpu

D_MODEL = 1024
N_HEADS = 8
SB_HEAD_DIM = 64
SB_W = N_HEADS * SB_HEAD_DIM
MLA_NOPE = 64
MLA_ROPE = 32
ROPE_HALF = MLA_ROPE // 2
KV_LORA = 128
MLA_V = 64
D_FF = 2816
CONV_W = 3
CHUNK = 64
D_PLE = 256
ROPE_THETA = 10000.0
EPS = 1e-6
SB_SCALE = SB_HEAD_DIM ** -0.5
MLA_SCALE = (MLA_NOPE + MLA_ROPE) ** -0.5
LOG2E = math.log2(math.e)

LANES = 128
SUBLANES = 8
MXU_DIM = 256
VMEM_LIMIT_BYTES = 56 * 1024 * 1024

BLK = MXU_DIM
FF_CHUNK = MXU_DIM
N_FF_CHUNKS = D_FF // FF_CHUNK
FF_DOWN_GROUP = 4
QM_W = 2 * LANES
DEC_KEYS = 1024
MLA_DEC_KEYS = 512
HEAD_GROUP = MXU_DIM // SB_HEAD_DIM
N_GROUPS = N_HEADS // HEAD_GROUP

BF16 = jnp.bfloat16
F32 = jnp.float32


def _dot(a, b):
    return jnp.dot(a, b, preferred_element_type=F32)


def _dot_nt(a, b):
    return lax.dot_general(a, b, (((1,), (1,)), ((), ())), preferred_element_type=F32)


def _rmsnorm(x, g):
    ms = jnp.mean(x * x, axis=-1, keepdims=True)
    return x * lax.rsqrt(ms + EPS) * g


def _sigmoid(x):
    return 1.0 / (1.0 + jnp.exp(-x))


def _softplus2(z2):
    return jnp.maximum(z2, 0.0) + jnp.log2(1.0 + jnp.exp2(-jnp.abs(z2)))


def _gelu_tanh(x):
    c = math.sqrt(2.0 / math.pi)
    return 0.5 * x * (1.0 + jnp.tanh(c * (x + 0.044715 * (x * x * x))))


def _split_bf16(x):
    hi = x.astype(BF16)
    lo = (x - hi.astype(F32)).astype(BF16)
    return hi, lo


def _params(*sem):
    return pltpu.CompilerParams(dimension_semantics=sem, vmem_limit_bytes=VMEM_LIMIT_BYTES)


def _const_spec(shape):
    return pl.BlockSpec(shape, lambda *_: (0,) * len(shape), pipeline_mode=pl.Buffered(1))


def _proj_kernel(x_ref, g_ref, cos_ref, sin_ref, wqkv_ref, wqn_ref, wqr_ref, wckv_ref, wkx_ref,
                 gain_ref, wa_ref, wb_ref, eye_ref, *out_refs, transposed):
    if transposed:
        (q_ref, kst_ref, kbf_ref, vst_ref, vt_ref, qm_ref, cm_ref, ct_ref, cst_ref, rst_ref) = out_refs
    else:
        (q_ref, kst_ref, kbf_ref, vst_ref, qm_ref, cm_ref, cst_ref, rst_ref) = out_refs
    hb = _rmsnorm(x_ref[...], g_ref[...]).astype(BF16)

    qkv = _dot(hb, wqkv_ref[...])
    q_ref[...] = (qkv[:, :SB_W] * SB_SCALE).astype(BF16)
    k = qkv[:, SB_W:2 * SB_W]
    v = qkv[:, 2 * SB_W:]
    kst_ref[...] = k
    vst_ref[...] = v
    kbf_ref[...] = k.astype(BF16)
    if transposed:
        vb = v.astype(BF16)
        for p in range(SB_W // LANES):
            vt_ref[0, 0, p * LANES:(p + 1) * LANES, :] = _dot_nt(
                eye_ref[...], vb[:, p * LANES:(p + 1) * LANES]).astype(BF16)

    cos = cos_ref[...]
    sin = sin_ref[...]
    qn = _dot(hb, wqn_ref[...]).astype(BF16)
    qr = _dot(hb, wqr_ref[...])
    x1, x2 = qr[:, :LANES], qr[:, LANES:]
    r12 = jnp.concatenate([x1 * cos - x2 * sin, x1 * sin + x2 * cos], axis=1).astype(BF16)
    for h in range(N_HEADS):
        p = h // 2
        qm_ref[h] = (_dot(qn[:, p * LANES:(p + 1) * LANES], wa_ref[h]) + _dot(r12, wb_ref[h])).astype(BF16)

    ckv = _rmsnorm(_dot(hb, wckv_ref[...]), gain_ref[...])
    cst_ref[...] = ckv
    kx = _dot(hb, wkx_ref[...])
    k1, k2 = kx[:, :LANES], kx[:, LANES:]
    kr1 = k1 * cos - k2 * sin
    kr2 = k1 * sin + k2 * cos
    lane = lax.broadcasted_iota(jnp.int32, kr1.shape, 1)
    kr = jnp.where(lane < ROPE_HALF, kr1, pltpu.roll(kr2, ROPE_HALF, axis=1))
    rst_ref[...] = kr[:, :MLA_ROPE]
    ckv_b = ckv.astype(BF16)
    cm_ref[...] = jnp.concatenate([ckv_b, kr.astype(BF16)], axis=1)
    if transposed:
        ct_ref[0, 0] = _dot_nt(eye_ref[...], ckv_b).astype(BF16)


def _proj(x2d, seq_len, tabs, lw, *, transposed):
    n = x2d.shape[0]
    tm = BLK
    nt = n // tm
    cos, sin = tabs
    tab_blocks = cos.shape[0] // tm
    tok = lambda w: pl.BlockSpec((tm, w), lambda i: (i, 0))
    tab = pl.BlockSpec((tm, LANES), lambda i: (i % tab_blocks, 0))
    in_specs = [tok(D_MODEL), _const_spec((1, D_MODEL)), tab, tab,
                _const_spec(lw["w_qkv"].shape), _const_spec(lw["w_qn"].shape), _const_spec(lw["w_qr"].shape),
                _const_spec(lw["w_ckv"].shape), _const_spec(lw["w_kx"].shape), _const_spec((1, KV_LORA)),
                _const_spec(lw["wa"].shape), _const_spec(lw["wb"].shape), _const_spec((LANES, LANES))]
    out_shape = [jax.ShapeDtypeStruct((n, SB_W), BF16), jax.ShapeDtypeStruct((n, SB_W), F32),
                 jax.ShapeDtypeStruct((n, SB_W), BF16), jax.ShapeDtypeStruct((n, SB_W), F32)]
    out_specs = [tok(SB_W), tok(SB_W), tok(SB_W), tok(SB_W)]
    if transposed:
        assert seq_len % tm == 0
        nb = seq_len // tm
        out_shape.append(jax.ShapeDtypeStruct((n // seq_len, nb, SB_W, tm), BF16))
        out_specs.append(pl.BlockSpec((1, 1, SB_W, tm), lambda i: (i // nb, i % nb, 0, 0)))
    out_shape += [jax.ShapeDtypeStruct((N_HEADS, n, QM_W), BF16), jax.ShapeDtypeStruct((n, QM_W), BF16)]
    out_specs += [pl.BlockSpec((N_HEADS, tm, QM_W), lambda i: (0, i, 0)), tok(QM_W)]
    if transposed:
        out_shape.append(jax.ShapeDtypeStruct((n // seq_len, nb, KV_LORA, tm), BF16))
        out_specs.append(pl.BlockSpec((1, 1, KV_LORA, tm), lambda i: (i // nb, i % nb, 0, 0)))
    out_shape += [jax.ShapeDtypeStruct((n, KV_LORA), F32), jax.ShapeDtypeStruct((n, MLA_ROPE), F32)]
    out_specs += [tok(KV_LORA), tok(MLA_ROPE)]
    return pl.pallas_call(
        functools.partial(_proj_kernel, transposed=transposed),
        grid=(nt,), in_specs=in_specs, out_specs=out_specs, out_shape=out_shape,
        compiler_params=_params("parallel"), name="proj_t" if transposed else "proj",
    )(x2d, lw["g_pre_mix"], cos, sin, lw["w_qkv"], lw["w_qn"], lw["w_qr"], lw["w_ckv"], lw["w_kx"],
      lw["ckv_gain"], lw["wa"], lw["wb"], lw["eye"])


def _sb_block(q_ref, k_ref, vt_ref, qrows, krows, j, acc_ref, car_ref, umat, vis):
    lane = lax.broadcasted_iota(jnp.int32, (1, LANES), 1)
    heads = range(N_HEADS)
    zs = []
    for h in heads:
        p, hh = h // 2, h % 2
        cols = slice(p * LANES, (p + 1) * LANES)
        k2 = k_ref[0, krows, cols]
        head_lanes = (lane >= hh * SB_HEAD_DIM) & (lane < (hh + 1) * SB_HEAD_DIM)
        kh = jnp.where(head_lanes, k2, jnp.zeros_like(k2))
        zs.append(_dot_nt(kh, q_ref[0, qrows, cols]) * LOG2E)
    mms = []
    for h in heads:
        sp = _softplus2(zs[h])
        if vis is not None:
            sp = jnp.where(vis, sp, 0.0)
        hi, lo = _split_bf16(sp)
        mms.append(_dot(umat, hi) + _dot(umat, lo))
    for h in heads:
        car = car_ref[h, 0:1, :]
        a = jnp.exp2(zs[h] + mms[h] + car)
        if vis is not None:
            a = jnp.where(vis, a, 0.0)
        rows = slice(h * SB_HEAD_DIM, (h + 1) * SB_HEAD_DIM)
        acc_ref[rows, :] += _dot(vt_ref[0, j, rows, :], a.astype(BF16))
        car_ref[h, 0:1, :] = car + mms[h][0:1, :]


def _sb_kernel(q_ref, k_ref, vt_ref, o_ref, acc_ref, car_ref, *, nblk):
    kidx = lax.broadcasted_iota(jnp.int32, (BLK, BLK), 0)
    qidx = lax.broadcasted_iota(jnp.int32, (BLK, BLK), 1)
    umat = jnp.where(qidx >= kidx, -1.0, 0.0).astype(BF16)
    vis = kidx < qidx

    def q_block(i, carry):
        acc_ref[...] = jnp.zeros_like(acc_ref)
        car_ref[...] = jnp.zeros_like(car_ref)
        qrows = pl.ds(pl.multiple_of(i * BLK, BLK), BLK)
        _sb_block(q_ref, k_ref, vt_ref, qrows, qrows, i, acc_ref, car_ref, umat, vis)

        def k_block(jj, c):
            j = i - 1 - jj
            krows = pl.ds(pl.multiple_of(j * BLK, BLK), BLK)
            _sb_block(q_ref, k_ref, vt_ref, qrows, krows, j, acc_ref, car_ref, umat, None)
            return c

        lax.fori_loop(0, i, k_block, 0)
        o_ref[0, qrows, :] = acc_ref[...].T.astype(BF16)
        return carry

    lax.fori_loop(0, nblk, q_block, 0)


def _sb_prompt(q, kbf, vt, batch, seq_len):
    nblk = seq_len // BLK
    q3 = q.reshape(batch, seq_len, SB_W)
    k3 = kbf.reshape(batch, seq_len, SB_W)
    row = pl.BlockSpec((1, seq_len, SB_W), lambda b: (b, 0, 0))
    out = pl.pallas_call(
        functools.partial(_sb_kernel, nblk=nblk),
        grid=(batch,),
        in_specs=[row, row, pl.BlockSpec((1, nblk, SB_W, BLK), lambda b: (b, 0, 0, 0))],
        out_specs=row,
        out_shape=jax.ShapeDtypeStruct((batch, seq_len, SB_W), BF16),
        scratch_shapes=[pltpu.VMEM((SB_W, BLK), F32), pltpu.VMEM((N_HEADS, SUBLANES, BLK), F32)],
        compiler_params=_params("parallel"), name="sb_prompt",
    )(q3, k3, vt)
    return out.reshape(batch * seq_len, SB_W)


def _mla_block(q_ref, c_ref, ct_ref, qrows, krows, j, acc_ref, st_ref, vis):
    c = c_ref[krows, :]
    ct = ct_ref[0, j]
    ss = [_dot_nt(c, q_ref[h, qrows, :]) for h in range(N_HEADS)]
    for h in range(N_HEADS):
        s = ss[h]
        if vis is not None:
            s = jnp.where(vis, s, -jnp.inf)
        m_old = st_ref[h, 0:1, :]
        m_new = jnp.maximum(m_old, jnp.max(s, axis=0, keepdims=True))
        p = jnp.exp2((s - m_new) * (MLA_SCALE * LOG2E))
        alpha = jnp.exp2((m_old - m_new) * (MLA_SCALE * LOG2E))
        st_ref[h, 1:2, :] = alpha * st_ref[h, 1:2, :] + jnp.sum(p, axis=0, keepdims=True)
        st_ref[h, 0:1, :] = m_new
        acc_ref[h] = alpha * acc_ref[h] + _dot(ct, p.astype(BF16))


def _mla_kernel(q_ref, c_ref, ct_ref, o_ref, acc_ref, st_ref, *, nblk):
    kidx = lax.broadcasted_iota(jnp.int32, (BLK, BLK), 0)
    qidx = lax.broadcasted_iota(jnp.int32, (BLK, BLK), 1)
    vis = (kidx // CHUNK) <= (qidx // CHUNK)

    def q_block(i, carry):
        acc_ref[...] = jnp.zeros_like(acc_ref)
        for h in range(N_HEADS):
            st_ref[h, 0:1, :] = jnp.full((1, BLK), -jnp.inf, F32)
            st_ref[h, 1:2, :] = jnp.zeros((1, BLK), F32)
        rows = pl.ds(pl.multiple_of(i * BLK, BLK), BLK)
        _mla_block(q_ref, c_ref, ct_ref, rows, rows, i, acc_ref, st_ref, vis)

        def k_block(j, c):
            krows = pl.ds(pl.multiple_of(j * BLK, BLK), BLK)
            _mla_block(q_ref, c_ref, ct_ref, rows, krows, j, acc_ref, st_ref, None)
            return c

        lax.fori_loop(0, i, k_block, 0)
        for h in range(N_HEADS):
            o = acc_ref[h] * (1.0 / st_ref[h, 1:2, :])
            o_ref[rows, h * KV_LORA:(h + 1) * KV_LORA] = o.T.astype(BF16)
        return carry

    lax.fori_loop(0, nblk, q_block, 0)


def _mla_prompt(qm, cm, ct, batch, seq_len):
    nblk = seq_len // BLK
    return pl.pallas_call(
        functools.partial(_mla_kernel, nblk=nblk),
        grid=(batch,),
        in_specs=[pl.BlockSpec((N_HEADS, seq_len, QM_W), lambda b: (0, b, 0)),
                  pl.BlockSpec((seq_len, QM_W), lambda b: (b, 0)),
                  pl.BlockSpec((1, nblk, KV_LORA, BLK), lambda b: (b, 0, 0, 0))],
        out_specs=pl.BlockSpec((seq_len, N_HEADS * KV_LORA), lambda b: (b, 0)),
        out_shape=jax.ShapeDtypeStruct((batch * seq_len, N_HEADS * KV_LORA), BF16),
        scratch_shapes=[pltpu.VMEM((N_HEADS, KV_LORA, BLK), F32), pltpu.VMEM((N_HEADS, SUBLANES, BLK), F32)],
        compiler_params=_params("parallel"), name="mla_prompt",
    )(qm, cm, ct)


def _sbdec_blocks(qh_ref, k_of, v_of, nblocks, acc_ref, car_ref, wn, vis):
    heads = range(N_HEADS)
    t = qh_ref.shape[1]
    zs = [jnp.concatenate([_dot_nt(qh_ref[h], k_of(b, h).astype(BF16)) for h in heads], axis=0) * LOG2E
          for b in range(nblocks)]
    mms = []
    for b in range(nblocks):
        sp = _softplus2(zs[b])
        if vis is not None:
            sp = jnp.where(vis, sp, 0.0)
        hi, lo = _split_bf16(sp)
        mm2 = _dot(jnp.concatenate([hi, lo], axis=0), wn)
        mms.append(mm2[:N_HEADS * t] + mm2[N_HEADS * t:])
    car = car_ref[:, 0:1]
    for b in range(nblocks):
        a = jnp.exp2(zs[b] + mms[b] + car)
        if vis is not None:
            a = jnp.where(vis, a, 0.0)
        a = a.astype(BF16)
        for h in heads:
            acc_ref[h] += _dot(a[h * t:(h + 1) * t, :], v_of(b, h).astype(BF16))
        car = car + mms[b][:, 0:1]
    car_ref[...] = jnp.broadcast_to(car, car_ref.shape)


def _sbdec_kernel(q_ref, kn_ref, vn_ref, ck_ref, cv_ref, o_ref, qh_ref, acc_ref, car_ref, *, t_new):
    j = pl.program_id(1)
    hcols = lambda h: slice(h * SB_HEAD_DIM, (h + 1) * SB_HEAD_DIM)

    @pl.when(j == 0)
    def _():
        acc_ref[...] = jnp.zeros_like(acc_ref)
        car_ref[...] = jnp.zeros_like(car_ref)
        for h in range(N_HEADS):
            qh_ref[h] = q_ref[:, hcols(h)]
        kp = lax.broadcasted_iota(jnp.int32, (t_new, t_new), 0)
        kc = lax.broadcasted_iota(jnp.int32, (t_new, t_new), 1)
        wn_new = jnp.where(kp >= kc, -1.0, 0.0).astype(BF16)
        qi = lax.rem(lax.broadcasted_iota(jnp.int32, (N_HEADS * t_new, t_new), 0), t_new)
        ki = lax.broadcasted_iota(jnp.int32, (N_HEADS * t_new, t_new), 1)
        _sbdec_blocks(qh_ref, lambda b, h: kn_ref[:, hcols(h)], lambda b, h: vn_ref[:, hcols(h)], 1,
                      acc_ref, car_ref, wn_new, ki < qi)

    kp = lax.broadcasted_iota(jnp.int32, (BLK, BLK), 0)
    kc = lax.broadcasted_iota(jnp.int32, (BLK, BLK), 1)
    wn = jnp.where(kp >= kc, -1.0, 0.0).astype(BF16)
    nsub = DEC_KEYS // BLK
    rows = lambda b, h: pl.ds((nsub - 1 - b) * BLK * N_HEADS + h, BLK, stride=N_HEADS)
    _sbdec_blocks(qh_ref, lambda b, h: ck_ref[0, 0, rows(b, h), :], lambda b, h: cv_ref[0, 0, rows(b, h), :],
                  nsub, acc_ref, car_ref, wn, None)

    @pl.when(j == pl.num_programs(1) - 1)
    def _():
        for h in range(N_HEADS):
            o_ref[:, hcols(h)] = acc_ref[h].astype(BF16)


def _sb_sample(q, k_new, v_new, cache_k, cache_v, layer, batch, t_new):
    past = cache_k.shape[2] // N_HEADS
    assert past % DEC_KEYS == 0
    nkb = past // DEC_KEYS
    tok = lambda: pl.BlockSpec((t_new, SB_W), lambda b, j: (b, 0))
    cache = lambda: pl.BlockSpec((1, 1, DEC_KEYS * N_HEADS, SB_HEAD_DIM), lambda b, j: (layer, b, nkb - 1 - j, 0))
    return pl.pallas_call(
        functools.partial(_sbdec_kernel, t_new=t_new),
        grid=(batch, nkb),
        in_specs=[tok(), tok(), tok(), cache(), cache()],
        out_specs=tok(),
        out_shape=jax.ShapeDtypeStruct((batch * t_new, SB_W), BF16),
        scratch_shapes=[pltpu.VMEM((N_HEADS, t_new, SB_HEAD_DIM), BF16),
                        pltpu.VMEM((N_HEADS, t_new, SB_HEAD_DIM), F32),
                        pltpu.VMEM((N_HEADS * t_new, LANES), F32)],
        compiler_params=_params("parallel", "arbitrary"), name="sb_sample",
    )(q, k_new, v_new, cache_k, cache_v)


def _mladec_kernel(q_ref, cn_ref, ck_ref, kr_ref, o_ref, *, t_new, past, new_visible):
    q = q_ref[...].reshape(N_HEADS * t_new, QM_W)
    q_lat = q[:, :KV_LORA]
    q_rope = q[:, KV_LORA:KV_LORA + MLA_ROPE]
    c2 = MLA_SCALE * LOG2E

    cn = cn_ref[...]
    s = _dot_nt(q, cn)
    if not new_visible:
        qi = past + lax.broadcasted_iota(jnp.int32, s.shape, 0) % t_new
        ki = past + lax.broadcasted_iota(jnp.int32, s.shape, 1)
        s = jnp.where((ki // CHUNK) <= (qi // CHUNK), s, -jnp.inf)
    m = jnp.max(s, axis=1, keepdims=True)
    p = jnp.exp2((s - m) * c2)
    l = jnp.sum(p, axis=1, keepdims=True)
    acc = _dot(p.astype(BF16), cn[:, :KV_LORA])

    def chunk(c, carry):
        m, l, acc = carry
        rows = pl.ds(pl.multiple_of(c * MLA_DEC_KEYS, MLA_DEC_KEYS), MLA_DEC_KEYS)
        ck = ck_ref[0, 0, rows, :].astype(BF16)
        kr = kr_ref[0, 0, rows, :].astype(BF16)
        s = _dot_nt(q_lat, ck) + _dot_nt(q_rope, kr)
        m_new = jnp.maximum(m, jnp.max(s, axis=1, keepdims=True))
        p = jnp.exp2((s - m_new) * c2)
        alpha = jnp.exp2((m - m_new) * c2)
        return (m_new, alpha * l + jnp.sum(p, axis=1, keepdims=True), alpha * acc + _dot(p.astype(BF16), ck))

    m, l, acc = lax.fori_loop(0, past // MLA_DEC_KEYS, chunk, (m, l, acc))
    o = (acc * (1.0 / l)).astype(BF16)
    for h in range(N_HEADS):
        o_ref[:, h * KV_LORA:(h + 1) * KV_LORA] = o[h * t_new:(h + 1) * t_new, :]


def _mla_sample(qm, cm, cache_ckv, cache_kr, layer, batch, t_new):
    past = cache_ckv.shape[2]
    assert past % MLA_DEC_KEYS == 0 and past % CHUNK == 0
    q_pos = past + np.arange(t_new)
    new_visible = bool(((q_pos[None, :] // CHUNK) <= (q_pos[:, None] // CHUNK)).all())
    return pl.pallas_call(
        functools.partial(_mladec_kernel, t_new=t_new, past=past, new_visible=new_visible),
        grid=(batch,),
        in_specs=[pl.BlockSpec((N_HEADS, t_new, QM_W), lambda b: (0, b, 0)),
                  pl.BlockSpec((t_new, QM_W), lambda b: (b, 0)),
                  pl.BlockSpec((1, 1, past, KV_LORA), lambda b: (layer, b, 0, 0)),
                  pl.BlockSpec((1, 1, past, MLA_ROPE), lambda b: (layer, b, 0, 0))],
        out_specs=pl.BlockSpec((t_new, N_HEADS * KV_LORA), lambda b: (b, 0)),
        out_shape=jax.ShapeDtypeStruct((batch * t_new, N_HEADS * KV_LORA), BF16),
        compiler_params=_params("parallel"), name="mla_sample",
    )(qm, cm, cache_ckv, cache_kr)


def _mix_kernel(x_ref, gpre_ref, osb_ref, olat_ref, wg_ref, wpsb_ref, wuv_ref, wpmla_ref, wout_ref,
                gpost_ref, o_ref):
    x = x_ref[...]
    hb = _rmsnorm(x, gpre_ref[...]).astype(BF16)
    gates = _dot(hb, wg_ref[...])
    y_a = _dot(osb_ref[...], wpsb_ref[...])
    o_mla = _dot(olat_ref[...], wuv_ref[...]).astype(BF16)
    y_b = _dot(o_mla, wpmla_ref[...])
    mixin = _sigmoid(gates[:, :D_MODEL]) * y_a + _sigmoid(gates[:, D_MODEL:]) * y_b
    mix = _dot(mixin.astype(BF16), wout_ref[...])
    o_ref[...] = x + _rmsnorm(mix, gpost_ref[...])


def _mix(x2d, o_sb, o_lat, lw):
    n = x2d.shape[0]
    tm = BLK
    tok = lambda w: pl.BlockSpec((tm, w), lambda i: (i, 0))
    return pl.pallas_call(
        _mix_kernel, grid=(n // tm,),
        in_specs=[tok(D_MODEL), _const_spec((1, D_MODEL)), tok(SB_W), tok(N_HEADS * KV_LORA),
                  _const_spec(lw["w_gates"].shape), _const_spec(lw["w_proj_sb"].shape),
                  _const_spec(lw["w_uv_bd"].shape), _const_spec(lw["w_proj_mla"].shape),
                  _const_spec(lw["w_out"].shape), _const_spec((1, D_MODEL))],
        out_specs=tok(D_MODEL), out_shape=jax.ShapeDtypeStruct((n, D_MODEL), F32),
        compiler_params=_params("parallel"), name="mix",
    )(x2d, lw["g_pre_mix"], o_sb, o_lat, lw["w_gates"], lw["w_proj_sb"], lw["w_uv_bd"], lw["w_proj_mla"],
      lw["w_out"], lw["g_post_mix"])


def _ffn_kernel(x_ref, gpre_ref, wup_ref, cw_ref, past_ref, wdown_ref, gpost_ref, wpg_ref, p_ref, wpp_ref,
                o_ref, conv_ref, u_scr, act_scr, *, tm):
    t = pl.program_id(1)

    @pl.when(t == 0)
    def _():
        u_scr[:, SUBLANES - (CONV_W - 1):SUBLANES, :] = past_ref[0]

    x = x_ref[...]
    hb = _rmsnorm(x, gpre_ref[...]).astype(BF16)

    def up(c):
        for cc in (c, N_FF_CHUNKS + c):
            u_scr[cc, SUBLANES:, :] = _dot(hb, wup_ref[cc])

    def conv(c):
        cw = cw_ref[c]
        uc = cw[3:4, :]
        for i in range(CONV_W):
            uc = uc + u_scr[c, SUBLANES - (CONV_W - 1) + i:SUBLANES - (CONV_W - 1) + i + tm, :] * cw[i:i + 1, :]
        conv_ref[0, c] = u_scr[c, SUBLANES + tm - (CONV_W - 1):SUBLANES + tm, :]
        u_scr[c, 0:SUBLANES, :] = u_scr[c, tm:tm + SUBLANES, :]
        return uc

    up(0)
    f = None
    for c in range(N_FF_CHUNKS):
        if c + 1 < N_FF_CHUNKS:
            up(c + 1)
        act = _gelu_tanh(conv(c)) * conv(N_FF_CHUNKS + c)
        act_scr[:, c * FF_CHUNK:(c + 1) * FF_CHUNK] = act.astype(BF16)
        if (c + 1) % FF_DOWN_GROUP == 0 or c + 1 == N_FF_CHUNKS:
            k0 = (c // FF_DOWN_GROUP) * FF_DOWN_GROUP * FF_CHUNK
            k1 = (c + 1) * FF_CHUNK
            part = _dot(act_scr[:, k0:k1], wdown_ref[k0:k1, :])
            f = part if f is None else f + part
    x2 = x + _rmsnorm(f, gpost_ref[...])
    ple = _sigmoid(_dot(x2.astype(BF16), wpg_ref[...])) * _dot(p_ref[0].astype(BF16), wpp_ref[...])
    o_ref[...] = x2 + ple


def _ffn(x2d, p3d, layer, conv_past, batch, seq_len, lw):
    tm = min(BLK, seq_len)
    nt = seq_len // tm
    nch = 2 * N_FF_CHUNKS
    tok = lambda w: pl.BlockSpec((tm, w), lambda b, t: (b * nt + t, 0))
    ple = pl.BlockSpec((1, tm, D_PLE), lambda b, t: (layer, b * nt + t, 0))
    state = pl.BlockSpec((1, nch, CONV_W - 1, FF_CHUNK), lambda b, t: (b, 0, 0, 0))
    return pl.pallas_call(
        functools.partial(_ffn_kernel, tm=tm),
        grid=(batch, nt),
        in_specs=[tok(D_MODEL), _const_spec((1, D_MODEL)), _const_spec(lw["w_up"].shape),
                  _const_spec(lw["conv"].shape), state, _const_spec(lw["w_down"].shape),
                  _const_spec((1, D_MODEL)), _const_spec(lw["w_ple_gate"].shape), ple,
                  _const_spec(lw["w_ple_proj"].shape)],
        out_specs=[tok(D_MODEL), state],
        out_shape=[jax.ShapeDtypeStruct(x2d.shape, F32), jax.ShapeDtypeStruct(conv_past.shape, F32)],
        scratch_shapes=[pltpu.VMEM((nch, SUBLANES + tm, FF_CHUNK), F32), pltpu.VMEM((tm, D_FF), BF16)],
        compiler_params=_params("arbitrary", "arbitrary"), name="ffn",
    )(x2d, lw["g_pre_ffn"], lw["w_up"], lw["conv"], conv_past, lw["w_down"], lw["g_post_ffn"],
      lw["w_ple_gate"], p3d, lw["w_ple_proj"])


def _prep_layer(w_in, ckv_gain, w_uk, w_uv, w_proj_sb, w_proj_mla, w_out, g_pre_mix, g_post_mix, g_pre_ffn,
                g_post_ffn, w_up, conv_w, conv_b, w_down, w_ple_gate, w_ple_proj):
    o_qm = 3 * SB_W
    o_ckv = o_qm + N_HEADS * (MLA_NOPE + MLA_ROPE)
    o_kr = o_ckv + KV_LORA
    o_g = o_kr + MLA_ROPE
    wqm = w_in[:, o_qm:o_ckv].reshape(D_MODEL, N_HEADS, MLA_NOPE + MLA_ROPE)
    w_qn = wqm[:, :, :MLA_NOPE].reshape(D_MODEL, N_HEADS * MLA_NOPE)
    w_qr = jnp.concatenate([wqm[:, :, MLA_NOPE:MLA_NOPE + ROPE_HALF].reshape(D_MODEL, LANES),
                            wqm[:, :, MLA_NOPE + ROPE_HALF:].reshape(D_MODEL, LANES)], axis=1)
    w_kr = w_in[:, o_kr:o_g]
    w_kx = jnp.zeros((D_MODEL, 2 * LANES), F32)
    w_kx = w_kx.at[:, :ROPE_HALF].set(w_kr[:, :ROPE_HALF]).at[:, LANES:LANES + ROPE_HALF].set(w_kr[:, ROPE_HALF:])
    wa = jnp.zeros((N_HEADS, LANES, QM_W), F32)
    wb = np.zeros((N_HEADS, 2 * LANES, QM_W), np.float32)
    for h in range(N_HEADS):
        r0 = (h % 2) * MLA_NOPE
        wa = wa.at[h, r0:r0 + MLA_NOPE, :KV_LORA].set(w_uk[:, h, :].T)
        for i in range(ROPE_HALF):
            wb[h, h * ROPE_HALF + i, KV_LORA + i] = 1.0
            wb[h, LANES + h * ROPE_HALF + i, KV_LORA + ROPE_HALF + i] = 1.0
    w_uv_bd = jnp.zeros((N_HEADS * KV_LORA, N_HEADS * MLA_V), F32)
    for h in range(N_HEADS):
        w_uv_bd = w_uv_bd.at[h * KV_LORA:(h + 1) * KV_LORA, h * MLA_V:(h + 1) * MLA_V].set(w_uv[:, h, :])
    nch = 2 * N_FF_CHUNKS
    conv = jnp.concatenate([conv_w, conv_b[None, :], jnp.zeros((SUBLANES - CONV_W - 1, 2 * D_FF), F32)], axis=0)
    row = lambda g: g.reshape(1, -1)
    return dict(
        w_qkv=w_in[:, :o_qm].astype(BF16), w_qn=w_qn.astype(BF16), w_qr=w_qr.astype(BF16),
        w_ckv=w_in[:, o_ckv:o_kr].astype(BF16), w_kx=w_kx.astype(BF16), ckv_gain=row(ckv_gain),
        wa=wa.astype(BF16), wb=jnp.asarray(wb, BF16), eye=jnp.eye(LANES, dtype=BF16),
        w_gates=w_in[:, o_g:].astype(BF16), w_proj_sb=w_proj_sb.astype(BF16), w_uv_bd=w_uv_bd.astype(BF16),
        w_proj_mla=w_proj_mla.astype(BF16), w_out=w_out.astype(BF16),
        g_pre_mix=row(g_pre_mix), g_post_mix=row(g_post_mix), g_pre_ffn=row(g_pre_ffn), g_post_ffn=row(g_post_ffn),
        w_up=w_up.reshape(D_MODEL, nch, FF_CHUNK).transpose(1, 0, 2).astype(BF16),
        conv=conv.reshape(SUBLANES, nch, FF_CHUNK).transpose(1, 0, 2),
        w_down=w_down.astype(BF16),
        w_ple_gate=w_ple_gate.astype(BF16), w_ple_proj=w_ple_proj.astype(BF16))


def _rope_tables(pos, rows):
    freqs = jnp.power(ROPE_THETA, -jnp.arange(ROPE_HALF, dtype=F32) / ROPE_HALF)
    ang = pos.astype(F32)[:, None] * freqs[None, :]
    reps = max(1, rows // pos.shape[0])
    tile = lambda a: jnp.tile(a, (reps, LANES // ROPE_HALF))
    return tile(jnp.cos(ang)), tile(jnp.sin(ang))


def _conv_to_chunks(state):
    b = state.shape[0]
    return state.reshape(b, CONV_W - 1, 2 * N_FF_CHUNKS, FF_CHUNK).transpose(0, 2, 1, 3)


def _conv_from_chunks(state):
    b = state.shape[0]
    return state.transpose(0, 2, 1, 3).reshape(b, CONV_W - 1, 2 * D_FF)


def kernel(x_prompt, x_sample, p_prompt, p_sample, cache_sb_k, cache_sb_v, cache_mla_ckv, cache_mla_krope,
           state_ffn_conv, w_in, ckv_gain, w_uk, w_uv, w_proj_sb, w_proj_mla, w_out, g_pre_mix, g_post_mix,
           g_pre_ffn, g_post_ffn, w_up, conv_w, conv_b, w_down, w_ple_gate, w_ple_proj):
    bp, tp, _ = x_prompt.shape
    bs, ts, _ = x_sample.shape
    depth = w_in.shape[0]
    past = cache_sb_k.shape[2]
    assert tp % BLK == 0 and (bs * ts) % BLK == 0 and BLK % ts == 0
    tabs_p = _rope_tables(jnp.arange(tp, dtype=jnp.int32), BLK)
    tabs_s = _rope_tables(past + jnp.arange(ts, dtype=jnp.int32), BLK)
    xp = x_prompt.reshape(bp * tp, D_MODEL)
    xs = x_sample.reshape(bs * ts, D_MODEL)
    pp = p_prompt.reshape(depth, bp * tp, D_PLE)
    ps = p_sample.reshape(depth, bs * ts, D_PLE)
    ck_rows = cache_sb_k.reshape(depth, bs, past * N_HEADS, SB_HEAD_DIM)
    cv_rows = cache_sb_v.reshape(depth, bs, past * N_HEADS, SB_HEAD_DIM)
    zero_conv = jnp.zeros((bp, 2 * N_FF_CHUNKS, CONV_W - 1, FF_CHUNK), F32)
    st_p, st_s = [], []
    for i in range(depth):
        lw = _prep_layer(w_in[i], ckv_gain[i], w_uk[i], w_uv[i], w_proj_sb[i], w_proj_mla[i], w_out[i],
                         g_pre_mix[i], g_post_mix[i], g_pre_ffn[i], g_post_ffn[i], w_up[i], conv_w[i],
                         conv_b[i], w_down[i], w_ple_gate[i], w_ple_proj[i])
        q, k_st, k_bf, v_st, vt, qm, cm, ct, c_st, r_st = _proj(xp, tp, tabs_p, lw, transposed=True)
        o_sb = _sb_prompt(q, k_bf, vt, bp, tp)
        o_lat = _mla_prompt(qm, cm, ct, bp, tp)
        x1 = _mix(xp, o_sb, o_lat, lw)
        xp, conv_p = _ffn(x1, pp, i, zero_conv, bp, tp, lw)
        st_p.append((k_st.reshape(bp, tp, N_HEADS, SB_HEAD_DIM), v_st.reshape(bp, tp, N_HEADS, SB_HEAD_DIM),
                     c_st.reshape(bp, tp, KV_LORA), r_st.reshape(bp, tp, MLA_ROPE), _conv_from_chunks(conv_p)))
        q, k_st, _, v_st, qm, cm, c_st, r_st = _proj(xs, ts, tabs_s, lw, transposed=False)
        o_sb = _sb_sample(q, k_st, v_st, ck_rows, cv_rows, i, bs, ts)
        o_lat = _mla_sample(qm, cm, cache_mla_ckv, cache_mla_krope, i, bs, ts)
        x1 = _mix(xs, o_sb, o_lat, lw)
        xs, conv_s = _ffn(x1, ps, i, _conv_to_chunks(state_ffn_conv[i]), bs, ts, lw)
        st_s.append((k_st.reshape(bs, ts, N_HEADS, SB_HEAD_DIM), v_st.reshape(bs, ts, N_HEADS, SB_HEAD_DIM),
                     c_st.reshape(bs, ts, KV_LORA), r_st.reshape(bs, ts, MLA_ROPE), _conv_from_chunks(conv_s)))
    stack = lambda sts, k: jnp.stack([s[k] for s in sts])
    return ((xp.reshape(bp, tp, D_MODEL), xs.reshape(bs, ts, D_MODEL))
            + tuple(stack(st_p, k) for k in range(5)) + tuple(stack(st_s, k) for k in range(5)))
```

```python
import functools
import math

import jax
import jax.numpy as jnp
import numpy as np
from jax import lax
from jax.experimental import pallas as pl
from jax.experimental.pallas import tpu as pltpu

D_MODEL = 1024
N_HEADS = 8
SB_HEAD_DIM = 64
SB_W = N_HEADS * SB_HEAD_DIM
MLA_NOPE = 64
MLA_ROPE = 32
ROPE_HALF = MLA_ROPE // 2
KV_LORA = 128
MLA_V = 64
D_FF = 2816
CONV_W = 3
CHUNK = 64
D_PLE = 256
ROPE_THETA = 10000.0
EPS = 1e-6
SB_SCALE = SB_HEAD_DIM ** -0.5
MLA_SCALE = (MLA_NOPE + MLA_ROPE) ** -0.5
LOG2E = math.log2(math.e)

LANES = 128
SUBLANES = 8
MXU_DIM = 256
VMEM_LIMIT_BYTES = 56 * 1024 * 1024

BLK = MXU_DIM
FF_CHUNK = MXU_DIM
N_FF_CHUNKS = D_FF // FF_CHUNK
FF_DOWN_GROUP = 4
QM_W = 2 * LANES
DEC_KEYS = 1024
MLA_DEC_KEYS = 512
HEAD_GROUP = MXU_DIM // SB_HEAD_DIM
N_GROUPS = N_HEADS // HEAD_GROUP

BF16 = jnp.bfloat16
F32 = jnp.float32


def _dot(a, b):
    return jnp.dot(a, b, preferred_element_type=F32)


def _dot_nt(a, b):
    return lax.dot_general(a, b, (((1,), (1,)), ((), ())), preferred_element_type=F32)


def _rmsnorm(x, g):
    ms = jnp.mean(x * x, axis=-1, keepdims=True)
    return x * lax.rsqrt(ms + EPS) * g


def _sigmoid(x):
    return 1.0 / (1.0 + jnp.exp(-x))


def _softplus2(z2):
    return jnp.maximum(z2, 0.0) + jnp.log2(1.0 + jnp.exp2(-jnp.abs(z2)))


def _gelu_tanh(x):
    c = math.sqrt(2.0 / math.pi)
    return 0.5 * x * (1.0 + jnp.tanh(c * (x + 0.044715 * (x * x * x))))


def _split_bf16(x):
    hi = x.astype(BF16)
    lo = (x - hi.astype(F32)).astype(BF16)
    return hi, lo


def _params(*sem):
    return pltpu.CompilerParams(dimension_semantics=sem, vmem_limit_bytes=VMEM_LIMIT_BYTES)


def _const_spec(shape):
    return pl.BlockSpec(shape, lambda *_: (0,) * len(shape), pipeline_mode=pl.Buffered(1))


def _proj_kernel(x_ref, g_ref, cos_ref, sin_ref, wqkv_ref, wqn_ref, wqr_ref, wckv_ref, wkx_ref,
                 gain_ref, wa_ref, wb_ref, eye_ref, *out_refs, transposed):
    if transposed:
        (q_ref, kst_ref, kbf_ref, vst_ref, vt_ref, qm_ref, cm_ref, ct_ref, cst_ref, rst_ref) = out_refs
    else:
        (q_ref, kst_ref, kbf_ref, vst_ref, qm_ref, cm_ref, cst_ref, rst_ref) = out_refs
    hb = _rmsnorm(x_ref[...], g_ref[...]).astype(BF16)

    qkv = _dot(hb, wqkv_ref[...])
    q_ref[...] = (qkv[:, :SB_W] * SB_SCALE).astype(BF16)
    k = qkv[:, SB_W:2 * SB_W]
    v = qkv[:, 2 * SB_W:]
    kst_ref[...] = k
    vst_ref[...] = v
    kbf_ref[...] = k.astype(BF16)
    if transposed:
        vb = v.astype(BF16)
        for p in range(SB_W // LANES):
            vt_ref[0, 0, p * LANES:(p + 1) * LANES, :] = _dot_nt(
                eye_ref[...], vb[:, p * LANES:(p + 1) * LANES]).astype(BF16)

    cos = cos_ref[...]
    sin = sin_ref[...]
    qn = _dot(hb, wqn_ref[...]).astype(BF16)
    qr = _dot(hb, wqr_ref[...])
    x1, x2 = qr[:, :LANES], qr[:, LANES:]
    r12 = jnp.concatenate([x1 * cos - x2 * sin, x1 * sin + x2 * cos], axis=1).astype(BF16)
    for h in range(N_HEADS):
        p = h // 2
        qm_ref[h] = (_dot(qn[:, p * LANES:(p + 1) * LANES], wa_ref[h]) + _dot(r12, wb_ref[h])).astype(BF16)

    ckv = _rmsnorm(_dot(hb, wckv_ref[...]), gain_ref[...])
    cst_ref[...] = ckv
    kx = _dot(hb, wkx_ref[...])
    k1, k2 = kx[:, :LANES], kx[:, LANES:]
    kr1 = k1 * cos - k2 * sin
    kr2 = k1 * sin + k2 * cos
    lane = lax.broadcasted_iota(jnp.int32, kr1.shape, 1)
    kr = jnp.where(lane < ROPE_HALF, kr1, pltpu.roll(kr2, ROPE_HALF, axis=1))
    rst_ref[...] = kr[:, :MLA_ROPE]
    ckv_b = ckv.astype(BF16)
    cm_ref[...] = jnp.concatenate([ckv_b, kr.astype(BF16)], axis=1)
    if transposed:
        ct_ref[0, 0] = _dot_nt(eye_ref[...], ckv_b).astype(BF16)


def _proj(x2d, seq_len, tabs, lw, *, transposed):
    n = x2d.shape[0]
    tm = BLK
    nt = n // tm
    cos, sin = tabs
    tab_blocks = cos.shape[0] // tm
    tok = lambda w: pl.BlockSpec((tm, w), lambda i: (i, 0))
    tab = pl.BlockSpec((tm, LANES), lambda i: (i % tab_blocks, 0))
    in_specs = [tok(D_MODEL), _const_spec((1, D_MODEL)), tab, tab,
                _const_spec(lw["w_qkv"].shape), _const_spec(lw["w_qn"].shape), _const_spec(lw["w_qr"].shape),
                _const_spec(lw["w_ckv"].shape), _const_spec(lw["w_kx"].shape), _const_spec((1, KV_LORA)),
                _const_spec(lw["wa"].shape), _const_spec(lw["wb"].shape), _const_spec((LANES, LANES))]
    out_shape = [jax.ShapeDtypeStruct((n, SB_W), BF16), jax.ShapeDtypeStruct((n, SB_W), F32),
                 jax.ShapeDtypeStruct((n, SB_W), BF16), jax.ShapeDtypeStruct((n, SB_W), F32)]
    out_specs = [tok(SB_W), tok(SB_W), tok(SB_W), tok(SB_W)]
    if transposed:
        assert seq_len % tm == 0
        nb = seq_len // tm
        out_shape.append(jax.ShapeDtypeStruct((n // seq_len, nb, SB_W, tm), BF16))
        out_specs.append(pl.BlockSpec((1, 1, SB_W, tm), lambda i: (i // nb, i % nb, 0, 0)))
    out_shape += [jax.ShapeDtypeStruct((N_HEADS, n, QM_W), BF16), jax.ShapeDtypeStruct((n, QM_W), BF16)]
    out_specs += [pl.BlockSpec((N_HEADS, tm, QM_W), lambda i: (0, i, 0)), tok(QM_W)]
    if transposed:
        out_shape.append(jax.ShapeDtypeStruct((n // seq_len, nb, KV_LORA, tm), BF16))
        out_specs.append(pl.BlockSpec((1, 1, KV_LORA, tm), lambda i: (i // nb, i % nb, 0, 0)))
    out_shape += [jax.ShapeDtypeStruct((n, KV_LORA), F32), jax.ShapeDtypeStruct((n, MLA_ROPE), F32)]
    out_specs += [tok(KV_LORA), tok(MLA_ROPE)]
    return pl.pallas_call(
        functools.partial(_proj_kernel, transposed=transposed),
        grid=(nt,), in_specs=in_specs, out_specs=out_specs, out_shape=out_shape,
        compiler_params=_params("parallel"), name="proj_t" if transposed else "proj",
    )(x2d, lw["g_pre_mix"], cos, sin, lw["w_qkv"], lw["w_qn"], lw["w_qr"], lw["w_ckv"], lw["w_kx"],
      lw["ckv_gain"], lw["wa"], lw["wb"], lw["eye"])


def _sb_block(q_ref, k_ref, vt_ref, qrows, krows, j, acc_ref, car_ref, umat, vis):
    lane = lax.broadcasted_iota(jnp.int32, (1, LANES), 1)
    heads = range(N_HEADS)
    zs = []
    for h in heads:
        p, hh = h // 2, h % 2
        cols = slice(p * LANES, (p + 1) * LANES)
        k2 = k_ref[0, krows, cols]
        head_lanes = (lane >= hh * SB_HEAD_DIM) & (lane < (hh + 1) * SB_HEAD_DIM)
        kh = jnp.where(head_lanes, k2, jnp.zeros_like(k2))
        zs.append(_dot_nt(kh, q_ref[0, qrows, cols]) * LOG2E)
    mms = []
    for h in heads:
        sp = _softplus2(zs[h])
        if vis is not None:
            sp = jnp.where(vis, sp, 0.0)
        hi, lo = _split_bf16(sp)
        mms.append(_dot(umat, hi) + _dot(umat, lo))
    for h in heads:
        car = car_ref[h, 0:1, :]
        a = jnp.exp2(zs[h] + mms[h] + car)
        if vis is not None:
            a = jnp.where(vis, a, 0.0)
        rows = slice(h * SB_HEAD_DIM, (h + 1) * SB_HEAD_DIM)
        acc_ref[rows, :] += _dot(vt_ref[0, j, rows, :], a.astype(BF16))
        car_ref[h, 0:1, :] = car + mms[h][0:1, :]


def _sb_kernel(q_ref, k_ref, vt_ref, o_ref, acc_ref, car_ref, *, nblk):
    kidx = lax.broadcasted_iota(jnp.int32, (BLK, BLK), 0)
    qidx = lax.broadcasted_iota(jnp.int32, (BLK, BLK), 1)
    umat = jnp.where(qidx >= kidx, -1.0, 0.0).astype(BF16)
    vis = kidx < qidx

    def q_block(i, carry):
        acc_ref[...] = jnp.zeros_like(acc_ref)
        car_ref[...] = jnp.zeros_like(car_ref)
        qrows = pl.ds(pl.multiple_of(i * BLK, BLK), BLK)
        _sb_block(q_ref, k_ref, vt_ref, qrows, qrows, i, acc_ref, car_ref, umat, vis)

        def k_block(jj, c):
            j = i - 1 - jj
            krows = pl.ds(pl.multiple_of(j * BLK, BLK), BLK)
            _sb_block(q_ref, k_ref, vt_ref, qrows, krows, j, acc_ref, car_ref, umat, None)
            return c

        lax.fori_loop(0, i, k_block, 0)
        o_ref[0, qrows, :] = acc_ref[...].T.astype(BF16)
        return carry

    lax.fori_loop(0, nblk, q_block, 0)


def _sb_prompt(q, kbf, vt, batch, seq_len):
    nblk = seq_len // BLK
    q3 = q.reshape(batch, seq_len, SB_W)
    k3 = kbf.reshape(batch, seq_len, SB_W)
    row = pl.BlockSpec((1, seq_len, SB_W), lambda b: (b, 0, 0))
    out = pl.pallas_call(
        functools.partial(_sb_kernel, nblk=nblk),
        grid=(batch,),
        in_specs=[row, row, pl.BlockSpec((1, nblk, SB_W, BLK), lambda b: (b, 0, 0, 0))],
        out_specs=row,
        out_shape=jax.ShapeDtypeStruct((batch, seq_len, SB_W), BF16),
        scratch_shapes=[pltpu.VMEM((SB_W, BLK), F32), pltpu.VMEM((N_HEADS, SUBLANES, BLK), F32)],
        compiler_params=_params("parallel"), name="sb_prompt",
    )(q3, k3, vt)
    return out.reshape(batch * seq_len, SB_W)


def _mla_block(q_ref, c_ref, ct_ref, qrows, krows, j, acc_ref, st_ref, vis):
    c = c_ref[krows, :]
    ct = ct_ref[0, j]
    ss = [_dot_nt(c, q_ref[h, qrows, :]) for h in range(N_HEADS)]
    for h in range(N_HEADS):
        s = ss[h]
        if vis is not None:
            s = jnp.where(vis, s, -jnp.inf)
        m_old = st_ref[h, 0:1, :]
        m_new = jnp.maximum(m_old, jnp.max(s, axis=0, keepdims=True))
        p = jnp.exp2((s - m_new) * (MLA_SCALE * LOG2E))
        alpha = jnp.exp2((m_old - m_new) * (MLA_SCALE * LOG2E))
        st_ref[h, 1:2, :] = alpha * st_ref[h, 1:2, :] + jnp.sum(p, axis=0, keepdims=True)
        st_ref[h, 0:1, :] = m_new
        acc_ref[h] = alpha * acc_ref[h] + _dot(ct, p.astype(BF16))


def _mla_kernel(q_ref, c_ref, ct_ref, o_ref, acc_ref, st_ref, *, nblk):
    kidx = lax.broadcasted_iota(jnp.int32, (BLK, BLK), 0)
    qidx = lax.broadcasted_iota(jnp.int32, (BLK, BLK), 1)
    vis = (kidx // CHUNK) <= (qidx // CHUNK)

    def q_block(i, carry):
        acc_ref[...] = jnp.zeros_like(acc_ref)
        for h in range(N_HEADS):
            st_ref[h, 0:1, :] = jnp.full((1, BLK), -jnp.inf, F32)
            st_ref[h, 1:2, :] = jnp.zeros((1, BLK), F32)
        rows = pl.ds(pl.multiple_of(i * BLK, BLK), BLK)
        _mla_block(q_ref, c_ref, ct_ref, rows, rows, i, acc_ref, st_ref, vis)

        def k_block(j, c):
            krows = pl.ds(pl.multiple_of(j * BLK, BLK), BLK)
            _mla_block(q_ref, c_ref, ct_ref, rows, krows, j, acc_ref, st_ref, None)
            return c

        lax.fori_loop(0, i, k_block, 0)
        for h in range(N_HEADS):
            o = acc_ref[h] * (1.0 / st_ref[h, 1:2, :])
            o_ref[rows, h * KV_LORA:(h + 1) * KV_LORA] = o.T.astype(BF16)
        return carry

    lax.fori_loop(0, nblk, q_block, 0)


def _mla_prompt(qm, cm, ct, batch, seq_len):
    nblk = seq_len // BLK
    return pl.pallas_call(
        functools.partial(_mla_kernel, nblk=nblk),
        grid=(batch,),
        in_specs=[pl.BlockSpec((N_HEADS, seq_len, QM_W), lambda b: (0, b, 0)),
                  pl.BlockSpec((seq_len, QM_W), lambda b: (b, 0)),
                  pl.BlockSpec((1, nblk, KV_LORA, BLK), lambda b: (b, 0, 0, 0))],
        out_specs=pl.BlockSpec((seq_len, N_HEADS * KV_LORA), lambda b: (b, 0)),
        out_shape=jax.ShapeDtypeStruct((batch * seq_len, N_HEADS * KV_LORA), BF16),
        scratch_shapes=[pltpu.VMEM((N_HEADS, KV_LORA, BLK), F32), pltpu.VMEM((N_HEADS, SUBLANES, BLK), F32)],
        compiler_params=_params("parallel"), name="mla_prompt",
    )(qm, cm, ct)


def _sbdec_blocks(score, av, nblocks, t, acc_ref, car_ref, wn, vis):
    heads = range(N_HEADS)
    zs = [jnp.concatenate([score(b, h) for h in heads], axis=0) * LOG2E
          for b in range(nblocks)]
    mms = []
    for b in range(nblocks):
        sp = _softplus2(zs[b])
        if vis is not None:
            sp = jnp.where(vis, sp, 0.0)
        hi, lo = _split_bf16(sp)
        mm2 = _dot(jnp.concatenate([hi, lo], axis=0), wn)
        mms.append(mm2[:N_HEADS * t] + mm2[N_HEADS * t:])
    car = car_ref[:, 0:1]
    for b in range(nblocks):
        a = jnp.exp2(zs[b] + mms[b] + car)
        if vis is not None:
            a = jnp.where(vis, a, 0.0)
        a = a.astype(BF16)
        for h in heads:
            acc_ref[h] += av(b, h, a[h * t:(h + 1) * t, :])
        car = car + mms[b][:, 0:1]
    car_ref[...] = jnp.broadcast_to(car, car_ref.shape)


def _sbdec_kernel(q_ref, kn_ref, vn_ref, ck_ref, cv_ref, o_ref, qh_ref, acc_ref, car_ref, *, t_new):
    j = pl.program_id(1)
    hcols = lambda h: slice(h * SB_HEAD_DIM, (h + 1) * SB_HEAD_DIM)

    @pl.when(j == 0)
    def _():
        acc_ref[...] = jnp.zeros_like(acc_ref)
        car_ref[...] = jnp.zeros_like(car_ref)
        for h in range(N_HEADS):
            qh_ref[h] = q_ref[:, hcols(h)]
        kp = lax.broadcasted_iota(jnp.int32, (t_new, t_new), 0)
        kc = lax.broadcasted_iota(jnp.int32, (t_new, t_new), 1)
        wn_new = jnp.where(kp >= kc, -1.0, 0.0).astype(BF16)
        qi = lax.rem(lax.broadcasted_iota(jnp.int32, (N_HEADS * t_new, t_new), 0), t_new)
        ki = lax.broadcasted_iota(jnp.int32, (N_HEADS * t_new, t_new), 1)
        _sbdec_blocks(lambda b, h: _dot_nt(qh_ref[h], kn_ref[:, hcols(h)].astype(BF16)),
                      lambda b, h, a: _dot(a, vn_ref[:, hcols(h)].astype(BF16)),
                      1, t_new, acc_ref, car_ref, wn_new, ki < qi)

    kp = lax.broadcasted_iota(jnp.int32, (BLK, BLK), 0)
    kc = lax.broadcasted_iota(jnp.int32, (BLK, BLK), 1)
    wn = jnp.where(kp >= kc, -1.0, 0.0).astype(BF16)
    nsub = DEC_KEYS // BLK
    keys = lambda b: slice((nsub - 1 - b) * BLK, (nsub - b) * BLK)
    _sbdec_blocks(lambda b, h: _dot(qh_ref[h], ck_ref[0, 0, h, :, keys(b)].astype(BF16)),
                  lambda b, h, a: _dot_nt(a, cv_ref[0, 0, h, :, keys(b)].astype(BF16)),
                  nsub, t_new, acc_ref, car_ref, wn, None)

    @pl.when(j == pl.num_programs(1) - 1)
    def _():
        for h in range(N_HEADS):
            o_ref[:, hcols(h)] = acc_ref[h].astype(BF16)


def _sb_sample(q, k_new, v_new, cache_k, cache_v, layer, batch, t_new):
    past = cache_k.shape[4]
    assert past % DEC_KEYS == 0
    nkb = past // DEC_KEYS
    tok = lambda: pl.BlockSpec((t_new, SB_W), lambda b, j: (b, 0))
    cache = lambda: pl.BlockSpec((1, 1, N_HEADS, SB_HEAD_DIM, DEC_KEYS), lambda b, j: (layer, b, 0, 0, nkb - 1 - j))
    return pl.pallas_call(
        functools.partial(_sbdec_kernel, t_new=t_new),
        grid=(batch, nkb),
        in_specs=[tok(), tok(), tok(), cache(), cache()],
        out_specs=tok(),
        out_shape=jax.ShapeDtypeStruct((batch * t_new, SB_W), BF16),
        scratch_shapes=[pltpu.VMEM((N_HEADS, t_new, SB_HEAD_DIM), BF16),
                        pltpu.VMEM((N_HEADS, t_new, SB_HEAD_DIM), F32),
                        pltpu.VMEM((N_HEADS * t_new, LANES), F32)],
        compiler_params=_params("parallel", "arbitrary"), name="sb_sample",
    )(q, k_new, v_new, cache_k, cache_v)


def _mladec_kernel(q_ref, cn_ref, ck_ref, kr_ref, o_ref, *, t_new, past, new_visible):
    q = q_ref[...].reshape(N_HEADS * t_new, QM_W)
    q_lat = q[:, :KV_LORA]
    q_rope = q[:, KV_LORA:KV_LORA + MLA_ROPE]
    c2 = MLA_SCALE * LOG2E

    cn = cn_ref[...]
    s = _dot_nt(q, cn)
    if not new_visible:
        qi = past + lax.broadcasted_iota(jnp.int32, s.shape, 0) % t_new
        ki = past + lax.broadcasted_iota(jnp.int32, s.shape, 1)
        s = jnp.where((ki // CHUNK) <= (qi // CHUNK), s, -jnp.inf)
    m = jnp.max(s, axis=1, keepdims=True)
    p = jnp.exp2((s - m) * c2)
    l = jnp.sum(p, axis=1, keepdims=True)
    acc = _dot(p.astype(BF16), cn[:, :KV_LORA])

    for c in range(past // MLA_DEC_KEYS):
        rows = slice(c * MLA_DEC_KEYS, (c + 1) * MLA_DEC_KEYS)
        ck = ck_ref[0, 0, rows, :].astype(BF16)
        kr_t = kr_ref[0, 0, :, rows].astype(BF16)
        s = _dot_nt(q_lat, ck) + _dot(q_rope, kr_t)
        m_new = jnp.maximum(m, jnp.max(s, axis=1, keepdims=True))
        p = jnp.exp2((s - m_new) * c2)
        alpha = jnp.exp2((m - m_new) * c2)
        l = alpha * l + jnp.sum(p, axis=1, keepdims=True)
        acc = alpha * acc + _dot(p.astype(BF16), ck)
        m = m_new
    o = (acc * (1.0 / l)).astype(BF16)
    for h in range(N_HEADS):
        o_ref[:, h * KV_LORA:(h + 1) * KV_LORA] = o[h * t_new:(h + 1) * t_new, :]


def _mla_sample(qm, cm, cache_ckv, cache_kr, layer, batch, t_new):
    past = cache_ckv.shape[2]
    assert past % MLA_DEC_KEYS == 0 and past % CHUNK == 0
    q_pos = past + np.arange(t_new)
    new_visible = bool(((q_pos[None, :] // CHUNK) <= (q_pos[:, None] // CHUNK)).all())
    return pl.pallas_call(
        functools.partial(_mladec_kernel, t_new=t_new, past=past, new_visible=new_visible),
        grid=(batch,),
        in_specs=[pl.BlockSpec((N_HEADS, t_new, QM_W), lambda b: (0, b, 0)),
                  pl.BlockSpec((t_new, QM_W), lambda b: (b, 0)),
                  pl.BlockSpec((1, 1, past, KV_LORA), lambda b: (layer, b, 0, 0)),
                  pl.BlockSpec((1, 1, MLA_ROPE, past), lambda b: (layer, b, 0, 0))],
        out_specs=pl.BlockSpec((t_new, N_HEADS * KV_LORA), lambda b: (b, 0)),
        out_shape=jax.ShapeDtypeStruct((batch * t_new, N_HEADS * KV_LORA), BF16),
        compiler_params=_params("parallel"), name="mla_sample",
    )(qm, cm, cache_ckv, cache_kr)


def _mix_kernel(x_ref, gpre_ref, osb_ref, olat_ref, wg_ref, wpsb_ref, wuv_ref, wpmla_ref, wout_ref,
                gpost_ref, o_ref):
    x = x_ref[...]
    hb = _rmsnorm(x, gpre_ref[...]).astype(BF16)
    gates = _dot(hb, wg_ref[...])
    y_a = _dot(osb_ref[...], wpsb_ref[...])
    o_mla = _dot(olat_ref[...], wuv_ref[...]).astype(BF16)
    y_b = _dot(o_mla, wpmla_ref[...])
    mixin = _sigmoid(gates[:, :D_MODEL]) * y_a + _sigmoid(gates[:, D_MODEL:]) * y_b
    mix = _dot(mixin.astype(BF16), wout_ref[...])
    o_ref[...] = x + _rmsnorm(mix, gpost_ref[...])


def _mix(x2d, o_sb, o_lat, lw):
    n = x2d.shape[0]
    tm = BLK
    tok = lambda w: pl.BlockSpec((tm, w), lambda i: (i, 0))
    return pl.pallas_call(
        _mix_kernel, grid=(n // tm,),
        in_specs=[tok(D_MODEL), _const_spec((1, D_MODEL)), tok(SB_W), tok(N_HEADS * KV_LORA),
                  _const_spec(lw["w_gates"].shape), _const_spec(lw["w_proj_sb"].shape),
                  _const_spec(lw["w_uv_bd"].shape), _const_spec(lw["w_proj_mla"].shape),
                  _const_spec(lw["w_out"].shape), _const_spec((1, D_MODEL))],
        out_specs=tok(D_MODEL), out_shape=jax.ShapeDtypeStruct((n, D_MODEL), F32),
        compiler_params=_params("parallel"), name="mix",
    )(x2d, lw["g_pre_mix"], o_sb, o_lat, lw["w_gates"], lw["w_proj_sb"], lw["w_uv_bd"], lw["w_proj_mla"],
      lw["w_out"], lw["g_post_mix"])


def _ffn_kernel(x_ref, gpre_ref, wup_ref, cw_ref, past_ref, wdown_ref, gpost_ref, wpg_ref, p_ref, wpp_ref,
                o_ref, conv_ref, u_scr, act_scr, *, tm):
    t = pl.program_id(1)

    @pl.when(t == 0)
    def _():
        u_scr[:, SUBLANES - (CONV_W - 1):SUBLANES, :] = past_ref[0]

    x = x_ref[...]
    hb = _rmsnorm(x, gpre_ref[...]).astype(BF16)

    def up(c):
        for cc in (c, N_FF_CHUNKS + c):
            u_scr[cc, SUBLANES:, :] = _dot(hb, wup_ref[cc])

    def conv(c):
        cw = cw_ref[c]
        uc = cw[3:4, :]
        for i in range(CONV_W):
            uc = uc + u_scr[c, SUBLANES - (CONV_W - 1) + i:SUBLANES - (CONV_W - 1) + i + tm, :] * cw[i:i + 1, :]
        conv_ref[0, c] = u_scr[c, SUBLANES + tm - (CONV_W - 1):SUBLANES + tm, :]
        u_scr[c, 0:SUBLANES, :] = u_scr[c, tm:tm + SUBLANES, :]
        return uc

    up(0)
    f = None
    for c in range(N_FF_CHUNKS):
        if c + 1 < N_FF_CHUNKS:
            up(c + 1)
        act = _gelu_tanh(conv(c)) * conv(N_FF_CHUNKS + c)
        act_scr[:, c * FF_CHUNK:(c + 1) * FF_CHUNK] = act.astype(BF16)
        if (c + 1) % FF_DOWN_GROUP == 0 or c + 1 == N_FF_CHUNKS:
            k0 = (c // FF_DOWN_GROUP) * FF_DOWN_GROUP * FF_CHUNK
            k1 = (c + 1) * FF_CHUNK
            part = _dot(act_scr[:, k0:k1], wdown_ref[k0:k1, :])
            f = part if f is None else f + part
    x2 = x + _rmsnorm(f, gpost_ref[...])
    ple = _sigmoid(_dot(x2.astype(BF16), wpg_ref[...])) * _dot(p_ref[0].astype(BF16), wpp_ref[...])
    o_ref[...] = x2 + ple


def _ffn(x2d, p3d, layer, conv_past, batch, seq_len, lw):
    tm = min(BLK, seq_len)
    nt = seq_len // tm
    nch = 2 * N_FF_CHUNKS
    tok = lambda w: pl.BlockSpec((tm, w), lambda b, t: (b * nt + t, 0))
    ple = pl.BlockSpec((1, tm, D_PLE), lambda b, t: (layer, b * nt + t, 0))
    state = pl.BlockSpec((1, nch, CONV_W - 1, FF_CHUNK), lambda b, t: (b, 0, 0, 0))
    return pl.pallas_call(
        functools.partial(_ffn_kernel, tm=tm),
        grid=(batch, nt),
        in_specs=[tok(D_MODEL), _const_spec((1, D_MODEL)), _const_spec(lw["w_up"].shape),
                  _const_spec(lw["conv"].shape), state, _const_spec(lw["w_down"].shape),
                  _const_spec((1, D_MODEL)), _const_spec(lw["w_ple_gate"].shape), ple,
                  _const_spec(lw["w_ple_proj"].shape)],
        out_specs=[tok(D_MODEL), state],
        out_shape=[jax.ShapeDtypeStruct(x2d.shape, F32), jax.ShapeDtypeStruct(conv_past.shape, F32)],
        scratch_shapes=[pltpu.VMEM((nch, SUBLANES + tm, FF_CHUNK), F32), pltpu.VMEM((tm, D_FF), BF16)],
        compiler_params=_params("arbitrary", "arbitrary"), name="ffn",
    )(x2d, lw["g_pre_ffn"], lw["w_up"], lw["conv"], conv_past, lw["w_down"], lw["g_post_ffn"],
      lw["w_ple_gate"], p3d, lw["w_ple_proj"])


def _prep_layer(w_in, ckv_gain, w_uk, w_uv, w_proj_sb, w_proj_mla, w_out, g_pre_mix, g_post_mix, g_pre_ffn,
                g_post_ffn, w_up, conv_w, conv_b, w_down, w_ple_gate, w_ple_proj):
    o_qm = 3 * SB_W
    o_ckv = o_qm + N_HEADS * (MLA_NOPE + MLA_ROPE)
    o_kr = o_ckv + KV_LORA
    o_g = o_kr + MLA_ROPE
    wqm = w_in[:, o_qm:o_ckv].reshape(D_MODEL, N_HEADS, MLA_NOPE + MLA_ROPE)
    w_qn = wqm[:, :, :MLA_NOPE].reshape(D_MODEL, N_HEADS * MLA_NOPE)
    w_qr = jnp.concatenate([wqm[:, :, MLA_NOPE:MLA_NOPE + ROPE_HALF].reshape(D_MODEL, LANES),
                            wqm[:, :, MLA_NOPE + ROPE_HALF:].reshape(D_MODEL, LANES)], axis=1)
    w_kr = w_in[:, o_kr:o_g]
    w_kx = jnp.zeros((D_MODEL, 2 * LANES), F32)
    w_kx = w_kx.at[:, :ROPE_HALF].set(w_kr[:, :ROPE_HALF]).at[:, LANES:LANES + ROPE_HALF].set(w_kr[:, ROPE_HALF:])
    wa = jnp.zeros((N_HEADS, LANES, QM_W), F32)
    wb = np.zeros((N_HEADS, 2 * LANES, QM_W), np.float32)
    for h in range(N_HEADS):
        r0 = (h % 2) * MLA_NOPE
        wa = wa.at[h, r0:r0 + MLA_NOPE, :KV_LORA].set(w_uk[:, h, :].T)
        for i in range(ROPE_HALF):
            wb[h, h * ROPE_HALF + i, KV_LORA + i] = 1.0
            wb[h, LANES + h * ROPE_HALF + i, KV_LORA + ROPE_HALF + i] = 1.0
    w_uv_bd = jnp.zeros((N_HEADS * KV_LORA, N_HEADS * MLA_V), F32)
    for h in range(N_HEADS):
        w_uv_bd = w_uv_bd.at[h * KV_LORA:(h + 1) * KV_LORA, h * MLA_V:(h + 1) * MLA_V].set(w_uv[:, h, :])
    nch = 2 * N_FF_CHUNKS
    conv = jnp.concatenate([conv_w, conv_b[None, :], jnp.zeros((SUBLANES - CONV_W - 1, 2 * D_FF), F32)], axis=0)
    row = lambda g: g.reshape(1, -1)
    return dict(
        w_qkv=w_in[:, :o_qm].astype(BF16), w_qn=w_qn.astype(BF16), w_qr=w_qr.astype(BF16),
        w_ckv=w_in[:, o_ckv:o_kr].astype(BF16), w_kx=w_kx.astype(BF16), ckv_gain=row(ckv_gain),
        wa=wa.astype(BF16), wb=jnp.asarray(wb, BF16), eye=jnp.eye(LANES, dtype=BF16),
        w_gates=w_in[:, o_g:].astype(BF16), w_proj_sb=w_proj_sb.astype(BF16), w_uv_bd=w_uv_bd.astype(BF16),
        w_proj_mla=w_proj_mla.astype(BF16), w_out=w_out.astype(BF16),
        g_pre_mix=row(g_pre_mix), g_post_mix=row(g_post_mix), g_pre_ffn=row(g_pre_ffn), g_post_ffn=row(g_post_ffn),
        w_up=w_up.reshape(D_MODEL, nch, FF_CHUNK).transpose(1, 0, 2).astype(BF16),
        conv=conv.reshape(SUBLANES, nch, FF_CHUNK).transpose(1, 0, 2),
        w_down=w_down.astype(BF16),
        w_ple_gate=w_ple_gate.astype(BF16), w_ple_proj=w_ple_proj.astype(BF16))


def _rope_tables(pos, rows):
    freqs = jnp.power(ROPE_THETA, -jnp.arange(ROPE_HALF, dtype=F32) / ROPE_HALF)
    ang = pos.astype(F32)[:, None] * freqs[None, :]
    reps = max(1, rows // pos.shape[0])
    tile = lambda a: jnp.tile(a, (reps, LANES // ROPE_HALF))
    return tile(jnp.cos(ang)), tile(jnp.sin(ang))


def _conv_to_chunks(state):
    b = state.shape[0]
    return state.reshape(b, CONV_W - 1, 2 * N_FF_CHUNKS, FF_CHUNK).transpose(0, 2, 1, 3)


def _conv_from_chunks(state):
    b = state.shape[0]
    return state.transpose(0, 2, 1, 3).reshape(b, CONV_W - 1, 2 * D_FF)


def kernel(x_prompt, x_sample, p_prompt, p_sample, cache_sb_k, cache_sb_v, cache_mla_ckv, cache_mla_krope,
           state_ffn_conv, w_in, ckv_gain, w_uk, w_uv, w_proj_sb, w_proj_mla, w_out, g_pre_mix, g_post_mix,
           g_pre_ffn, g_post_ffn, w_up, conv_w, conv_b, w_down, w_ple_gate, w_ple_proj):
    bp, tp, _ = x_prompt.shape
    bs, ts, _ = x_sample.shape
    depth = w_in.shape[0]
    past = cache_sb_k.shape[2]
    assert tp % BLK == 0 and (bs * ts) % BLK == 0 and BLK % ts == 0
    tabs_p = _rope_tables(jnp.arange(tp, dtype=jnp.int32), BLK)
    tabs_s = _rope_tables(past + jnp.arange(ts, dtype=jnp.int32), BLK)
    xp = x_prompt.reshape(bp * tp, D_MODEL)
    xs = x_sample.reshape(bs * ts, D_MODEL)
    pp = p_prompt.reshape(depth, bp * tp, D_PLE)
    ps = p_sample.reshape(depth, bs * ts, D_PLE)
    ck_t = cache_sb_k.transpose(0, 1, 3, 4, 2)
    cv_t = cache_sb_v.transpose(0, 1, 3, 4, 2)
    kr_t = cache_mla_krope.transpose(0, 1, 3, 2)
    zero_conv = jnp.zeros((bp, 2 * N_FF_CHUNKS, CONV_W - 1, FF_CHUNK), F32)
    st_p, st_s = [], []
    for i in range(depth):
        lw = _prep_layer(w_in[i], ckv_gain[i], w_uk[i], w_uv[i], w_proj_sb[i], w_proj_mla[i], w_out[i],
                         g_pre_mix[i], g_post_mix[i], g_pre_ffn[i], g_post_ffn[i], w_up[i], conv_w[i],
                         conv_b[i], w_down[i], w_ple_gate[i], w_ple_proj[i])
        q, k_st, k_bf, v_st, vt, qm, cm, ct, c_st, r_st = _proj(xp, tp, tabs_p, lw, transposed=True)
        o_sb = _sb_prompt(q, k_bf, vt, bp, tp)
        o_lat = _mla_prompt(qm, cm, ct, bp, tp)
        x1 = _mix(xp, o_sb, o_lat, lw)
        xp, conv_p = _ffn(x1, pp, i, zero_conv, bp, tp, lw)
        st_p.append((k_st.reshape(bp, tp, N_HEADS, SB_HEAD_DIM), v_st.reshape(bp, tp, N_HEADS, SB_HEAD_DIM),
                     c_st.reshape(bp, tp, KV_LORA), r_st.reshape(bp, tp, MLA_ROPE), _conv_from_chunks(conv_p)))
        q, k_st, _, v_st, qm, cm, c_st, r_st = _proj(xs, ts, tabs_s, lw, transposed=False)
        o_sb = _sb_sample(q, k_st, v_st, ck_t, cv_t, i, bs, ts)
        o_lat = _mla_sample(qm, cm, cache_mla_ckv, kr_t, i, bs, ts)
        x1 = _mix(xs, o_sb, o_lat, lw)
        xs, conv_s = _ffn(x1, ps, i, _conv_to_chunks(state_ffn_conv[i]), bs, ts, lw)
        st_s.append((k_st.reshape(bs, ts, N_HEADS, SB_HEAD_DIM), v_st.reshape(bs, ts, N_HEADS, SB_HEAD_DIM),
                     c_st.reshape(bs, ts, KV_LORA), r_st.reshape(bs, ts, MLA_ROPE), _conv_from_chunks(conv_s)))
    stack = lambda sts, k: jnp.stack([s[k] for s in sts])
    return ((xp.reshape(bp, tp, D_MODEL), xs.reshape(bs, ts, D_MODEL))
            + tuple(stack(st_p, k) for k in range(5)) + tuple(stack(st_s, k) for k in range(5)))
```

```python
import functools
import math

import jax
import jax.numpy as jnp
import numpy as np
from jax import lax
from jax.experimental import pallas as pl
from jax.experimental.pallas import tpu as pltpu

D_MODEL = 1024
N_HEADS = 8
SB_HEAD_DIM = 64
SB_W = N_HEADS * SB_HEAD_DIM
MLA_NOPE = 64
MLA_ROPE = 32
ROPE_HALF = MLA_ROPE // 2
KV_LORA = 128
MLA_V = 64
D_FF = 2816
CONV_W = 3
CHUNK = 64
D_PLE = 256
ROPE_THETA = 10000.0
EPS = 1e-6
SB_SCALE = SB_HEAD_DIM ** -0.5
MLA_SCALE = (MLA_NOPE + MLA_ROPE) ** -0.5
LOG2E = math.log2(math.e)

LANES = 128
SUBLANES = 8
MXU_DIM = 256
VMEM_LIMIT_BYTES = 56 * 1024 * 1024

BLK = MXU_DIM
TOK_TILE = 2 * MXU_DIM
FF_CHUNK = MXU_DIM
N_FF_CHUNKS = D_FF // FF_CHUNK
FF_DOWN_GROUP = 4
QM_W = 2 * LANES
DEC_KEYS = 1024
MLA_DEC_KEYS = 512
HEAD_GROUP = MXU_DIM // SB_HEAD_DIM
N_GROUPS = N_HEADS // HEAD_GROUP

BF16 = jnp.bfloat16
F32 = jnp.float32


def _dot(a, b):
    return jnp.dot(a, b, preferred_element_type=F32)


def _dot_nt(a, b):
    return lax.dot_general(a, b, (((1,), (1,)), ((), ())), preferred_element_type=F32)


def _rmsnorm(x, g):
    ms = jnp.mean(x * x, axis=-1, keepdims=True)
    return x * lax.rsqrt(ms + EPS) * g


def _sigmoid(x):
    return 1.0 / (1.0 + jnp.exp(-x))


def _softplus2(z2):
    return jnp.maximum(z2, 0.0) + jnp.log2(1.0 + jnp.exp2(-jnp.abs(z2)))


def _gelu_tanh(x):
    c = math.sqrt(2.0 / math.pi)
    return 0.5 * x * (1.0 + jnp.tanh(c * (x + 0.044715 * (x * x * x))))


def _split_bf16(x):
    hi = x.astype(BF16)
    lo = (x - hi.astype(F32)).astype(BF16)
    return hi, lo


def _params(*sem):
    return pltpu.CompilerParams(dimension_semantics=sem, vmem_limit_bytes=VMEM_LIMIT_BYTES)


def _const_spec(shape):
    return pl.BlockSpec(shape, lambda *_: (0,) * len(shape), pipeline_mode=pl.Buffered(1))


def _proj_kernel(x_ref, g_ref, cos_ref, sin_ref, wqkv_ref, wqn_ref, wqr_ref, wckv_ref, wkx_ref,
                 gain_ref, wa_ref, wb_ref, eye_ref, *out_refs, transposed):
    if transposed:
        (q_ref, kst_ref, kbf_ref, vst_ref, vt_ref, qm_ref, cm_ref, ct_ref, cst_ref, rst_ref) = out_refs
    else:
        (q_ref, kst_ref, kbf_ref, vst_ref, qm_ref, cm_ref, cst_ref, rst_ref) = out_refs
    hb = _rmsnorm(x_ref[...], g_ref[...]).astype(BF16)

    qkv = _dot(hb, wqkv_ref[...])
    q_ref[...] = (qkv[:, :SB_W] * SB_SCALE).astype(BF16)
    k = qkv[:, SB_W:2 * SB_W]
    v = qkv[:, 2 * SB_W:]
    kst_ref[...] = k
    vst_ref[...] = v
    kbf_ref[...] = k.astype(BF16)
    nsub = x_ref.shape[0] // BLK
    if transposed:
        vb = v.astype(BF16)
        for s in range(nsub):
            for p in range(SB_W // LANES):
                vt_ref[0, s, p * LANES:(p + 1) * LANES, :] = _dot_nt(
                    eye_ref[...], vb[s * BLK:(s + 1) * BLK, p * LANES:(p + 1) * LANES]).astype(BF16)

    cos = cos_ref[...]
    sin = sin_ref[...]
    qn = _dot(hb, wqn_ref[...]).astype(BF16)
    qr = _dot(hb, wqr_ref[...])
    x1, x2 = qr[:, :LANES], qr[:, LANES:]
    r12 = jnp.concatenate([x1 * cos - x2 * sin, x1 * sin + x2 * cos], axis=1).astype(BF16)
    for h in range(N_HEADS):
        p = h // 2
        qm_ref[h] = (_dot(qn[:, p * LANES:(p + 1) * LANES], wa_ref[h]) + _dot(r12, wb_ref[h])).astype(BF16)

    ckv = _rmsnorm(_dot(hb, wckv_ref[...]), gain_ref[...])
    cst_ref[...] = ckv
    kx = _dot(hb, wkx_ref[...])
    k1, k2 = kx[:, :LANES], kx[:, LANES:]
    kr1 = k1 * cos - k2 * sin
    kr2 = k1 * sin + k2 * cos
    lane = lax.broadcasted_iota(jnp.int32, kr1.shape, 1)
    kr = jnp.where(lane < ROPE_HALF, kr1, pltpu.roll(kr2, ROPE_HALF, axis=1))
    rst_ref[...] = kr[:, :MLA_ROPE]
    ckv_b = ckv.astype(BF16)
    cm_ref[...] = jnp.concatenate([ckv_b, kr.astype(BF16)], axis=1)
    if transposed:
        for s in range(nsub):
            ct_ref[0, s] = _dot_nt(eye_ref[...], ckv_b[s * BLK:(s + 1) * BLK, :]).astype(BF16)


def _proj(x2d, seq_len, tabs, lw, *, transposed):
    n = x2d.shape[0]
    tm = TOK_TILE
    nt = n // tm
    cos, sin = tabs
    tab_blocks = cos.shape[0] // tm
    tok = lambda w: pl.BlockSpec((tm, w), lambda i: (i, 0))
    tab = pl.BlockSpec((tm, LANES), lambda i: (i % tab_blocks, 0))
    in_specs = [tok(D_MODEL), _const_spec((1, D_MODEL)), tab, tab,
                _const_spec(lw["w_qkv"].shape), _const_spec(lw["w_qn"].shape), _const_spec(lw["w_qr"].shape),
                _const_spec(lw["w_ckv"].shape), _const_spec(lw["w_kx"].shape), _const_spec((1, KV_LORA)),
                _const_spec(lw["wa"].shape), _const_spec(lw["wb"].shape), _const_spec((LANES, LANES))]
    out_shape = [jax.ShapeDtypeStruct((n, SB_W), BF16), jax.ShapeDtypeStruct((n, SB_W), F32),
                 jax.ShapeDtypeStruct((n, SB_W), BF16), jax.ShapeDtypeStruct((n, SB_W), F32)]
    out_specs = [tok(SB_W), tok(SB_W), tok(SB_W), tok(SB_W)]
    if transposed:
        assert seq_len % tm == 0
        nb = seq_len // tm
        sub = tm // BLK
        out_shape.append(jax.ShapeDtypeStruct((n // seq_len, nb * sub, SB_W, BLK), BF16))
        out_specs.append(pl.BlockSpec((1, sub, SB_W, BLK), lambda i: (i // nb, i % nb, 0, 0)))
    out_shape += [jax.ShapeDtypeStruct((N_HEADS, n, QM_W), BF16), jax.ShapeDtypeStruct((n, QM_W), BF16)]
    out_specs += [pl.BlockSpec((N_HEADS, tm, QM_W), lambda i: (0, i, 0)), tok(QM_W)]
    if transposed:
        out_shape.append(jax.ShapeDtypeStruct((n // seq_len, nb * sub, KV_LORA, BLK), BF16))
        out_specs.append(pl.BlockSpec((1, sub, KV_LORA, BLK), lambda i: (i // nb, i % nb, 0, 0)))
    out_shape += [jax.ShapeDtypeStruct((n, KV_LORA), F32), jax.ShapeDtypeStruct((n, MLA_ROPE), F32)]
    out_specs += [tok(KV_LORA), tok(MLA_ROPE)]
    return pl.pallas_call(
        functools.partial(_proj_kernel, transposed=transposed),
        grid=(nt,), in_specs=in_specs, out_specs=out_specs, out_shape=out_shape,
        compiler_params=_params("parallel"), name="proj_t" if transposed else "proj",
    )(x2d, lw["g_pre_mix"], cos, sin, lw["w_qkv"], lw["w_qn"], lw["w_qr"], lw["w_ckv"], lw["w_kx"],
      lw["ckv_gain"], lw["wa"], lw["wb"], lw["eye"])


def _sb_scores(q_ref, k_ref, qrows, j, z_out):
    lane = lax.broadcasted_iota(jnp.int32, (1, LANES), 1)
    krows = pl.ds(pl.multiple_of(j * BLK, BLK), BLK)
    for h in range(N_HEADS):
        p, hh = h // 2, h % 2
        cols = slice(p * LANES, (p + 1) * LANES)
        k2 = k_ref[0, krows, cols]
        head_lanes = (lane >= hh * SB_HEAD_DIM) & (lane < (hh + 1) * SB_HEAD_DIM)
        kh = jnp.where(head_lanes, k2, jnp.zeros_like(k2))
        z_out[h] = _dot_nt(kh, q_ref[0, qrows, cols]) * LOG2E


def _sb_block(z_in, vt_ref, j, acc_ref, car_ref, umat, vis):
    heads = range(N_HEADS)
    mms = []
    for h in heads:
        sp = _softplus2(z_in[h])
        if vis is not None:
            sp = jnp.where(vis, sp, 0.0)
        hi, lo = _split_bf16(sp)
        mms.append(_dot(umat, hi) + _dot(umat, lo))
    for h in heads:
        car = car_ref[h, 0:1, :]
        a = jnp.exp2(z_in[h] + mms[h] + car)
        if vis is not None:
            a = jnp.where(vis, a, 0.0)
        rows = slice(h * SB_HEAD_DIM, (h + 1) * SB_HEAD_DIM)
        acc_ref[rows, :] += _dot(vt_ref[0, j, rows, :], a.astype(BF16))
        car_ref[h, 0:1, :] = car + mms[h][0:1, :]


def _sb_kernel(q_ref, k_ref, vt_ref, o_ref, acc_ref, car_ref, z0_ref, z1_ref, *, nblk):
    kidx = lax.broadcasted_iota(jnp.int32, (BLK, BLK), 0)
    qidx = lax.broadcasted_iota(jnp.int32, (BLK, BLK), 1)
    umat = jnp.where(qidx >= kidx, -1.0, 0.0).astype(BF16)
    vis = kidx < qidx

    def q_block(i, carry):
        acc_ref[...] = jnp.zeros_like(acc_ref)
        car_ref[...] = jnp.zeros_like(car_ref)
        qrows = pl.ds(pl.multiple_of(i * BLK, BLK), BLK)

        def step(j, z_cur, z_next, mask):
            _sb_scores(q_ref, k_ref, qrows, jnp.maximum(j - 1, 0), z_next)
            _sb_block(z_cur, vt_ref, j, acc_ref, car_ref, umat, mask)

        _sb_scores(q_ref, k_ref, qrows, i, z0_ref)
        step(i, z0_ref, z1_ref, vis)

        def k_pair(m, c):
            j = i - 1 - 2 * m
            step(j, z1_ref, z0_ref, None)
            step(j - 1, z0_ref, z1_ref, None)
            return c

        lax.fori_loop(0, i // 2, k_pair, 0)

        @pl.when(i % 2 == 1)
        def _():
            step(0, z1_ref, z0_ref, None)

        o_ref[0, qrows, :] = acc_ref[...].T.astype(BF16)
        return carry

    lax.fori_loop(0, nblk, q_block, 0)


def _sb_prompt(q, kbf, vt, batch, seq_len):
    nblk = seq_len // BLK
    q3 = q.reshape(batch, seq_len, SB_W)
    k3 = kbf.reshape(batch, seq_len, SB_W)
    row = pl.BlockSpec((1, seq_len, SB_W), lambda b: (b, 0, 0))
    out = pl.pallas_call(
        functools.partial(_sb_kernel, nblk=nblk),
        grid=(batch,),
        in_specs=[row, row, pl.BlockSpec((1, nblk, SB_W, BLK), lambda b: (b, 0, 0, 0))],
        out_specs=row,
        out_shape=jax.ShapeDtypeStruct((batch, seq_len, SB_W), BF16),
        scratch_shapes=[pltpu.VMEM((SB_W, BLK), F32), pltpu.VMEM((N_HEADS, SUBLANES, BLK), F32),
                        pltpu.VMEM((N_HEADS, BLK, BLK), F32), pltpu.VMEM((N_HEADS, BLK, BLK), F32)],
        compiler_params=_params("parallel"), name="sb_prompt",
    )(q3, k3, vt)
    return out.reshape(batch * seq_len, SB_W)


def _mla_scores(q_ref, c_ref, qrows, j, s_out):
    c = c_ref[pl.ds(pl.multiple_of(j * BLK, BLK), BLK), :]
    for h in range(N_HEADS):
        s_out[h] = _dot_nt(c, q_ref[h, qrows, :])


def _mla_block(s_in, ct_ref, j, acc_ref, st_ref, vis):
    ct = ct_ref[0, j]
    for h in range(N_HEADS):
        s = s_in[h]
        if vis is not None:
            s = jnp.where(vis, s, -jnp.inf)
        m_old = st_ref[h, 0:1, :]
        m_new = jnp.maximum(m_old, jnp.max(s, axis=0, keepdims=True))
        p = jnp.exp2((s - m_new) * (MLA_SCALE * LOG2E))
        alpha = jnp.exp2((m_old - m_new) * (MLA_SCALE * LOG2E))
        st_ref[h, 1:2, :] = alpha * st_ref[h, 1:2, :] + jnp.sum(p, axis=0, keepdims=True)
        st_ref[h, 0:1, :] = m_new
        acc_ref[h] = alpha * acc_ref[h] + _dot(ct, p.astype(BF16))


def _mla_kernel(q_ref, c_ref, ct_ref, o_ref, acc_ref, st_ref, s0_ref, s1_ref, *, nblk):
    kidx = lax.broadcasted_iota(jnp.int32, (BLK, BLK), 0)
    qidx = lax.broadcasted_iota(jnp.int32, (BLK, BLK), 1)
    vis = (kidx // CHUNK) <= (qidx // CHUNK)

    def q_block(i, carry):
        acc_ref[...] = jnp.zeros_like(acc_ref)
        for h in range(N_HEADS):
            st_ref[h, 0:1, :] = jnp.full((1, BLK), -jnp.inf, F32)
            st_ref[h, 1:2, :] = jnp.zeros((1, BLK), F32)
        rows = pl.ds(pl.multiple_of(i * BLK, BLK), BLK)
        def step(j, j_next, s_cur, s_next, mask):
            _mla_scores(q_ref, c_ref, rows, j_next, s_next)
            _mla_block(s_cur, ct_ref, j, acc_ref, st_ref, mask)

        nxt = lambda j: jnp.minimum(j + 1, nblk - 1)
        _mla_scores(q_ref, c_ref, rows, i, s0_ref)
        step(i, 0, s0_ref, s1_ref, vis)

        def k_pair(m, c):
            j = 2 * m
            step(j, nxt(j), s1_ref, s0_ref, None)
            step(j + 1, nxt(j + 1), s0_ref, s1_ref, None)
            return c

        lax.fori_loop(0, i // 2, k_pair, 0)

        @pl.when(i % 2 == 1)
        def _():
            step(i - 1, nxt(i - 1), s1_ref, s0_ref, None)

        for h in range(N_HEADS):
            o = acc_ref[h] * (1.0 / st_ref[h, 1:2, :])
            o_ref[rows, h * KV_LORA:(h + 1) * KV_LORA] = o.T.astype(BF16)
        return carry

    lax.fori_loop(0, nblk, q_block, 0)


def _mla_prompt(qm, cm, ct, batch, seq_len):
    nblk = seq_len // BLK
    return pl.pallas_call(
        functools.partial(_mla_kernel, nblk=nblk),
        grid=(batch,),
        in_specs=[pl.BlockSpec((N_HEADS, seq_len, QM_W), lambda b: (0, b, 0)),
                  pl.BlockSpec((seq_len, QM_W), lambda b: (b, 0)),
                  pl.BlockSpec((1, nblk, KV_LORA, BLK), lambda b: (b, 0, 0, 0))],
        out_specs=pl.BlockSpec((seq_len, N_HEADS * KV_LORA), lambda b: (b, 0)),
        out_shape=jax.ShapeDtypeStruct((batch * seq_len, N_HEADS * KV_LORA), BF16),
        scratch_shapes=[pltpu.VMEM((N_HEADS, KV_LORA, BLK), F32), pltpu.VMEM((N_HEADS, SUBLANES, BLK), F32),
                        pltpu.VMEM((N_HEADS, BLK, BLK), F32), pltpu.VMEM((N_HEADS, BLK, BLK), F32)],
        compiler_params=_params("parallel"), name="mla_prompt",
    )(qm, cm, ct)


def _sbdec_blocks(score, av, nblocks, t, acc_ref, car_ref, wn, vis):
    heads = range(N_HEADS)
    zs = [jnp.concatenate([score(b, h) for h in heads], axis=0) * LOG2E
          for b in range(nblocks)]
    mms = []
    for b in range(nblocks):
        sp = _softplus2(zs[b])
        if vis is not None:
            sp = jnp.where(vis, sp, 0.0)
        hi, lo = _split_bf16(sp)
        mm2 = _dot(jnp.concatenate([hi, lo], axis=0), wn)
        mms.append(mm2[:N_HEADS * t] + mm2[N_HEADS * t:])
    car = car_ref[:, 0:1]
    for b in range(nblocks):
        a = jnp.exp2(zs[b] + mms[b] + car)
        if vis is not None:
            a = jnp.where(vis, a, 0.0)
        a = a.astype(BF16)
        for h in heads:
            acc_ref[h] += av(b, h, a[h * t:(h + 1) * t, :])
        car = car + mms[b][:, 0:1]
    car_ref[...] = jnp.broadcast_to(car, car_ref.shape)


def _sbdec_kernel(q_ref, kn_ref, vn_ref, ck_ref, cv_ref, o_ref, qh_ref, acc_ref, car_ref, *, t_new):
    j = pl.program_id(1)
    hcols = lambda h: slice(h * SB_HEAD_DIM, (h + 1) * SB_HEAD_DIM)

    @pl.when(j == 0)
    def _():
        acc_ref[...] = jnp.zeros_like(acc_ref)
        car_ref[...] = jnp.zeros_like(car_ref)
        for h in range(N_HEADS):
            qh_ref[h] = q_ref[:, hcols(h)]
        kp = lax.broadcasted_iota(jnp.int32, (t_new, t_new), 0)
        kc = lax.broadcasted_iota(jnp.int32, (t_new, t_new), 1)
        wn_new = jnp.where(kp >= kc, -1.0, 0.0).astype(BF16)
        qi = lax.rem(lax.broadcasted_iota(jnp.int32, (N_HEADS * t_new, t_new), 0), t_new)
        ki = lax.broadcasted_iota(jnp.int32, (N_HEADS * t_new, t_new), 1)
        _sbdec_blocks(lambda b, h: _dot_nt(qh_ref[h], kn_ref[:, hcols(h)].astype(BF16)),
                      lambda b, h, a: _dot(a, vn_ref[:, hcols(h)].astype(BF16)),
                      1, t_new, acc_ref, car_ref, wn_new, ki < qi)

    kp = lax.broadcasted_iota(jnp.int32, (BLK, BLK), 0)
    kc = lax.broadcasted_iota(jnp.int32, (BLK, BLK), 1)
    wn = jnp.where(kp >= kc, -1.0, 0.0).astype(BF16)
    nsub = DEC_KEYS // BLK
    keys = lambda b: slice((nsub - 1 - b) * BLK, (nsub - b) * BLK)
    _sbdec_blocks(lambda b, h: _dot(qh_ref[h], ck_ref[0, 0, h, :, keys(b)].astype(BF16)),
                  lambda b, h, a: _dot_nt(a, cv_ref[0, 0, h, :, keys(b)].astype(BF16)),
                  nsub, t_new, acc_ref, car_ref, wn, None)

    @pl.when(j == pl.num_programs(1) - 1)
    def _():
        for h in range(N_HEADS):
            o_ref[:, hcols(h)] = acc_ref[h].astype(BF16)


def _sb_sample(q, k_new, v_new, cache_k, cache_v, layer, batch, t_new):
    past = cache_k.shape[4]
    assert past % DEC_KEYS == 0
    nkb = past // DEC_KEYS
    tok = lambda: pl.BlockSpec((t_new, SB_W), lambda b, j: (b, 0))
    cache = lambda: pl.BlockSpec((1, 1, N_HEADS, SB_HEAD_DIM, DEC_KEYS), lambda b, j: (layer, b, 0, 0, nkb - 1 - j))
    return pl.pallas_call(
        functools.partial(_sbdec_kernel, t_new=t_new),
        grid=(batch, nkb),
        in_specs=[tok(), tok(), tok(), cache(), cache()],
        out_specs=tok(),
        out_shape=jax.ShapeDtypeStruct((batch * t_new, SB_W), BF16),
        scratch_shapes=[pltpu.VMEM((N_HEADS, t_new, SB_HEAD_DIM), BF16),
                        pltpu.VMEM((N_HEADS, t_new, SB_HEAD_DIM), F32),
                        pltpu.VMEM((N_HEADS * t_new, LANES), F32)],
        compiler_params=_params("parallel", "arbitrary"), name="sb_sample",
    )(q, k_new, v_new, cache_k, cache_v)


def _mladec_kernel(q_ref, cn_ref, ck_ref, kr_ref, o_ref, *, t_new, past, new_visible):
    q = q_ref[...].reshape(N_HEADS * t_new, QM_W)
    q_lat = q[:, :KV_LORA]
    q_rope = q[:, KV_LORA:KV_LORA + MLA_ROPE]
    c2 = MLA_SCALE * LOG2E

    cn = cn_ref[...]
    s = _dot_nt(q, cn)
    if not new_visible:
        qi = past + lax.broadcasted_iota(jnp.int32, s.shape, 0) % t_new
        ki = past + lax.broadcasted_iota(jnp.int32, s.shape, 1)
        s = jnp.where((ki // CHUNK) <= (qi // CHUNK), s, -jnp.inf)
    m = jnp.max(s, axis=1, keepdims=True)
    p = jnp.exp2((s - m) * c2)
    l = jnp.sum(p, axis=1, keepdims=True)
    acc = _dot(p.astype(BF16), cn[:, :KV_LORA])

    for c in range(past // MLA_DEC_KEYS):
        rows = slice(c * MLA_DEC_KEYS, (c + 1) * MLA_DEC_KEYS)
        ck = ck_ref[0, 0, rows, :].astype(BF16)
        kr_t = kr_ref[0, 0, :, rows].astype(BF16)
        s = _dot_nt(q_lat, ck) + _dot(q_rope, kr_t)
        m_new = jnp.maximum(m, jnp.max(s, axis=1, keepdims=True))
        p = jnp.exp2((s - m_new) * c2)
        alpha = jnp.exp2((m - m_new) * c2)
        l = alpha * l + jnp.sum(p, axis=1, keepdims=True)
        acc = alpha * acc + _dot(p.astype(BF16), ck)
        m = m_new
    o = (acc * (1.0 / l)).astype(BF16)
    for h in range(N_HEADS):
        o_ref[:, h * KV_LORA:(h + 1) * KV_LORA] = o[h * t_new:(h + 1) * t_new, :]


def _mla_sample(qm, cm, cache_ckv, cache_kr, layer, batch, t_new):
    past = cache_ckv.shape[2]
    assert past % MLA_DEC_KEYS == 0 and past % CHUNK == 0
    q_pos = past + np.arange(t_new)
    new_visible = bool(((q_pos[None, :] // CHUNK) <= (q_pos[:, None] // CHUNK)).all())
    return pl.pallas_call(
        functools.partial(_mladec_kernel, t_new=t_new, past=past, new_visible=new_visible),
        grid=(batch,),
        in_specs=[pl.BlockSpec((N_HEADS, t_new, QM_W), lambda b: (0, b, 0)),
                  pl.BlockSpec((t_new, QM_W), lambda b: (b, 0)),
                  pl.BlockSpec((1, 1, past, KV_LORA), lambda b: (layer, b, 0, 0)),
                  pl.BlockSpec((1, 1, MLA_ROPE, past), lambda b: (layer, b, 0, 0))],
        out_specs=pl.BlockSpec((t_new, N_HEADS * KV_LORA), lambda b: (b, 0)),
        out_shape=jax.ShapeDtypeStruct((batch * t_new, N_HEADS * KV_LORA), BF16),
        compiler_params=_params("parallel"), name="mla_sample",
    )(qm, cm, cache_ckv, cache_kr)


def _mix_kernel(x_ref, gpre_ref, osb_ref, olat_ref, wg_ref, wpsb_ref, wuv_ref, wpmla_ref, wout_ref,
                gpost_ref, o_ref):
    x = x_ref[...]
    hb = _rmsnorm(x, gpre_ref[...]).astype(BF16)
    gates = _dot(hb, wg_ref[...])
    y_a = _dot(osb_ref[...], wpsb_ref[...])
    o_mla = _dot(olat_ref[...], wuv_ref[...]).astype(BF16)
    y_b = _dot(o_mla, wpmla_ref[...])
    mixin = _sigmoid(gates[:, :D_MODEL]) * y_a + _sigmoid(gates[:, D_MODEL:]) * y_b
    mix = _dot(mixin.astype(BF16), wout_ref[...])
    o_ref[...] = x + _rmsnorm(mix, gpost_ref[...])


def _mix(x2d, o_sb, o_lat, lw):
    n = x2d.shape[0]
    tm = TOK_TILE
    tok = lambda w: pl.BlockSpec((tm, w), lambda i: (i, 0))
    return pl.pallas_call(
        _mix_kernel, grid=(n // tm,),
        in_specs=[tok(D_MODEL), _const_spec((1, D_MODEL)), tok(SB_W), tok(N_HEADS * KV_LORA),
                  _const_spec(lw["w_gates"].shape), _const_spec(lw["w_proj_sb"].shape),
                  _const_spec(lw["w_uv_bd"].shape), _const_spec(lw["w_proj_mla"].shape),
                  _const_spec(lw["w_out"].shape), _const_spec((1, D_MODEL))],
        out_specs=tok(D_MODEL), out_shape=jax.ShapeDtypeStruct((n, D_MODEL), F32),
        compiler_params=_params("parallel"), name="mix",
    )(x2d, lw["g_pre_mix"], o_sb, o_lat, lw["w_gates"], lw["w_proj_sb"], lw["w_uv_bd"], lw["w_proj_mla"],
      lw["w_out"], lw["g_post_mix"])


def _ffn_kernel(x_ref, gpre_ref, wup_ref, cw_ref, past_ref, wdown_ref, gpost_ref, wpg_ref, p_ref, wpp_ref,
                o_ref, conv_ref, *u_scr, tm):
    t = pl.program_id(1)

    @pl.when(t == 0)
    def _():
        for c in range(2 * N_FF_CHUNKS):
            u_scr[c][SUBLANES - (CONV_W - 1):SUBLANES, :] = past_ref[0, c]

    x = x_ref[...]
    hb = _rmsnorm(x, gpre_ref[...]).astype(BF16)

    def up(c):
        for cc in (c, N_FF_CHUNKS + c):
            u_scr[cc][SUBLANES:, :] = _dot(hb, wup_ref[cc])

    def conv(c):
        cw = cw_ref[c]
        u = u_scr[c]
        uc = cw[3:4, :]
        for i in range(CONV_W):
            uc = uc + u[SUBLANES - (CONV_W - 1) + i:SUBLANES - (CONV_W - 1) + i + tm, :] * cw[i:i + 1, :]
        conv_ref[0, c] = u[SUBLANES + tm - (CONV_W - 1):SUBLANES + tm, :]
        u[0:SUBLANES, :] = u[tm:tm + SUBLANES, :]
        return uc

    up(0)
    f = None
    acts = []
    for c in range(N_FF_CHUNKS):
        if c + 1 < N_FF_CHUNKS:
            up(c + 1)
        acts.append((_gelu_tanh(conv(c)) * conv(N_FF_CHUNKS + c)).astype(BF16))
        if len(acts) == FF_DOWN_GROUP or c + 1 == N_FF_CHUNKS:
            k1 = (c + 1) * FF_CHUNK
            k0 = k1 - len(acts) * FF_CHUNK
            part = _dot(jnp.concatenate(acts, axis=1), wdown_ref[k0:k1, :])
            f = part if f is None else f + part
            acts = []
    x2 = x + _rmsnorm(f, gpost_ref[...])
    ple = _sigmoid(_dot(x2.astype(BF16), wpg_ref[...])) * _dot(p_ref[0].astype(BF16), wpp_ref[...])
    o_ref[...] = x2 + ple


def _ffn(x2d, p3d, layer, conv_past, batch, seq_len, lw):
    tm = min(TOK_TILE, seq_len)
    nt = seq_len // tm
    nch = 2 * N_FF_CHUNKS
    tok = lambda w: pl.BlockSpec((tm, w), lambda b, t: (b * nt + t, 0))
    ple = pl.BlockSpec((1, tm, D_PLE), lambda b, t: (layer, b * nt + t, 0))
    state = pl.BlockSpec((1, nch, CONV_W - 1, FF_CHUNK), lambda b, t: (b, 0, 0, 0))
    return pl.pallas_call(
        functools.partial(_ffn_kernel, tm=tm),
        grid=(batch, nt),
        in_specs=[tok(D_MODEL), _const_spec((1, D_MODEL)), _const_spec(lw["w_up"].shape),
                  _const_spec(lw["conv"].shape), state, _const_spec(lw["w_down"].shape),
                  _const_spec((1, D_MODEL)), _const_spec(lw["w_ple_gate"].shape), ple,
                  _const_spec(lw["w_ple_proj"].shape)],
        out_specs=[tok(D_MODEL), state],
        out_shape=[jax.ShapeDtypeStruct(x2d.shape, F32), jax.ShapeDtypeStruct(conv_past.shape, F32)],
        scratch_shapes=[pltpu.VMEM((SUBLANES + tm, FF_CHUNK), F32) for _ in range(nch)],
        compiler_params=_params("arbitrary", "arbitrary"), name="ffn",
    )(x2d, lw["g_pre_ffn"], lw["w_up"], lw["conv"], conv_past, lw["w_down"], lw["g_post_ffn"],
      lw["w_ple_gate"], p3d, lw["w_ple_proj"])


def _prep_layer(w_in, ckv_gain, w_uk, w_uv, w_proj_sb, w_proj_mla, w_out, g_pre_mix, g_post_mix, g_pre_ffn,
                g_post_ffn, w_up, conv_w, conv_b, w_down, w_ple_gate, w_ple_proj):
    o_qm = 3 * SB_W
    o_ckv = o_qm + N_HEADS * (MLA_NOPE + MLA_ROPE)
    o_kr = o_ckv + KV_LORA
    o_g = o_kr + MLA_ROPE
    wqm = w_in[:, o_qm:o_ckv].reshape(D_MODEL, N_HEADS, MLA_NOPE + MLA_ROPE)
    w_qn = wqm[:, :, :MLA_NOPE].reshape(D_MODEL, N_HEADS * MLA_NOPE)
    w_qr = jnp.concatenate([wqm[:, :, MLA_NOPE:MLA_NOPE + ROPE_HALF].reshape(D_MODEL, LANES),
                            wqm[:, :, MLA_NOPE + ROPE_HALF:].reshape(D_MODEL, LANES)], axis=1)
    w_kr = w_in[:, o_kr:o_g]
    w_kx = jnp.zeros((D_MODEL, 2 * LANES), F32)
    w_kx = w_kx.at[:, :ROPE_HALF].set(w_kr[:, :ROPE_HALF]).at[:, LANES:LANES + ROPE_HALF].set(w_kr[:, ROPE_HALF:])
    wa = jnp.zeros((N_HEADS, LANES, QM_W), F32)
    wb = np.zeros((N_HEADS, 2 * LANES, QM_W), np.float32)
    for h in range(N_HEADS):
        r0 = (h % 2) * MLA_NOPE
        wa = wa.at[h, r0:r0 + MLA_NOPE, :KV_LORA].set(w_uk[:, h, :].T)
        for i in range(ROPE_HALF):
            wb[h, h * ROPE_HALF + i, KV_LORA + i] = 1.0
            wb[h, LANES + h * ROPE_HALF + i, KV_LORA + ROPE_HALF + i] = 1.0
    w_uv_bd = jnp.zeros((N_HEADS * KV_LORA, N_HEADS * MLA_V), F32)
    for h in range(N_HEADS):
        w_uv_bd = w_uv_bd.at[h * KV_LORA:(h + 1) * KV_LORA, h * MLA_V:(h + 1) * MLA_V].set(w_uv[:, h, :])
    nch = 2 * N_FF_CHUNKS
    conv = jnp.concatenate([conv_w, conv_b[None, :], jnp.zeros((SUBLANES - CONV_W - 1, 2 * D_FF), F32)], axis=0)
    row = lambda g: g.reshape(1, -1)
    return dict(
        w_qkv=w_in[:, :o_qm].astype(BF16), w_qn=w_qn.astype(BF16), w_qr=w_qr.astype(BF16),
        w_ckv=w_in[:, o_ckv:o_kr].astype(BF16), w_kx=w_kx.astype(BF16), ckv_gain=row(ckv_gain),
        wa=wa.astype(BF16), wb=jnp.asarray(wb, BF16), eye=jnp.eye(LANES, dtype=BF16),
        w_gates=w_in[:, o_g:].astype(BF16), w_proj_sb=w_proj_sb.astype(BF16), w_uv_bd=w_uv_bd.astype(BF16),
        w_proj_mla=w_proj_mla.astype(BF16), w_out=w_out.astype(BF16),
        g_pre_mix=row(g_pre_mix), g_post_mix=row(g_post_mix), g_pre_ffn=row(g_pre_ffn), g_post_ffn=row(g_post_ffn),
        w_up=w_up.reshape(D_MODEL, nch, FF_CHUNK).transpose(1, 0, 2).astype(BF16),
        conv=conv.reshape(SUBLANES, nch, FF_CHUNK).transpose(1, 0, 2),
        w_down=w_down.astype(BF16),
        w_ple_gate=w_ple_gate.astype(BF16), w_ple_proj=w_ple_proj.astype(BF16))


def _rope_tables(pos, rows):
    freqs = jnp.power(ROPE_THETA, -jnp.arange(ROPE_HALF, dtype=F32) / ROPE_HALF)
    ang = pos.astype(F32)[:, None] * freqs[None, :]
    reps = max(1, rows // pos.shape[0])
    tile = lambda a: jnp.tile(a, (reps, LANES // ROPE_HALF))
    return tile(jnp.cos(ang)), tile(jnp.sin(ang))


def _conv_to_chunks(state):
    b = state.shape[0]
    return state.reshape(b, CONV_W - 1, 2 * N_FF_CHUNKS, FF_CHUNK).transpose(0, 2, 1, 3)


def _conv_from_chunks(state):
    b = state.shape[0]
    return state.transpose(0, 2, 1, 3).reshape(b, CONV_W - 1, 2 * D_FF)


def kernel(x_prompt, x_sample, p_prompt, p_sample, cache_sb_k, cache_sb_v, cache_mla_ckv, cache_mla_krope,
           state_ffn_conv, w_in, ckv_gain, w_uk, w_uv, w_proj_sb, w_proj_mla, w_out, g_pre_mix, g_post_mix,
           g_pre_ffn, g_post_ffn, w_up, conv_w, conv_b, w_down, w_ple_gate, w_ple_proj):
    bp, tp, _ = x_prompt.shape
    bs, ts, _ = x_sample.shape
    depth = w_in.shape[0]
    past = cache_sb_k.shape[2]
    assert tp % TOK_TILE == 0 and (bs * ts) % TOK_TILE == 0 and TOK_TILE % ts == 0
    tabs_p = _rope_tables(jnp.arange(tp, dtype=jnp.int32), TOK_TILE)
    tabs_s = _rope_tables(past + jnp.arange(ts, dtype=jnp.int32), TOK_TILE)
    xp = x_prompt.reshape(bp * tp, D_MODEL)
    xs = x_sample.reshape(bs * ts, D_MODEL)
    pp = p_prompt.reshape(depth, bp * tp, D_PLE)
    ps = p_sample.reshape(depth, bs * ts, D_PLE)
    ck_t = cache_sb_k.transpose(0, 1, 3, 4, 2)
    cv_t = cache_sb_v.transpose(0, 1, 3, 4, 2)
    kr_t = cache_mla_krope.transpose(0, 1, 3, 2)
    zero_conv = jnp.zeros((bp, 2 * N_FF_CHUNKS, CONV_W - 1, FF_CHUNK), F32)
    st_p, st_s = [], []
    for i in range(depth):
        lw = _prep_layer(w_in[i], ckv_gain[i], w_uk[i], w_uv[i], w_proj_sb[i], w_proj_mla[i], w_out[i],
                         g_pre_mix[i], g_post_mix[i], g_pre_ffn[i], g_post_ffn[i], w_up[i], conv_w[i],
                         conv_b[i], w_down[i], w_ple_gate[i], w_ple_proj[i])
        q, k_st, k_bf, v_st, vt, qm, cm, ct, c_st, r_st = _proj(xp, tp, tabs_p, lw, transposed=True)
        o_sb = _sb_prompt(q, k_bf, vt, bp, tp)
        o_lat = _mla_prompt(qm, cm, ct, bp, tp)
        x1 = _mix(xp, o_sb, o_lat, lw)
        xp, conv_p = _ffn(x1, pp, i, zero_conv, bp, tp, lw)
        st_p.append((k_st.reshape(bp, tp, N_HEADS, SB_HEAD_DIM), v_st.reshape(bp, tp, N_HEADS, SB_HEAD_DIM),
                     c_st.reshape(bp, tp, KV_LORA), r_st.reshape(bp, tp, MLA_ROPE), _conv_from_chunks(conv_p)))
        q, k_st, _, v_st, qm, cm, c_st, r_st = _proj(xs, ts, tabs_s, lw, transposed=False)
        o_sb = _sb_sample(q, k_st, v_st, ck_t, cv_t, i, bs, ts)
        o_lat = _mla_sample(qm, cm, cache_mla_ckv, kr_t, i, bs, ts)
        x1 = _mix(xs, o_sb, o_lat, lw)
        xs, conv_s = _ffn(x1, ps, i, _conv_to_chunks(state_ffn_conv[i]), bs, ts, lw)
        st_s.append((k_st.reshape(bs, ts, N_HEADS, SB_HEAD_DIM), v_st.reshape(bs, ts, N_HEADS, SB_HEAD_DIM),
                     c_st.reshape(bs, ts, KV_LORA), r_st.reshape(bs, ts, MLA_ROPE), _conv_from_chunks(conv_s)))
    stack = lambda sts, k: jnp.stack([s[k] for s in sts])
    return ((xp.reshape(bp, tp, D_MODEL), xs.reshape(bs, ts, D_MODEL))
            + tuple(stack(st_p, k) for k in range(5)) + tuple(stack(st_s, k) for k in range(5)))
```

```python
import functools
import math

import jax
import jax.numpy as jnp
import numpy as np
from jax import lax
from jax.experimental import pallas as pl
from jax.experimental.pallas import tpu as pltpu

D_MODEL = 1024
N_HEADS = 8
SB_HEAD_DIM = 64
SB_W = N_HEADS * SB_HEAD_DIM
MLA_NOPE = 64
MLA_ROPE = 32
ROPE_HALF = MLA_ROPE // 2
KV_LORA = 128
MLA_V = 64
D_FF = 2816
CONV_W = 3
CHUNK = 64
D_PLE = 256
ROPE_THETA = 10000.0
EPS = 1e-6
SB_SCALE = SB_HEAD_DIM ** -0.5
MLA_SCALE = (MLA_NOPE + MLA_ROPE) ** -0.5
LOG2E = math.log2(math.e)

LANES = 128
SUBLANES = 8
MXU_DIM = 256
VMEM_LIMIT_BYTES = 56 * 1024 * 1024

BLK = MXU_DIM
TOK_TILE = 2 * MXU_DIM
FF_CHUNK = MXU_DIM
N_FF_CHUNKS = D_FF // FF_CHUNK
FF_DOWN_GROUP = 4
QM_W = 2 * LANES
DEC_KEYS = 1024
MLA_DEC_KEYS = 512
HEAD_GROUP = MXU_DIM // SB_HEAD_DIM
N_GROUPS = N_HEADS // HEAD_GROUP

BF16 = jnp.bfloat16
F32 = jnp.float32


def _dot(a, b):
    return jnp.dot(a, b, preferred_element_type=F32)


def _dot_nt(a, b):
    return lax.dot_general(a, b, (((1,), (1,)), ((), ())), preferred_element_type=F32)


def _rmsnorm(x, g):
    ms = jnp.mean(x * x, axis=-1, keepdims=True)
    return x * lax.rsqrt(ms + EPS) * g


def _sigmoid(x):
    return 1.0 / (1.0 + jnp.exp(-x))


def _softplus2(z2):
    return jnp.maximum(z2, 0.0) + jnp.log2(1.0 + jnp.exp2(-jnp.abs(z2)))


def _gelu_tanh(x):
    c = math.sqrt(2.0 / math.pi)
    return 0.5 * x * (1.0 + jnp.tanh(c * (x + 0.044715 * (x * x * x))))


def _split_bf16(x):
    hi = x.astype(BF16)
    lo = (x - hi.astype(F32)).astype(BF16)
    return hi, lo


def _params(*sem):
    return pltpu.CompilerParams(dimension_semantics=sem, vmem_limit_bytes=VMEM_LIMIT_BYTES)


def _const_spec(shape):
    return pl.BlockSpec(shape, lambda *_: (0,) * len(shape), pipeline_mode=pl.Buffered(1))


def _proj_kernel(x_ref, g_ref, cos_ref, sin_ref, wqkv_ref, wkvt_ref, wqn_ref, wqr_ref, wckv_ref, wkx_ref,
                 gain_ref, wa_ref, wb_ref, eye_ref, *refs, transposed, n_alias):
    out_refs = refs[n_alias:]
    if transposed:
        (q_ref, kbf_ref, ktst_ref, vtst_ref, vt_ref, qm_ref, cm_ref, ct_ref, cst_ref, rst_ref) = out_refs
    else:
        (q_ref, kst_ref, vst_ref, qm_ref, cm_ref, cst_ref, rst_ref) = out_refs
    hb = _rmsnorm(x_ref[...], g_ref[...]).astype(BF16)
    nsub = x_ref.shape[0] // BLK

    if transposed:
        qk = _dot(hb, wqkv_ref[:, :2 * SB_W])
        q_ref[...] = (qk[:, :SB_W] * SB_SCALE).astype(BF16)
        kbf_ref[...] = qk[:, SB_W:].astype(BF16)
        kt = _dot_nt(wkvt_ref[:SB_W, :], hb)
        vt = _dot_nt(wkvt_ref[SB_W:, :], hb)
        ktst_ref[0, 0] = kt
        vtst_ref[0, 0] = vt
        for s in range(nsub):
            vt_ref[0, s] = vt[:, s * BLK:(s + 1) * BLK].astype(BF16)
    else:
        qkv = _dot(hb, wqkv_ref[...])
        q_ref[...] = (qkv[:, :SB_W] * SB_SCALE).astype(BF16)
        kst_ref[...] = qkv[:, SB_W:2 * SB_W]
        vst_ref[...] = qkv[:, 2 * SB_W:]

    cos = cos_ref[...]
    sin = sin_ref[...]
    qn = _dot(hb, wqn_ref[...]).astype(BF16)
    qr = _dot(hb, wqr_ref[...])
    x1, x2 = qr[:, :LANES], qr[:, LANES:]
    r12 = jnp.concatenate([x1 * cos - x2 * sin, x1 * sin + x2 * cos], axis=1).astype(BF16)
    for h in range(N_HEADS):
        p = h // 2
        qm_ref[h] = (_dot(qn[:, p * LANES:(p + 1) * LANES], wa_ref[h]) + _dot(r12, wb_ref[h])).astype(BF16)

    ckv = _rmsnorm(_dot(hb, wckv_ref[...]), gain_ref[...])
    cst_ref[...] = ckv
    kx = _dot(hb, wkx_ref[...])
    k1, k2 = kx[:, :LANES], kx[:, LANES:]
    kr1 = k1 * cos - k2 * sin
    kr2 = k1 * sin + k2 * cos
    lane = lax.broadcasted_iota(jnp.int32, kr1.shape, 1)
    kr = jnp.where(lane < ROPE_HALF, kr1, pltpu.roll(kr2, ROPE_HALF, axis=1))
    rst_ref[...] = kr[:, :MLA_ROPE]
    ckv_b = ckv.astype(BF16)
    cm_ref[...] = jnp.concatenate([ckv_b, kr.astype(BF16)], axis=1)
    if transposed:
        for s in range(nsub):
            ct_ref[0, s] = _dot_nt(eye_ref[...], ckv_b[s * BLK:(s + 1) * BLK, :]).astype(BF16)


def _proj(x2d, seq_len, tabs, lw, *, kv_states=None):
    n = x2d.shape[0]
    tm = TOK_TILE
    nt = n // tm
    cos, sin = tabs
    tab_blocks = cos.shape[0] // tm
    transposed = kv_states is not None
    tok = lambda w: pl.BlockSpec((tm, w), lambda i: (i, 0))
    tab = pl.BlockSpec((tm, LANES), lambda i: (i % tab_blocks, 0))
    in_specs = [tok(D_MODEL), _const_spec((1, D_MODEL)), tab, tab,
                _const_spec(lw["w_qkv"].shape), _const_spec(lw["w_kvt"].shape),
                _const_spec(lw["w_qn"].shape), _const_spec(lw["w_qr"].shape),
                _const_spec(lw["w_ckv"].shape), _const_spec(lw["w_kx"].shape), _const_spec((1, KV_LORA)),
                _const_spec(lw["wa"].shape), _const_spec(lw["wb"].shape), _const_spec((LANES, LANES))]
    args = [x2d, lw["g_pre_mix"], cos, sin, lw["w_qkv"], lw["w_kvt"], lw["w_qn"], lw["w_qr"], lw["w_ckv"],
            lw["w_kx"], lw["ckv_gain"], lw["wa"], lw["wb"], lw["eye"]]
    aliases = {}
    if transposed:
        layer, depth, prev = kv_states
        assert seq_len % tm == 0
        batch = n // seq_len
        nb = seq_len // tm
        sub = tm // BLK
        state = jax.ShapeDtypeStruct((depth, batch, SB_W, seq_len), F32)
        state_spec = pl.BlockSpec((1, 1, SB_W, tm), lambda i: (layer, i // nb, 0, i % nb))
        blocks = lambda w: (jax.ShapeDtypeStruct((batch, nb * sub, w, BLK), BF16),
                            pl.BlockSpec((1, sub, w, BLK), lambda i: (i // nb, i % nb, 0, 0)))
        outs = [(jax.ShapeDtypeStruct((n, SB_W), BF16), tok(SB_W)),
                (jax.ShapeDtypeStruct((n, SB_W), BF16), tok(SB_W)),
                (state, state_spec), (state, state_spec), blocks(SB_W)]
        if prev is not None:
            aliases = {len(args): 2, len(args) + 1: 3}
            args += list(prev)
            in_specs += [pl.BlockSpec(memory_space=pl.ANY)] * 2
    else:
        outs = [(jax.ShapeDtypeStruct((n, SB_W), BF16), tok(SB_W)),
                (jax.ShapeDtypeStruct((n, SB_W), F32), tok(SB_W)), (jax.ShapeDtypeStruct((n, SB_W), F32), tok(SB_W))]
    outs += [(jax.ShapeDtypeStruct((N_HEADS, n, QM_W), BF16), pl.BlockSpec((N_HEADS, tm, QM_W), lambda i: (0, i, 0))),
             (jax.ShapeDtypeStruct((n, QM_W), BF16), tok(QM_W))]
    if transposed:
        outs.append(blocks(KV_LORA))
    outs += [(jax.ShapeDtypeStruct((n, KV_LORA), F32), tok(KV_LORA)),
             (jax.ShapeDtypeStruct((n, MLA_ROPE), F32), tok(MLA_ROPE))]
    return pl.pallas_call(
        functools.partial(_proj_kernel, transposed=transposed, n_alias=len(aliases)),
        grid=(nt,), in_specs=in_specs, out_specs=[o[1] for o in outs], out_shape=[o[0] for o in outs],
        input_output_aliases=aliases,
        compiler_params=_params("parallel"), name="proj_t" if transposed else "proj",
    )(*args)


def _sb_scores(q_ref, k_ref, qrows, j, z_out):
    lane = lax.broadcasted_iota(jnp.int32, (1, LANES), 1)
    krows = pl.ds(pl.multiple_of(j * BLK, BLK), BLK)
    for h in range(N_HEADS):
        p, hh = h // 2, h % 2
        cols = slice(p * LANES, (p + 1) * LANES)
        k2 = k_ref[0, krows, cols]
        head_lanes = (lane >= hh * SB_HEAD_DIM) & (lane < (hh + 1) * SB_HEAD_DIM)
        kh = jnp.where(head_lanes, k2, jnp.zeros_like(k2))
        z_out[h] = _dot_nt(kh, q_ref[0, qrows, cols]) * LOG2E


def _sb_block(z_in, vt_ref, j, acc_ref, car_ref, umat, vis):
    heads = range(N_HEADS)
    mms = []
    for h in heads:
        sp = _softplus2(z_in[h])
        if vis is not None:
            sp = jnp.where(vis, sp, 0.0)
        hi, lo = _split_bf16(sp)
        mms.append(_dot(umat, hi) + _dot(umat, lo))
    for h in heads:
        car = car_ref[h, 0:1, :]
        a = jnp.exp2(z_in[h] + mms[h] + car)
        if vis is not None:
            a = jnp.where(vis, a, 0.0)
        rows = slice(h * SB_HEAD_DIM, (h + 1) * SB_HEAD_DIM)
        acc_ref[rows, :] += _dot(vt_ref[0, j, rows, :], a.astype(BF16))
        car_ref[h, 0:1, :] = car + mms[h][0:1, :]


def _sb_kernel(q_ref, k_ref, vt_ref, o_ref, acc_ref, car_ref, z0_ref, z1_ref, *, nblk):
    kidx = lax.broadcasted_iota(jnp.int32, (BLK, BLK), 0)
    qidx = lax.broadcasted_iota(jnp.int32, (BLK, BLK), 1)
    umat = jnp.where(qidx >= kidx, -1.0, 0.0).astype(BF16)
    vis = kidx < qidx

    def q_block(i, carry):
        acc_ref[...] = jnp.zeros_like(acc_ref)
        car_ref[...] = jnp.zeros_like(car_ref)
        qrows = pl.ds(pl.multiple_of(i * BLK, BLK), BLK)

        def step(j, z_cur, z_next, mask):
            _sb_scores(q_ref, k_ref, qrows, jnp.maximum(j - 1, 0), z_next)
            _sb_block(z_cur, vt_ref, j, acc_ref, car_ref, umat, mask)

        _sb_scores(q_ref, k_ref, qrows, i, z0_ref)
        step(i, z0_ref, z1_ref, vis)

        def k_pair(m, c):
            j = i - 1 - 2 * m
            step(j, z1_ref, z0_ref, None)
            step(j - 1, z0_ref, z1_ref, None)
            return c

        lax.fori_loop(0, i // 2, k_pair, 0)

        @pl.when(i % 2 == 1)
        def _():
            step(0, z1_ref, z0_ref, None)

        o_ref[0, qrows, :] = acc_ref[...].T.astype(BF16)
        return carry

    lax.fori_loop(0, nblk, q_block, 0)


def _sb_prompt(q, kbf, vt, batch, seq_len):
    nblk = seq_len // BLK
    q3 = q.reshape(batch, seq_len, SB_W)
    k3 = kbf.reshape(batch, seq_len, SB_W)
    row = pl.BlockSpec((1, seq_len, SB_W), lambda b: (b, 0, 0))
    out = pl.pallas_call(
        functools.partial(_sb_kernel, nblk=nblk),
        grid=(batch,),
        in_specs=[row, row, pl.BlockSpec((1, nblk, SB_W, BLK), lambda b: (b, 0, 0, 0))],
        out_specs=row,
        out_shape=jax.ShapeDtypeStruct((batch, seq_len, SB_W), BF16),
        scratch_shapes=[pltpu.VMEM((SB_W, BLK), F32), pltpu.VMEM((N_HEADS, SUBLANES, BLK), F32),
                        pltpu.VMEM((N_HEADS, BLK, BLK), F32), pltpu.VMEM((N_HEADS, BLK, BLK), F32)],
        compiler_params=_params("parallel"), name="sb_prompt",
    )(q3, k3, vt)
    return out.reshape(batch * seq_len, SB_W)


def _mla_scores(q_ref, c_ref, qrows, j, s_out):
    c = c_ref[pl.ds(pl.multiple_of(j * BLK, BLK), BLK), :]
    for h in range(N_HEADS):
        s_out[h] = _dot_nt(c, q_ref[h, qrows, :])


def _mla_block(s_in, ct_ref, j, acc_ref, st_ref, vis):
    ct = ct_ref[0, j]
    for h in range(N_HEADS):
        s = s_in[h]
        if vis is not None:
            s = jnp.where(vis, s, -jnp.inf)
        m_old = st_ref[h, 0:1, :]
        m_new = jnp.maximum(m_old, jnp.max(s, axis=0, keepdims=True))
        p = jnp.exp2((s - m_new) * (MLA_SCALE * LOG2E))
        alpha = jnp.exp2((m_old - m_new) * (MLA_SCALE * LOG2E))
        st_ref[h, 1:2, :] = alpha * st_ref[h, 1:2, :] + jnp.sum(p, axis=0, keepdims=True)
        st_ref[h, 0:1, :] = m_new
        acc_ref[h] = alpha * acc_ref[h] + _dot(ct, p.astype(BF16))


def _mla_kernel(q_ref, c_ref, ct_ref, o_ref, acc_ref, st_ref, s0_ref, s1_ref, *, nblk):
    kidx = lax.broadcasted_iota(jnp.int32, (BLK, BLK), 0)
    qidx = lax.broadcasted_iota(jnp.int32, (BLK, BLK), 1)
    vis = (kidx // CHUNK) <= (qidx // CHUNK)

    def q_block(i, carry):
        acc_ref[...] = jnp.zeros_like(acc_ref)
        for h in range(N_HEADS):
            st_ref[h, 0:1, :] = jnp.full((1, BLK), -jnp.inf, F32)
            st_ref[h, 1:2, :] = jnp.zeros((1, BLK), F32)
        rows = pl.ds(pl.multiple_of(i * BLK, BLK), BLK)
        def step(j, j_next, s_cur, s_next, mask):
            _mla_scores(q_ref, c_ref, rows, j_next, s_next)
            _mla_block(s_cur, ct_ref, j, acc_ref, st_ref, mask)

        nxt = lambda j: jnp.minimum(j + 1, nblk - 1)
        _mla_scores(q_ref, c_ref, rows, i, s0_ref)
        step(i, 0, s0_ref, s1_ref, vis)

        def k_pair(m, c):
            j = 2 * m
            step(j, nxt(j), s1_ref, s0_ref, None)
            step(j + 1, nxt(j + 1), s0_ref, s1_ref, None)
            return c

        lax.fori_loop(0, i // 2, k_pair, 0)

        @pl.when(i % 2 == 1)
        def _():
            step(i - 1, nxt(i - 1), s1_ref, s0_ref, None)

        for h in range(N_HEADS):
            o = acc_ref[h] * (1.0 / st_ref[h, 1:2, :])
            o_ref[rows, h * KV_LORA:(h + 1) * KV_LORA] = o.T.astype(BF16)
        return carry

    lax.fori_loop(0, nblk, q_block, 0)


def _mla_prompt(qm, cm, ct, batch, seq_len):
    nblk = seq_len // BLK
    return pl.pallas_call(
        functools.partial(_mla_kernel, nblk=nblk),
        grid=(batch,),
        in_specs=[pl.BlockSpec((N_HEADS, seq_len, QM_W), lambda b: (0, b, 0)),
                  pl.BlockSpec((seq_len, QM_W), lambda b: (b, 0)),
                  pl.BlockSpec((1, nblk, KV_LORA, BLK), lambda b: (b, 0, 0, 0))],
        out_specs=pl.BlockSpec((seq_len, N_HEADS * KV_LORA), lambda b: (b, 0)),
        out_shape=jax.ShapeDtypeStruct((batch * seq_len, N_HEADS * KV_LORA), BF16),
        scratch_shapes=[pltpu.VMEM((N_HEADS, KV_LORA, BLK), F32), pltpu.VMEM((N_HEADS, SUBLANES, BLK), F32),
                        pltpu.VMEM((N_HEADS, BLK, BLK), F32), pltpu.VMEM((N_HEADS, BLK, BLK), F32)],
        compiler_params=_params("parallel"), name="mla_prompt",
    )(qm, cm, ct)


def _sbdec_blocks(score, av, nblocks, t, acc_ref, car_ref, wn, vis):
    heads = range(N_HEADS)
    zs = [jnp.concatenate([score(b, h) for h in heads], axis=0) * LOG2E
          for b in range(nblocks)]
    mms = []
    for b in range(nblocks):
        sp = _softplus2(zs[b])
        if vis is not None:
            sp = jnp.where(vis, sp, 0.0)
        hi, lo = _split_bf16(sp)
        mm2 = _dot(jnp.concatenate([hi, lo], axis=0), wn)
        mms.append(mm2[:N_HEADS * t] + mm2[N_HEADS * t:])
    car = car_ref[:, 0:1]
    for b in range(nblocks):
        a = jnp.exp2(zs[b] + mms[b] + car)
        if vis is not None:
            a = jnp.where(vis, a, 0.0)
        a = a.astype(BF16)
        for h in heads:
            acc_ref[h] += av(b, h, a[h * t:(h + 1) * t, :])
        car = car + mms[b][:, 0:1]
    car_ref[...] = jnp.broadcast_to(car, car_ref.shape)


def _sbdec_kernel(q_ref, kn_ref, vn_ref, ck_ref, cv_ref, o_ref, qh_ref, acc_ref, car_ref, *, t_new):
    j = pl.program_id(1)
    hcols = lambda h: slice(h * SB_HEAD_DIM, (h + 1) * SB_HEAD_DIM)

    @pl.when(j == 0)
    def _():
        acc_ref[...] = jnp.zeros_like(acc_ref)
        car_ref[...] = jnp.zeros_like(car_ref)
        for h in range(N_HEADS):
            qh_ref[h] = q_ref[:, hcols(h)]
        kp = lax.broadcasted_iota(jnp.int32, (t_new, t_new), 0)
        kc = lax.broadcasted_iota(jnp.int32, (t_new, t_new), 1)
        wn_new = jnp.where(kp >= kc, -1.0, 0.0).astype(BF16)
        qi = lax.rem(lax.broadcasted_iota(jnp.int32, (N_HEADS * t_new, t_new), 0), t_new)
        ki = lax.broadcasted_iota(jnp.int32, (N_HEADS * t_new, t_new), 1)
        _sbdec_blocks(lambda b, h: _dot_nt(qh_ref[h], kn_ref[:, hcols(h)].astype(BF16)),
                      lambda b, h, a: _dot(a, vn_ref[:, hcols(h)].astype(BF16)),
                      1, t_new, acc_ref, car_ref, wn_new, ki < qi)

    kp = lax.broadcasted_iota(jnp.int32, (BLK, BLK), 0)
    kc = lax.broadcasted_iota(jnp.int32, (BLK, BLK), 1)
    wn = jnp.where(kp >= kc, -1.0, 0.0).astype(BF16)
    nsub = DEC_KEYS // BLK
    keys = lambda b: slice((nsub - 1 - b) * BLK, (nsub - b) * BLK)
    _sbdec_blocks(lambda b, h: _dot(qh_ref[h], ck_ref[0, 0, h, :, keys(b)].astype(BF16)),
                  lambda b, h, a: _dot_nt(a, cv_ref[0, 0, h, :, keys(b)].astype(BF16)),
                  nsub, t_new, acc_ref, car_ref, wn, None)

    @pl.when(j == pl.num_programs(1) - 1)
    def _():
        for h in range(N_HEADS):
            o_ref[:, hcols(h)] = acc_ref[h].astype(BF16)


def _sb_sample(q, k_new, v_new, cache_k, cache_v, layer, batch, t_new):
    past = cache_k.shape[4]
    assert past % DEC_KEYS == 0
    nkb = past // DEC_KEYS
    tok = lambda: pl.BlockSpec((t_new, SB_W), lambda b, j: (b, 0))
    cache = lambda: pl.BlockSpec((1, 1, N_HEADS, SB_HEAD_DIM, DEC_KEYS), lambda b, j: (layer, b, 0, 0, nkb - 1 - j))
    return pl.pallas_call(
        functools.partial(_sbdec_kernel, t_new=t_new),
        grid=(batch, nkb),
        in_specs=[tok(), tok(), tok(), cache(), cache()],
        out_specs=tok(),
        out_shape=jax.ShapeDtypeStruct((batch * t_new, SB_W), BF16),
        scratch_shapes=[pltpu.VMEM((N_HEADS, t_new, SB_HEAD_DIM), BF16),
                        pltpu.VMEM((N_HEADS, t_new, SB_HEAD_DIM), F32),
                        pltpu.VMEM((N_HEADS * t_new, LANES), F32)],
        compiler_params=_params("parallel", "arbitrary"), name="sb_sample",
    )(q, k_new, v_new, cache_k, cache_v)


def _mladec_kernel(q_ref, cn_ref, ck_ref, kr_ref, o_ref, *, t_new, past, new_visible):
    q = q_ref[...].reshape(N_HEADS * t_new, QM_W)
    q_lat = q[:, :KV_LORA]
    q_rope = q[:, KV_LORA:KV_LORA + MLA_ROPE]
    c2 = MLA_SCALE * LOG2E

    cn = cn_ref[...]
    s = _dot_nt(q, cn)
    if not new_visible:
        qi = past + lax.broadcasted_iota(jnp.int32, s.shape, 0) % t_new
        ki = past + lax.broadcasted_iota(jnp.int32, s.shape, 1)
        s = jnp.where((ki // CHUNK) <= (qi // CHUNK), s, -jnp.inf)
    m = jnp.max(s, axis=1, keepdims=True)
    p = jnp.exp2((s - m) * c2)
    l = jnp.sum(p, axis=1, keepdims=True)
    acc = _dot(p.astype(BF16), cn[:, :KV_LORA])

    for c in range(past // MLA_DEC_KEYS):
        rows = slice(c * MLA_DEC_KEYS, (c + 1) * MLA_DEC_KEYS)
        ck = ck_ref[0, 0, rows, :].astype(BF16)
        kr_t = kr_ref[0, 0, :, rows].astype(BF16)
        s = _dot_nt(q_lat, ck) + _dot(q_rope, kr_t)
        m_new = jnp.maximum(m, jnp.max(s, axis=1, keepdims=True))
        p = jnp.exp2((s - m_new) * c2)
        alpha = jnp.exp2((m - m_new) * c2)
        l = alpha * l + jnp.sum(p, axis=1, keepdims=True)
        acc = alpha * acc + _dot(p.astype(BF16), ck)
        m = m_new
    o = (acc * (1.0 / l)).astype(BF16)
    for h in range(N_HEADS):
        o_ref[:, h * KV_LORA:(h + 1) * KV_LORA] = o[h * t_new:(h + 1) * t_new, :]


def _mla_sample(qm, cm, cache_ckv, cache_kr, layer, batch, t_new):
    past = cache_ckv.shape[2]
    assert past % MLA_DEC_KEYS == 0 and past % CHUNK == 0
    q_pos = past + np.arange(t_new)
    new_visible = bool(((q_pos[None, :] // CHUNK) <= (q_pos[:, None] // CHUNK)).all())
    return pl.pallas_call(
        functools.partial(_mladec_kernel, t_new=t_new, past=past, new_visible=new_visible),
        grid=(batch,),
        in_specs=[pl.BlockSpec((N_HEADS, t_new, QM_W), lambda b: (0, b, 0)),
                  pl.BlockSpec((t_new, QM_W), lambda b: (b, 0)),
                  pl.BlockSpec((1, 1, past, KV_LORA), lambda b: (layer, b, 0, 0)),
                  pl.BlockSpec((1, 1, MLA_ROPE, past), lambda b: (layer, b, 0, 0))],
        out_specs=pl.BlockSpec((t_new, N_HEADS * KV_LORA), lambda b: (b, 0)),
        out_shape=jax.ShapeDtypeStruct((batch * t_new, N_HEADS * KV_LORA), BF16),
        compiler_params=_params("parallel"), name="mla_sample",
    )(qm, cm, cache_ckv, cache_kr)


def _mix_kernel(x_ref, gpre_ref, osb_ref, olat_ref, wg_ref, wpsb_ref, wuv_ref, wpmla_ref, wout_ref,
                gpost_ref, o_ref):
    x = x_ref[...]
    hb = _rmsnorm(x, gpre_ref[...]).astype(BF16)
    gates = _dot(hb, wg_ref[...])
    y_a = _dot(osb_ref[...], wpsb_ref[...])
    o_mla = _dot(olat_ref[...], wuv_ref[...]).astype(BF16)
    y_b = _dot(o_mla, wpmla_ref[...])
    mixin = _sigmoid(gates[:, :D_MODEL]) * y_a + _sigmoid(gates[:, D_MODEL:]) * y_b
    mix = _dot(mixin.astype(BF16), wout_ref[...])
    o_ref[...] = x + _rmsnorm(mix, gpost_ref[...])


def _mix(x2d, o_sb, o_lat, lw):
    n = x2d.shape[0]
    tm = TOK_TILE
    tok = lambda w: pl.BlockSpec((tm, w), lambda i: (i, 0))
    return pl.pallas_call(
        _mix_kernel, grid=(n // tm,),
        in_specs=[tok(D_MODEL), _const_spec((1, D_MODEL)), tok(SB_W), tok(N_HEADS * KV_LORA),
                  _const_spec(lw["w_gates"].shape), _const_spec(lw["w_proj_sb"].shape),
                  _const_spec(lw["w_uv_bd"].shape), _const_spec(lw["w_proj_mla"].shape),
                  _const_spec(lw["w_out"].shape), _const_spec((1, D_MODEL))],
        out_specs=tok(D_MODEL), out_shape=jax.ShapeDtypeStruct((n, D_MODEL), F32),
        compiler_params=_params("parallel"), name="mix",
    )(x2d, lw["g_pre_mix"], o_sb, o_lat, lw["w_gates"], lw["w_proj_sb"], lw["w_uv_bd"], lw["w_proj_mla"],
      lw["w_out"], lw["g_post_mix"])


def _ffn_kernel(x_ref, gpre_ref, wup_ref, cw_ref, past_ref, wdown_ref, gpost_ref, wpg_ref, p_ref, wpp_ref,
                o_ref, conv_ref, *u_scr, nseq, rows):
    t = pl.program_id(1)
    span = SUBLANES + rows
    lead = SUBLANES - (CONV_W - 1)

    @pl.when(t == 0)
    def _():
        for c in range(2 * N_FF_CHUNKS):
            for s in range(nseq):
                u_scr[c][s * span + lead:s * span + SUBLANES, :] = past_ref[s, c]

    x = x_ref[...]
    hb = _rmsnorm(x, gpre_ref[...]).astype(BF16)

    def up(c):
        for cc in (c, N_FF_CHUNKS + c):
            u = _dot(hb, wup_ref[cc])
            for s in range(nseq):
                u_scr[cc][s * span + SUBLANES:(s + 1) * span, :] = u[s * rows:(s + 1) * rows, :]

    def conv(c):
        cw = cw_ref[c]
        u = u_scr[c]
        out = []
        for s in range(nseq):
            uc = cw[3:4, :]
            for i in range(CONV_W):
                uc = uc + u[s * span + lead + i:s * span + lead + i + rows, :] * cw[i:i + 1, :]
            conv_ref[s, c] = u[(s + 1) * span - (CONV_W - 1):(s + 1) * span, :]
            u[s * span:s * span + SUBLANES, :] = u[s * span + rows:(s + 1) * span, :]
            out.append(uc)
        return out[0] if nseq == 1 else jnp.concatenate(out, axis=0)

    up(0)
    f = None
    acts = []
    for c in range(N_FF_CHUNKS):
        if c + 1 < N_FF_CHUNKS:
            up(c + 1)
        acts.append((_gelu_tanh(conv(c)) * conv(N_FF_CHUNKS + c)).astype(BF16))
        if len(acts) == FF_DOWN_GROUP or c + 1 == N_FF_CHUNKS:
            k1 = (c + 1) * FF_CHUNK
            k0 = k1 - len(acts) * FF_CHUNK
            part = _dot(jnp.concatenate(acts, axis=1), wdown_ref[k0:k1, :])
            f = part if f is None else f + part
            acts = []
    x2 = x + _rmsnorm(f, gpost_ref[...])
    ple = _sigmoid(_dot(x2.astype(BF16), wpg_ref[...])) * _dot(p_ref[0].astype(BF16), wpp_ref[...])
    o_ref[...] = x2 + ple


def _ffn(x2d, p3d, layer, conv_past, batch, seq_len, lw):
    tm = TOK_TILE
    rows = min(tm, seq_len)
    nseq = tm // rows
    nt = seq_len // rows
    assert batch % nseq == 0 and (nseq == 1 or nt == 1)
    nch = 2 * N_FF_CHUNKS
    tok = lambda w: pl.BlockSpec((tm, w), lambda b, t: (b * nt + t, 0))
    ple = pl.BlockSpec((1, tm, D_PLE), lambda b, t: (layer, b * nt + t, 0))
    state = pl.BlockSpec((nseq, nch, CONV_W - 1, FF_CHUNK), lambda b, t: (b, 0, 0, 0))
    return pl.pallas_call(
        functools.partial(_ffn_kernel, nseq=nseq, rows=rows),
        grid=(batch // nseq, nt),
        in_specs=[tok(D_MODEL), _const_spec((1, D_MODEL)), _const_spec(lw["w_up"].shape),
                  _const_spec(lw["conv"].shape), state, _const_spec(lw["w_down"].shape),
                  _const_spec((1, D_MODEL)), _const_spec(lw["w_ple_gate"].shape), ple,
                  _const_spec(lw["w_ple_proj"].shape)],
        out_specs=[tok(D_MODEL), state],
        out_shape=[jax.ShapeDtypeStruct(x2d.shape, F32), jax.ShapeDtypeStruct(conv_past.shape, F32)],
        scratch_shapes=[pltpu.VMEM((nseq * (SUBLANES + rows), FF_CHUNK), F32) for _ in range(nch)],
        compiler_params=_params("arbitrary", "arbitrary"), name="ffn",
    )(x2d, lw["g_pre_ffn"], lw["w_up"], lw["conv"], conv_past, lw["w_down"], lw["g_post_ffn"],
      lw["w_ple_gate"], p3d, lw["w_ple_proj"])


def _prep_layer(w_in, ckv_gain, w_uk, w_uv, w_proj_sb, w_proj_mla, w_out, g_pre_mix, g_post_mix, g_pre_ffn,
                g_post_ffn, w_up, conv_w, conv_b, w_down, w_ple_gate, w_ple_proj):
    o_qm = 3 * SB_W
    o_ckv = o_qm + N_HEADS * (MLA_NOPE + MLA_ROPE)
    o_kr = o_ckv + KV_LORA
    o_g = o_kr + MLA_ROPE
    wqm = w_in[:, o_qm:o_ckv].reshape(D_MODEL, N_HEADS, MLA_NOPE + MLA_ROPE)
    w_qn = wqm[:, :, :MLA_NOPE].reshape(D_MODEL, N_HEADS * MLA_NOPE)
    w_qr = jnp.concatenate([wqm[:, :, MLA_NOPE:MLA_NOPE + ROPE_HALF].reshape(D_MODEL, LANES),
                            wqm[:, :, MLA_NOPE + ROPE_HALF:].reshape(D_MODEL, LANES)], axis=1)
    w_kr = w_in[:, o_kr:o_g]
    w_kx = jnp.zeros((D_MODEL, 2 * LANES), F32)
    w_kx = w_kx.at[:, :ROPE_HALF].set(w_kr[:, :ROPE_HALF]).at[:, LANES:LANES + ROPE_HALF].set(w_kr[:, ROPE_HALF:])
    wa = jnp.zeros((N_HEADS, LANES, QM_W), F32)
    wb = np.zeros((N_HEADS, 2 * LANES, QM_W), np.float32)
    for h in range(N_HEADS):
        r0 = (h % 2) * MLA_NOPE
        wa = wa.at[h, r0:r0 + MLA_NOPE, :KV_LORA].set(w_uk[:, h, :].T)
        for i in range(ROPE_HALF):
            wb[h, h * ROPE_HALF + i, KV_LORA + i] = 1.0
            wb[h, LANES + h * ROPE_HALF + i, KV_LORA + ROPE_HALF + i] = 1.0
    w_uv_bd = jnp.zeros((N_HEADS * KV_LORA, N_HEADS * MLA_V), F32)
    for h in range(N_HEADS):
        w_uv_bd = w_uv_bd.at[h * KV_LORA:(h + 1) * KV_LORA, h * MLA_V:(h + 1) * MLA_V].set(w_uv[:, h, :])
    nch = 2 * N_FF_CHUNKS
    conv = jnp.concatenate([conv_w, conv_b[None, :], jnp.zeros((SUBLANES - CONV_W - 1, 2 * D_FF), F32)], axis=0)
    row = lambda g: g.reshape(1, -1)
    return dict(
        w_qkv=w_in[:, :o_qm].astype(BF16), w_kvt=w_in[:, SB_W:o_qm].T.astype(BF16),
        w_qn=w_qn.astype(BF16), w_qr=w_qr.astype(BF16),
        w_ckv=w_in[:, o_ckv:o_kr].astype(BF16), w_kx=w_kx.astype(BF16), ckv_gain=row(ckv_gain),
        wa=wa.astype(BF16), wb=jnp.asarray(wb, BF16), eye=jnp.eye(LANES, dtype=BF16),
        w_gates=w_in[:, o_g:].astype(BF16), w_proj_sb=w_proj_sb.astype(BF16), w_uv_bd=w_uv_bd.astype(BF16),
        w_proj_mla=w_proj_mla.astype(BF16), w_out=w_out.astype(BF16),
        g_pre_mix=row(g_pre_mix), g_post_mix=row(g_post_mix), g_pre_ffn=row(g_pre_ffn), g_post_ffn=row(g_post_ffn),
        w_up=w_up.reshape(D_MODEL, nch, FF_CHUNK).transpose(1, 0, 2).astype(BF16),
        conv=conv.reshape(SUBLANES, nch, FF_CHUNK).transpose(1, 0, 2),
        w_down=w_down.astype(BF16),
        w_ple_gate=w_ple_gate.astype(BF16), w_ple_proj=w_ple_proj.astype(BF16))


def _rope_tables(pos, rows):
    freqs = jnp.power(ROPE_THETA, -jnp.arange(ROPE_HALF, dtype=F32) / ROPE_HALF)
    ang = pos.astype(F32)[:, None] * freqs[None, :]
    reps = max(1, rows // pos.shape[0])
    tile = lambda a: jnp.tile(a, (reps, LANES // ROPE_HALF))
    return tile(jnp.cos(ang)), tile(jnp.sin(ang))


def _conv_to_chunks(state):
    b = state.shape[0]
    return state.reshape(b, CONV_W - 1, 2 * N_FF_CHUNKS, FF_CHUNK).transpose(0, 2, 1, 3)


def _conv_from_chunks(state):
    b = state.shape[0]
    return state.transpose(0, 2, 1, 3).reshape(b, CONV_W - 1, 2 * D_FF)


def kernel(x_prompt, x_sample, p_prompt, p_sample, cache_sb_k, cache_sb_v, cache_mla_ckv, cache_mla_krope,
           state_ffn_conv, w_in, ckv_gain, w_uk, w_uv, w_proj_sb, w_proj_mla, w_out, g_pre_mix, g_post_mix,
           g_pre_ffn, g_post_ffn, w_up, conv_w, conv_b, w_down, w_ple_gate, w_ple_proj):
    bp, tp, _ = x_prompt.shape
    bs, ts, _ = x_sample.shape
    depth = w_in.shape[0]
    past = cache_sb_k.shape[2]
    assert tp % TOK_TILE == 0 and (bs * ts) % TOK_TILE == 0 and TOK_TILE % ts == 0
    tabs_p = _rope_tables(jnp.arange(tp, dtype=jnp.int32), TOK_TILE)
    tabs_s = _rope_tables(past + jnp.arange(ts, dtype=jnp.int32), TOK_TILE)
    xp = x_prompt.reshape(bp * tp, D_MODEL)
    xs = x_sample.reshape(bs * ts, D_MODEL)
    pp = p_prompt.reshape(depth, bp * tp, D_PLE)
    ps = p_sample.reshape(depth, bs * ts, D_PLE)
    ck_t = cache_sb_k.transpose(0, 1, 3, 4, 2)
    cv_t = cache_sb_v.transpose(0, 1, 3, 4, 2)
    kr_t = cache_mla_krope.transpose(0, 1, 3, 2)
    zero_conv = jnp.zeros((bp, 2 * N_FF_CHUNKS, CONV_W - 1, FF_CHUNK), F32)
    st_p, st_s = [], []
    kv_p = None
    for i in range(depth):
        lw = _prep_layer(w_in[i], ckv_gain[i], w_uk[i], w_uv[i], w_proj_sb[i], w_proj_mla[i], w_out[i],
                         g_pre_mix[i], g_post_mix[i], g_pre_ffn[i], g_post_ffn[i], w_up[i], conv_w[i],
                         conv_b[i], w_down[i], w_ple_gate[i], w_ple_proj[i])
        q, k_bf, kt_all, vt_all, vt, qm, cm, ct, c_st, r_st = _proj(xp, tp, tabs_p, lw, kv_states=(i, depth, kv_p))
        kv_p = (kt_all, vt_all)
        o_sb = _sb_prompt(q, k_bf, vt, bp, tp)
        o_lat = _mla_prompt(qm, cm, ct, bp, tp)
        x1 = _mix(xp, o_sb, o_lat, lw)
        xp, conv_p = _ffn(x1, pp, i, zero_conv, bp, tp, lw)
        st_p.append((c_st.reshape(bp, tp, KV_LORA), r_st.reshape(bp, tp, MLA_ROPE), _conv_from_chunks(conv_p)))
        q, k_st, v_st, qm, cm, c_st, r_st = _proj(xs, ts, tabs_s, lw)
        o_sb = _sb_sample(q, k_st, v_st, ck_t, cv_t, i, bs, ts)
        o_lat = _mla_sample(qm, cm, cache_mla_ckv, kr_t, i, bs, ts)
        x1 = _mix(xs, o_sb, o_lat, lw)
        xs, conv_s = _ffn(x1, ps, i, _conv_to_chunks(state_ffn_conv[i]), bs, ts, lw)
        st_s.append((k_st.reshape(bs, ts, N_HEADS, SB_HEAD_DIM), v_st.reshape(bs, ts, N_HEADS, SB_HEAD_DIM),
                     c_st.reshape(bs, ts, KV_LORA), r_st.reshape(bs, ts, MLA_ROPE), _conv_from_chunks(conv_s)))
    stack = lambda sts, k: jnp.stack([s[k] for s in sts])
    heads_last = lambda a: a.reshape(depth, bp, N_HEADS, SB_HEAD_DIM, tp).transpose(0, 1, 4, 2, 3)
    return ((xp.reshape(bp, tp, D_MODEL), xs.reshape(bs, ts, D_MODEL), heads_last(kv_p[0]), heads_last(kv_p[1]))
            + tuple(stack(st_p, k) for k in range(3)) + tuple(stack(st_s, k) for k in range(5)))
```

```python
import functools
import math

import jax
import jax.numpy as jnp
import numpy as np
from jax import lax
from jax.experimental import pallas as pl
from jax.experimental.pallas import tpu as pltpu

D_MODEL = 1024
N_HEADS = 8
SB_HEAD_DIM = 64
SB_W = N_HEADS * SB_HEAD_DIM
MLA_NOPE = 64
MLA_ROPE = 32
ROPE_HALF = MLA_ROPE // 2
KV_LORA = 128
MLA_V = 64
D_FF = 2816
CONV_W = 3
CHUNK = 64
D_PLE = 256
ROPE_THETA = 10000.0
EPS = 1e-6
SB_SCALE = SB_HEAD_DIM ** -0.5
MLA_SCALE = (MLA_NOPE + MLA_ROPE) ** -0.5
LOG2E = math.log2(math.e)

LANES = 128
SUBLANES = 8
MXU_DIM = 256
VMEM_LIMIT_BYTES = 56 * 1024 * 1024

BLK = MXU_DIM
TOK_TILE = 2 * MXU_DIM
FF_CHUNK = MXU_DIM
N_FF_CHUNKS = D_FF // FF_CHUNK
FF_DOWN_GROUP = 4
QM_W = 2 * LANES
DEC_KEYS = 1024
MLA_DEC_KEYS = 512
HEAD_GROUP = MXU_DIM // SB_HEAD_DIM
N_GROUPS = N_HEADS // HEAD_GROUP

BF16 = jnp.bfloat16
F32 = jnp.float32


def _dot(a, b):
    return jnp.dot(a, b, preferred_element_type=F32)


def _dot_nt(a, b):
    return lax.dot_general(a, b, (((1,), (1,)), ((), ())), preferred_element_type=F32)


def _rmsnorm(x, g):
    ms = jnp.mean(x * x, axis=-1, keepdims=True)
    return x * lax.rsqrt(ms + EPS) * g


def _sigmoid(x):
    return 1.0 / (1.0 + jnp.exp(-x))


def _softplus2(z2):
    return jnp.maximum(z2, 0.0) + jnp.log2(1.0 + jnp.exp2(-jnp.abs(z2)))


def _gelu_tanh(x):
    c = math.sqrt(2.0 / math.pi)
    return 0.5 * x * (1.0 + jnp.tanh(c * (x + 0.044715 * (x * x * x))))


def _params(*sem):
    return pltpu.CompilerParams(dimension_semantics=sem, vmem_limit_bytes=VMEM_LIMIT_BYTES)


def _const_spec(shape):
    return pl.BlockSpec(shape, lambda *_: (0,) * len(shape), pipeline_mode=pl.Buffered(1))


def _proj_kernel(x_ref, g_ref, cos_ref, sin_ref, wqkv_ref, wkvt_ref, wqn_ref, wqr_ref, wckv_ref, wkx_ref,
                 gain_ref, wa_ref, wb_ref, eye_ref, *refs, transposed, n_alias):
    out_refs = refs[n_alias:]
    if transposed:
        (q_ref, kbf_ref, ktst_ref, vtst_ref, vt_ref, qm_ref, cm_ref, ct_ref, cst_ref, rst_ref) = out_refs
    else:
        (q_ref, kst_ref, vst_ref, qm_ref, cm_ref, cst_ref, rst_ref) = out_refs
    hb = _rmsnorm(x_ref[...], g_ref[...]).astype(BF16)
    nsub = x_ref.shape[0] // BLK

    if transposed:
        qk = _dot(hb, wqkv_ref[:, :2 * SB_W])
        q_ref[...] = (qk[:, :SB_W] * SB_SCALE).astype(BF16)
        kb = qk[:, SB_W:].astype(BF16)
        even = lax.broadcasted_iota(jnp.int32, kb.shape, 1) % LANES < SB_HEAD_DIM
        kbf_ref[0] = jnp.where(even, kb, jnp.zeros_like(kb))
        kbf_ref[1] = jnp.where(even, jnp.zeros_like(kb), kb)
        kt = _dot_nt(wkvt_ref[:SB_W, :], hb)
        vt = _dot_nt(wkvt_ref[SB_W:, :], hb)
        ktst_ref[0, 0] = kt
        vtst_ref[0, 0] = vt
        for s in range(nsub):
            vt_ref[0, s] = vt[:, s * BLK:(s + 1) * BLK].astype(BF16)
    else:
        qkv = _dot(hb, wqkv_ref[...])
        q_ref[...] = (qkv[:, :SB_W] * SB_SCALE).astype(BF16)
        kst_ref[...] = qkv[:, SB_W:2 * SB_W]
        vst_ref[...] = qkv[:, 2 * SB_W:]

    cos = cos_ref[...]
    sin = sin_ref[...]
    qn = _dot(hb, wqn_ref[...]).astype(BF16)
    qr = _dot(hb, wqr_ref[...])
    x1, x2 = qr[:, :LANES], qr[:, LANES:]
    r12 = jnp.concatenate([x1 * cos - x2 * sin, x1 * sin + x2 * cos], axis=1).astype(BF16)
    for h in range(N_HEADS):
        p = h // 2
        qm_ref[h] = (_dot(qn[:, p * LANES:(p + 1) * LANES], wa_ref[h]) + _dot(r12, wb_ref[h])).astype(BF16)

    ckv = _rmsnorm(_dot(hb, wckv_ref[...]), gain_ref[...])
    cst_ref[...] = ckv
    kx = _dot(hb, wkx_ref[...])
    k1, k2 = kx[:, :LANES], kx[:, LANES:]
    kr1 = k1 * cos - k2 * sin
    kr2 = k1 * sin + k2 * cos
    lane = lax.broadcasted_iota(jnp.int32, kr1.shape, 1)
    kr = jnp.where(lane < ROPE_HALF, kr1, pltpu.roll(kr2, ROPE_HALF, axis=1))
    rst_ref[...] = kr[:, :MLA_ROPE]
    ckv_b = ckv.astype(BF16)
    cm_ref[...] = jnp.concatenate([ckv_b, kr.astype(BF16)], axis=1)
    if transposed:
        for s in range(nsub):
            ct_ref[0, s] = _dot_nt(eye_ref[...], ckv_b[s * BLK:(s + 1) * BLK, :]).astype(BF16)


def _proj(x2d, seq_len, tabs, lw, *, kv_states=None):
    n = x2d.shape[0]
    tm = TOK_TILE
    nt = n // tm
    cos, sin = tabs
    tab_blocks = cos.shape[0] // tm
    transposed = kv_states is not None
    tok = lambda w: pl.BlockSpec((tm, w), lambda i: (i, 0))
    tab = pl.BlockSpec((tm, LANES), lambda i: (i % tab_blocks, 0))
    in_specs = [tok(D_MODEL), _const_spec((1, D_MODEL)), tab, tab,
                _const_spec(lw["w_qkv"].shape), _const_spec(lw["w_kvt"].shape),
                _const_spec(lw["w_qn"].shape), _const_spec(lw["w_qr"].shape),
                _const_spec(lw["w_ckv"].shape), _const_spec(lw["w_kx"].shape), _const_spec((1, KV_LORA)),
                _const_spec(lw["wa"].shape), _const_spec(lw["wb"].shape), _const_spec((LANES, LANES))]
    args = [x2d, lw["g_pre_mix"], cos, sin, lw["w_qkv"], lw["w_kvt"], lw["w_qn"], lw["w_qr"], lw["w_ckv"],
            lw["w_kx"], lw["ckv_gain"], lw["wa"], lw["wb"], lw["eye"]]
    aliases = {}
    if transposed:
        layer, depth, prev = kv_states
        assert seq_len % tm == 0
        batch = n // seq_len
        nb = seq_len // tm
        sub = tm // BLK
        state = jax.ShapeDtypeStruct((depth, batch, SB_W, seq_len), F32)
        state_spec = pl.BlockSpec((1, 1, SB_W, tm), lambda i: (layer, i // nb, 0, i % nb))
        blocks = lambda w: (jax.ShapeDtypeStruct((batch, nb * sub, w, BLK), BF16),
                            pl.BlockSpec((1, sub, w, BLK), lambda i: (i // nb, i % nb, 0, 0)))
        outs = [(jax.ShapeDtypeStruct((n, SB_W), BF16), tok(SB_W)),
                (jax.ShapeDtypeStruct((2, n, SB_W), BF16),
                 pl.BlockSpec((2, tm, SB_W), lambda i: (0, i, 0))),
                (state, state_spec), (state, state_spec), blocks(SB_W)]
        if prev is not None:
            aliases = {len(args): 2, len(args) + 1: 3}
            args += list(prev)
            in_specs += [pl.BlockSpec(memory_space=pl.ANY)] * 2
    else:
        outs = [(jax.ShapeDtypeStruct((n, SB_W), BF16), tok(SB_W)),
                (jax.ShapeDtypeStruct((n, SB_W), F32), tok(SB_W)), (jax.ShapeDtypeStruct((n, SB_W), F32), tok(SB_W))]
    outs += [(jax.ShapeDtypeStruct((N_HEADS, n, QM_W), BF16), pl.BlockSpec((N_HEADS, tm, QM_W), lambda i: (0, i, 0))),
             (jax.ShapeDtypeStruct((n, QM_W), BF16), tok(QM_W))]
    if transposed:
        outs.append(blocks(KV_LORA))
    outs += [(jax.ShapeDtypeStruct((n, KV_LORA), F32), tok(KV_LORA)),
             (jax.ShapeDtypeStruct((n, MLA_ROPE), F32), tok(MLA_ROPE))]
    return pl.pallas_call(
        functools.partial(_proj_kernel, transposed=transposed, n_alias=len(aliases)),
        grid=(nt,), in_specs=in_specs, out_specs=[o[1] for o in outs], out_shape=[o[0] for o in outs],
        input_output_aliases=aliases,
        compiler_params=_params("parallel"), name="proj_t" if transposed else "proj",
    )(*args)


def _sb_scores(q_ref, k_ref, qrows, j, z_out):
    krows = pl.ds(pl.multiple_of(j * BLK, BLK), BLK)
    for h in range(N_HEADS):
        p, hh = h // 2, h % 2
        cols = slice(p * LANES, (p + 1) * LANES)
        kh = k_ref[hh, 0, krows, cols]
        z_out[h] = _dot_nt(kh, q_ref[0, qrows, cols]) * LOG2E


def _sb_block(z_in, vt_ref, j, acc_ref, car_ref, umat, vis):
    heads = range(N_HEADS)
    mms = []
    for h in heads:
        sp = _softplus2(z_in[h])
        if vis is not None:
            sp = jnp.where(vis, sp, 0.0)
        mms.append(_dot(umat, sp.astype(BF16)))
    for h in heads:
        car = car_ref[h, 0:1, :]
        a = jnp.exp2(z_in[h] + mms[h] + car)
        if vis is not None:
            a = jnp.where(vis, a, 0.0)
        rows = slice(h * SB_HEAD_DIM, (h + 1) * SB_HEAD_DIM)
        acc_ref[rows, :] += _dot(vt_ref[0, j, rows, :], a.astype(BF16))
        car_ref[h, 0:1, :] = car + mms[h][0:1, :]


def _sb_kernel(q_ref, k_ref, vt_ref, o_ref, acc_ref, car_ref, z0_ref, z1_ref, *, nblk):
    kidx = lax.broadcasted_iota(jnp.int32, (BLK, BLK), 0)
    qidx = lax.broadcasted_iota(jnp.int32, (BLK, BLK), 1)
    umat = jnp.where(qidx >= kidx, -1.0, 0.0).astype(BF16)
    vis = kidx < qidx

    def q_block(i, carry):
        acc_ref[...] = jnp.zeros_like(acc_ref)
        car_ref[...] = jnp.zeros_like(car_ref)
        qrows = pl.ds(pl.multiple_of(i * BLK, BLK), BLK)

        def step(j, z_cur, z_next, mask):
            _sb_scores(q_ref, k_ref, qrows, jnp.maximum(j - 1, 0), z_next)
            _sb_block(z_cur, vt_ref, j, acc_ref, car_ref, umat, mask)

        _sb_scores(q_ref, k_ref, qrows, i, z0_ref)
        step(i, z0_ref, z1_ref, vis)

        def k_pair(m, c):
            j = i - 1 - 2 * m
            step(j, z1_ref, z0_ref, None)
            step(j - 1, z0_ref, z1_ref, None)
            return c

        lax.fori_loop(0, i // 2, k_pair, 0)

        @pl.when(i % 2 == 1)
        def _():
            step(0, z1_ref, z0_ref, None)

        o_ref[0, qrows, :] = acc_ref[...].T.astype(BF16)
        return carry

    lax.fori_loop(0, nblk, q_block, 0)


def _sb_prompt(q, kbf, vt, batch, seq_len):
    nblk = seq_len // BLK
    q3 = q.reshape(batch, seq_len, SB_W)
    k4 = kbf.reshape(2, batch, seq_len, SB_W)
    row = pl.BlockSpec((1, seq_len, SB_W), lambda b: (b, 0, 0))
    out = pl.pallas_call(
        functools.partial(_sb_kernel, nblk=nblk),
        grid=(batch,),
        in_specs=[row, pl.BlockSpec((2, 1, seq_len, SB_W), lambda b: (0, b, 0, 0)),
                  pl.BlockSpec((1, nblk, SB_W, BLK), lambda b: (b, 0, 0, 0))],
        out_specs=row,
        out_shape=jax.ShapeDtypeStruct((batch, seq_len, SB_W), BF16),
        scratch_shapes=[pltpu.VMEM((SB_W, BLK), F32), pltpu.VMEM((N_HEADS, SUBLANES, BLK), F32),
                        pltpu.VMEM((N_HEADS, BLK, BLK), F32), pltpu.VMEM((N_HEADS, BLK, BLK), F32)],
        compiler_params=_params("parallel"), name="sb_prompt",
    )(q3, k4, vt)
    return out.reshape(batch * seq_len, SB_W)


def _mla_scores(q_ref, c_ref, qrows, j, s_out):
    c = c_ref[pl.ds(pl.multiple_of(j * BLK, BLK), BLK), :]
    for h in range(N_HEADS):
        s_out[h] = _dot_nt(c, q_ref[h, qrows, :])


def _mla_block(s_in, ct_ref, j, acc_ref, st_ref, vis):
    ct = ct_ref[0, j]
    for h in range(N_HEADS):
        s = s_in[h]
        if vis is not None:
            s = jnp.where(vis, s, -jnp.inf)
        m_old = st_ref[h, 0:1, :]
        m_new = jnp.maximum(m_old, jnp.max(s, axis=0, keepdims=True))
        p = jnp.exp2((s - m_new) * (MLA_SCALE * LOG2E))
        alpha = jnp.exp2((m_old - m_new) * (MLA_SCALE * LOG2E))
        st_ref[h, 1:2, :] = alpha * st_ref[h, 1:2, :] + jnp.sum(p, axis=0, keepdims=True)
        st_ref[h, 0:1, :] = m_new
        acc_ref[h] = alpha * acc_ref[h] + _dot(ct, p.astype(BF16))


def _mla_kernel(q_ref, c_ref, ct_ref, o_ref, acc_ref, st_ref, s0_ref, s1_ref, *, nblk):
    kidx = lax.broadcasted_iota(jnp.int32, (BLK, BLK), 0)
    qidx = lax.broadcasted_iota(jnp.int32, (BLK, BLK), 1)
    vis = (kidx // CHUNK) <= (qidx // CHUNK)

    def q_block(i, carry):
        acc_ref[...] = jnp.zeros_like(acc_ref)
        for h in range(N_HEADS):
            st_ref[h, 0:1, :] = jnp.full((1, BLK), -jnp.inf, F32)
            st_ref[h, 1:2, :] = jnp.zeros((1, BLK), F32)
        rows = pl.ds(pl.multiple_of(i * BLK, BLK), BLK)
        def step(j, j_next, s_cur, s_next, mask):
            _mla_scores(q_ref, c_ref, rows, j_next, s_next)
            _mla_block(s_cur, ct_ref, j, acc_ref, st_ref, mask)

        nxt = lambda j: jnp.minimum(j + 1, nblk - 1)
        _mla_scores(q_ref, c_ref, rows, i, s0_ref)
        step(i, 0, s0_ref, s1_ref, vis)

        def k_pair(m, c):
            j = 2 * m
            step(j, nxt(j), s1_ref, s0_ref, None)
            step(j + 1, nxt(j + 1), s0_ref, s1_ref, None)
            return c

        lax.fori_loop(0, i // 2, k_pair, 0)

        @pl.when(i % 2 == 1)
        def _():
            step(i - 1, nxt(i - 1), s1_ref, s0_ref, None)

        for h in range(N_HEADS):
            o = acc_ref[h] * (1.0 / st_ref[h, 1:2, :])
            o_ref[rows, h * KV_LORA:(h + 1) * KV_LORA] = o.T.astype(BF16)
        return carry

    lax.fori_loop(0, nblk, q_block, 0)


def _mla_prompt(qm, cm, ct, batch, seq_len):
    nblk = seq_len // BLK
    return pl.pallas_call(
        functools.partial(_mla_kernel, nblk=nblk),
        grid=(batch,),
        in_specs=[pl.BlockSpec((N_HEADS, seq_len, QM_W), lambda b: (0, b, 0)),
                  pl.BlockSpec((seq_len, QM_W), lambda b: (b, 0)),
                  pl.BlockSpec((1, nblk, KV_LORA, BLK), lambda b: (b, 0, 0, 0))],
        out_specs=pl.BlockSpec((seq_len, N_HEADS * KV_LORA), lambda b: (b, 0)),
        out_shape=jax.ShapeDtypeStruct((batch * seq_len, N_HEADS * KV_LORA), BF16),
        scratch_shapes=[pltpu.VMEM((N_HEADS, KV_LORA, BLK), F32), pltpu.VMEM((N_HEADS, SUBLANES, BLK), F32),
                        pltpu.VMEM((N_HEADS, BLK, BLK), F32), pltpu.VMEM((N_HEADS, BLK, BLK), F32)],
        compiler_params=_params("parallel"), name="mla_prompt",
    )(qm, cm, ct)


def _sbdec_blocks(score, av, nblocks, t, acc_ref, car_ref, wn, vis):
    heads = range(N_HEADS)
    zs = [jnp.concatenate([score(b, h) for h in heads], axis=0) * LOG2E
          for b in range(nblocks)]
    mms = []
    for b in range(nblocks):
        sp = _softplus2(zs[b])
        if vis is not None:
            sp = jnp.where(vis, sp, 0.0)
        mms.append(_dot(sp.astype(BF16), wn))
    car = car_ref[:, 0:1]
    for b in range(nblocks):
        a = jnp.exp2(zs[b] + mms[b] + car)
        if vis is not None:
            a = jnp.where(vis, a, 0.0)
        a = a.astype(BF16)
        for h in heads:
            acc_ref[h] += av(b, h, a[h * t:(h + 1) * t, :])
        car = car + mms[b][:, 0:1]
    car_ref[...] = jnp.broadcast_to(car, car_ref.shape)


def _sbdec_kernel(q_ref, kn_ref, vn_ref, ck_ref, cv_ref, o_ref, qh_ref, acc_ref, car_ref, *, t_new):
    j = pl.program_id(1)
    hcols = lambda h: slice(h * SB_HEAD_DIM, (h + 1) * SB_HEAD_DIM)

    @pl.when(j == 0)
    def _():
        acc_ref[...] = jnp.zeros_like(acc_ref)
        car_ref[...] = jnp.zeros_like(car_ref)
        for h in range(N_HEADS):
            qh_ref[h] = q_ref[:, hcols(h)]
        kp = lax.broadcasted_iota(jnp.int32, (t_new, t_new), 0)
        kc = lax.broadcasted_iota(jnp.int32, (t_new, t_new), 1)
        wn_new = jnp.where(kp >= kc, -1.0, 0.0).astype(BF16)
        qi = lax.rem(lax.broadcasted_iota(jnp.int32, (N_HEADS * t_new, t_new), 0), t_new)
        ki = lax.broadcasted_iota(jnp.int32, (N_HEADS * t_new, t_new), 1)
        _sbdec_blocks(lambda b, h: _dot_nt(qh_ref[h], kn_ref[:, hcols(h)].astype(BF16)),
                      lambda b, h, a: _dot(a, vn_ref[:, hcols(h)].astype(BF16)),
                      1, t_new, acc_ref, car_ref, wn_new, ki < qi)

    kp = lax.broadcasted_iota(jnp.int32, (BLK, BLK), 0)
    kc = lax.broadcasted_iota(jnp.int32, (BLK, BLK), 1)
    wn = jnp.where(kp >= kc, -1.0, 0.0).astype(BF16)
    nsub = DEC_KEYS // BLK
    keys = lambda b: slice((nsub - 1 - b) * BLK, (nsub - b) * BLK)
    _sbdec_blocks(lambda b, h: _dot(qh_ref[h], ck_ref[0, 0, h, :, keys(b)].astype(BF16)),
                  lambda b, h, a: _dot_nt(a, cv_ref[0, 0, h, :, keys(b)].astype(BF16)),
                  nsub, t_new, acc_ref, car_ref, wn, None)

    @pl.when(j == pl.num_programs(1) - 1)
    def _():
        for h in range(N_HEADS):
            o_ref[:, hcols(h)] = acc_ref[h].astype(BF16)


def _sb_sample(q, k_new, v_new, cache_k, cache_v, layer, batch, t_new):
    past = cache_k.shape[4]
    assert past % DEC_KEYS == 0
    nkb = past // DEC_KEYS
    tok = lambda: pl.BlockSpec((t_new, SB_W), lambda b, j: (b, 0))
    cache = lambda: pl.BlockSpec((1, 1, N_HEADS, SB_HEAD_DIM, DEC_KEYS), lambda b, j: (layer, b, 0, 0, nkb - 1 - j))
    return pl.pallas_call(
        functools.partial(_sbdec_kernel, t_new=t_new),
        grid=(batch, nkb),
        in_specs=[tok(), tok(), tok(), cache(), cache()],
        out_specs=tok(),
        out_shape=jax.ShapeDtypeStruct((batch * t_new, SB_W), BF16),
        scratch_shapes=[pltpu.VMEM((N_HEADS, t_new, SB_HEAD_DIM), BF16),
                        pltpu.VMEM((N_HEADS, t_new, SB_HEAD_DIM), F32),
                        pltpu.VMEM((N_HEADS * t_new, LANES), F32)],
        compiler_params=_params("parallel", "arbitrary"), name="sb_sample",
    )(q, k_new, v_new, cache_k, cache_v)


def _mladec_kernel(q_ref, cn_ref, ck_ref, kr_ref, o_ref, *, t_new, past, new_visible):
    q = q_ref[...].reshape(N_HEADS * t_new, QM_W)
    q_lat = q[:, :KV_LORA]
    q_rope = q[:, KV_LORA:KV_LORA + MLA_ROPE]
    c2 = MLA_SCALE * LOG2E

    cn = cn_ref[...]
    s = _dot_nt(q, cn)
    if not new_visible:
        qi = past + lax.broadcasted_iota(jnp.int32, s.shape, 0) % t_new
        ki = past + lax.broadcasted_iota(jnp.int32, s.shape, 1)
        s = jnp.where((ki // CHUNK) <= (qi // CHUNK), s, -jnp.inf)
    m = jnp.max(s, axis=1, keepdims=True)
    p = jnp.exp2((s - m) * c2)
    l = jnp.sum(p, axis=1, keepdims=True)
    acc = _dot(p.astype(BF16), cn[:, :KV_LORA])

    for c in range(past // MLA_DEC_KEYS):
        rows = slice(c * MLA_DEC_KEYS, (c + 1) * MLA_DEC_KEYS)
        ck = ck_ref[0, 0, rows, :].astype(BF16)
        kr_t = kr_ref[0, 0, :, rows].astype(BF16)
        s = _dot_nt(q_lat, ck) + _dot(q_rope, kr_t)
        m_new = jnp.maximum(m, jnp.max(s, axis=1, keepdims=True))
        p = jnp.exp2((s - m_new) * c2)
        alpha = jnp.exp2((m - m_new) * c2)
        l = alpha * l + jnp.sum(p, axis=1, keepdims=True)
        acc = alpha * acc + _dot(p.astype(BF16), ck)
        m = m_new
    o = (acc * (1.0 / l)).astype(BF16)
    for h in range(N_HEADS):
        o_ref[:, h * KV_LORA:(h + 1) * KV_LORA] = o[h * t_new:(h + 1) * t_new, :]


def _mla_sample(qm, cm, cache_ckv, cache_kr, layer, batch, t_new):
    past = cache_ckv.shape[2]
    assert past % MLA_DEC_KEYS == 0 and past % CHUNK == 0
    q_pos = past + np.arange(t_new)
    new_visible = bool(((q_pos[None, :] // CHUNK) <= (q_pos[:, None] // CHUNK)).all())
    return pl.pallas_call(
        functools.partial(_mladec_kernel, t_new=t_new, past=past, new_visible=new_visible),
        grid=(batch,),
        in_specs=[pl.BlockSpec((N_HEADS, t_new, QM_W), lambda b: (0, b, 0)),
                  pl.BlockSpec((t_new, QM_W), lambda b: (b, 0)),
                  pl.BlockSpec((1, 1, past, KV_LORA), lambda b: (layer, b, 0, 0)),
                  pl.BlockSpec((1, 1, MLA_ROPE, past), lambda b: (layer, b, 0, 0))],
        out_specs=pl.BlockSpec((t_new, N_HEADS * KV_LORA), lambda b: (b, 0)),
        out_shape=jax.ShapeDtypeStruct((batch * t_new, N_HEADS * KV_LORA), BF16),
        compiler_params=_params("parallel"), name="mla_sample",
    )(qm, cm, cache_ckv, cache_kr)


def _mix_kernel(x_ref, gpre_ref, osb_ref, olat_ref, wg_ref, wpsb_ref, wuv_ref, wpmla_ref, wout_ref,
                gpost_ref, o_ref):
    x = x_ref[...]
    hb = _rmsnorm(x, gpre_ref[...]).astype(BF16)
    gates = _dot(hb, wg_ref[...])
    y_a = _dot(osb_ref[...], wpsb_ref[...])
    o_mla = _dot(olat_ref[...], wuv_ref[...]).astype(BF16)
    y_b = _dot(o_mla, wpmla_ref[...])
    mixin = _sigmoid(gates[:, :D_MODEL]) * y_a + _sigmoid(gates[:, D_MODEL:]) * y_b
    mix = _dot(mixin.astype(BF16), wout_ref[...])
    o_ref[...] = x + _rmsnorm(mix, gpost_ref[...])


def _mix(x2d, o_sb, o_lat, lw):
    n = x2d.shape[0]
    tm = TOK_TILE
    tok = lambda w: pl.BlockSpec((tm, w), lambda i: (i, 0))
    return pl.pallas_call(
        _mix_kernel, grid=(n // tm,),
        in_specs=[tok(D_MODEL), _const_spec((1, D_MODEL)), tok(SB_W), tok(N_HEADS * KV_LORA),
                  _const_spec(lw["w_gates"].shape), _const_spec(lw["w_proj_sb"].shape),
                  _const_spec(lw["w_uv_bd"].shape), _const_spec(lw["w_proj_mla"].shape),
                  _const_spec(lw["w_out"].shape), _const_spec((1, D_MODEL))],
        out_specs=tok(D_MODEL), out_shape=jax.ShapeDtypeStruct((n, D_MODEL), F32),
        compiler_params=_params("parallel"), name="mix",
    )(x2d, lw["g_pre_mix"], o_sb, o_lat, lw["w_gates"], lw["w_proj_sb"], lw["w_uv_bd"], lw["w_proj_mla"],
      lw["w_out"], lw["g_post_mix"])


def _ffn_kernel(x_ref, gpre_ref, wup_ref, cw_ref, past_ref, wdown_ref, gpost_ref, wpg_ref, p_ref, wpp_ref,
                o_ref, conv_ref, *u_scr, nseq, rows):
    t = pl.program_id(1)
    span = SUBLANES + rows
    lead = SUBLANES - (CONV_W - 1)

    @pl.when(t == 0)
    def _():
        for c in range(2 * N_FF_CHUNKS):
            for s in range(nseq):
                u_scr[c][s * span + lead:s * span + SUBLANES, :] = past_ref[s, c]

    x = x_ref[...]
    hb = _rmsnorm(x, gpre_ref[...]).astype(BF16)

    def up(c):
        for cc in (c, N_FF_CHUNKS + c):
            u = _dot(hb, wup_ref[cc])
            for s in range(nseq):
                u_scr[cc][s * span + SUBLANES:(s + 1) * span, :] = u[s * rows:(s + 1) * rows, :]

    def conv(c):
        cw = cw_ref[c]
        u = u_scr[c]
        out = []
        for s in range(nseq):
            full = u[s * span:(s + 1) * span, :]
            uc = cw[3:4, :]
            for i in range(CONV_W):
                back = CONV_W - 1 - i
                shifted = full if back == 0 else pltpu.roll(full, back, axis=0)
                uc = uc + shifted[SUBLANES:, :] * cw[i:i + 1, :]
            conv_ref[s, c] = u[(s + 1) * span - (CONV_W - 1):(s + 1) * span, :]
            u[s * span:s * span + SUBLANES, :] = u[s * span + rows:(s + 1) * span, :]
            out.append(uc)
        return out[0] if nseq == 1 else jnp.concatenate(out, axis=0)

    up(0)
    f = None
    acts = []
    for c in range(N_FF_CHUNKS):
        if c + 1 < N_FF_CHUNKS:
            up(c + 1)
        acts.append((_gelu_tanh(conv(c)) * conv(N_FF_CHUNKS + c)).astype(BF16))
        if len(acts) == FF_DOWN_GROUP or c + 1 == N_FF_CHUNKS:
            k1 = (c + 1) * FF_CHUNK
            k0 = k1 - len(acts) * FF_CHUNK
            part = _dot(jnp.concatenate(acts, axis=1), wdown_ref[k0:k1, :])
            f = part if f is None else f + part
            acts = []
    x2 = x + _rmsnorm(f, gpost_ref[...])
    ple = _sigmoid(_dot(x2.astype(BF16), wpg_ref[...])) * _dot(p_ref[0].astype(BF16), wpp_ref[...])
    o_ref[...] = x2 + ple


def _ffn(x2d, p3d, layer, conv_past, batch, seq_len, lw):
    tm = TOK_TILE
    rows = min(tm, seq_len)
    nseq = tm // rows
    nt = seq_len // rows
    assert batch % nseq == 0 and (nseq == 1 or nt == 1)
    nch = 2 * N_FF_CHUNKS
    tok = lambda w: pl.BlockSpec((tm, w), lambda b, t: (b * nt + t, 0))
    ple = pl.BlockSpec((1, tm, D_PLE), lambda b, t: (layer, b * nt + t, 0))
    state = pl.BlockSpec((nseq, nch, CONV_W - 1, FF_CHUNK), lambda b, t: (b, 0, 0, 0))
    return pl.pallas_call(
        functools.partial(_ffn_kernel, nseq=nseq, rows=rows),
        grid=(batch // nseq, nt),
        in_specs=[tok(D_MODEL), _const_spec((1, D_MODEL)), _const_spec(lw["w_up"].shape),
                  _const_spec(lw["conv"].shape), state, _const_spec(lw["w_down"].shape),
                  _const_spec((1, D_MODEL)), _const_spec(lw["w_ple_gate"].shape), ple,
                  _const_spec(lw["w_ple_proj"].shape)],
        out_specs=[tok(D_MODEL), state],
        out_shape=[jax.ShapeDtypeStruct(x2d.shape, F32), jax.ShapeDtypeStruct(conv_past.shape, F32)],
        scratch_shapes=[pltpu.VMEM((nseq * (SUBLANES + rows), FF_CHUNK), F32) for _ in range(nch)],
        compiler_params=_params("arbitrary", "arbitrary"), name="ffn",
    )(x2d, lw["g_pre_ffn"], lw["w_up"], lw["conv"], conv_past, lw["w_down"], lw["g_post_ffn"],
      lw["w_ple_gate"], p3d, lw["w_ple_proj"])


def _prep_layer(w_in, ckv_gain, w_uk, w_uv, w_proj_sb, w_proj_mla, w_out, g_pre_mix, g_post_mix, g_pre_ffn,
                g_post_ffn, w_up, conv_w, conv_b, w_down, w_ple_gate, w_ple_proj):
    o_qm = 3 * SB_W
    o_ckv = o_qm + N_HEADS * (MLA_NOPE + MLA_ROPE)
    o_kr = o_ckv + KV_LORA
    o_g = o_kr + MLA_ROPE
    wqm = w_in[:, o_qm:o_ckv].reshape(D_MODEL, N_HEADS, MLA_NOPE + MLA_ROPE)
    w_qn = wqm[:, :, :MLA_NOPE].reshape(D_MODEL, N_HEADS * MLA_NOPE)
    w_qr = jnp.concatenate([wqm[:, :, MLA_NOPE:MLA_NOPE + ROPE_HALF].reshape(D_MODEL, LANES),
                            wqm[:, :, MLA_NOPE + ROPE_HALF:].reshape(D_MODEL, LANES)], axis=1)
    w_kr = w_in[:, o_kr:o_g]
    w_kx = jnp.zeros((D_MODEL, 2 * LANES), F32)
    w_kx = w_kx.at[:, :ROPE_HALF].set(w_kr[:, :ROPE_HALF]).at[:, LANES:LANES + ROPE_HALF].set(w_kr[:, ROPE_HALF:])
    wa = jnp.zeros((N_HEADS, LANES, QM_W), F32)
    wb = np.zeros((N_HEADS, 2 * LANES, QM_W), np.float32)
    for h in range(N_HEADS):
        r0 = (h % 2) * MLA_NOPE
        wa = wa.at[h, r0:r0 + MLA_NOPE, :KV_LORA].set(w_uk[:, h, :].T)
        for i in range(ROPE_HALF):
            wb[h, h * ROPE_HALF + i, KV_LORA + i] = 1.0
            wb[h, LANES + h * ROPE_HALF + i, KV_LORA + ROPE_HALF + i] = 1.0
    w_uv_bd = jnp.zeros((N_HEADS * KV_LORA, N_HEADS * MLA_V), F32)
    for h in range(N_HEADS):
        w_uv_bd = w_uv_bd.at[h * KV_LORA:(h + 1) * KV_LORA, h * MLA_V:(h + 1) * MLA_V].set(w_uv[:, h, :])
    nch = 2 * N_FF_CHUNKS
    conv = jnp.concatenate([conv_w, conv_b[None, :], jnp.zeros((SUBLANES - CONV_W - 1, 2 * D_FF), F32)], axis=0)
    row = lambda g: g.reshape(1, -1)
    return dict(
        w_qkv=w_in[:, :o_qm].astype(BF16), w_kvt=w_in[:, SB_W:o_qm].T.astype(BF16),
        w_qn=w_qn.astype(BF16), w_qr=w_qr.astype(BF16),
        w_ckv=w_in[:, o_ckv:o_kr].astype(BF16), w_kx=w_kx.astype(BF16), ckv_gain=row(ckv_gain),
        wa=wa.astype(BF16), wb=jnp.asarray(wb, BF16), eye=jnp.eye(LANES, dtype=BF16),
        w_gates=w_in[:, o_g:].astype(BF16), w_proj_sb=w_proj_sb.astype(BF16), w_uv_bd=w_uv_bd.astype(BF16),
        w_proj_mla=w_proj_mla.astype(BF16), w_out=w_out.astype(BF16),
        g_pre_mix=row(g_pre_mix), g_post_mix=row(g_post_mix), g_pre_ffn=row(g_pre_ffn), g_post_ffn=row(g_post_ffn),
        w_up=w_up.reshape(D_MODEL, nch, FF_CHUNK).transpose(1, 0, 2).astype(BF16),
        conv=conv.reshape(SUBLANES, nch, FF_CHUNK).transpose(1, 0, 2),
        w_down=w_down.astype(BF16),
        w_ple_gate=w_ple_gate.astype(BF16), w_ple_proj=w_ple_proj.astype(BF16))


def _rope_tables(pos, rows):
    freqs = jnp.power(ROPE_THETA, -jnp.arange(ROPE_HALF, dtype=F32) / ROPE_HALF)
    ang = pos.astype(F32)[:, None] * freqs[None, :]
    reps = max(1, rows // pos.shape[0])
    tile = lambda a: jnp.tile(a, (reps, LANES // ROPE_HALF))
    return tile(jnp.cos(ang)), tile(jnp.sin(ang))


def _conv_to_chunks(state):
    b = state.shape[0]
    return state.reshape(b, CONV_W - 1, 2 * N_FF_CHUNKS, FF_CHUNK).transpose(0, 2, 1, 3)


def _conv_from_chunks(state):
    b = state.shape[0]
    return state.transpose(0, 2, 1, 3).reshape(b, CONV_W - 1, 2 * D_FF)


def kernel(x_prompt, x_sample, p_prompt, p_sample, cache_sb_k, cache_sb_v, cache_mla_ckv, cache_mla_krope,
           state_ffn_conv, w_in, ckv_gain, w_uk, w_uv, w_proj_sb, w_proj_mla, w_out, g_pre_mix, g_post_mix,
           g_pre_ffn, g_post_ffn, w_up, conv_w, conv_b, w_down, w_ple_gate, w_ple_proj):
    bp, tp, _ = x_prompt.shape
    bs, ts, _ = x_sample.shape
    depth = w_in.shape[0]
    past = cache_sb_k.shape[2]
    assert tp % TOK_TILE == 0 and (bs * ts) % TOK_TILE == 0 and TOK_TILE % ts == 0
    tabs_p = _rope_tables(jnp.arange(tp, dtype=jnp.int32), TOK_TILE)
    tabs_s = _rope_tables(past + jnp.arange(ts, dtype=jnp.int32), TOK_TILE)
    xp = x_prompt.reshape(bp * tp, D_MODEL)
    xs = x_sample.reshape(bs * ts, D_MODEL)
    pp = p_prompt.reshape(depth, bp * tp, D_PLE)
    ps = p_sample.reshape(depth, bs * ts, D_PLE)
    ck_t = cache_sb_k.transpose(0, 1, 3, 4, 2)
    cv_t = cache_sb_v.transpose(0, 1, 3, 4, 2)
    kr_t = cache_mla_krope.transpose(0, 1, 3, 2)
    zero_conv = jnp.zeros((bp, 2 * N_FF_CHUNKS, CONV_W - 1, FF_CHUNK), F32)
    st_p, st_s = [], []
    kv_p = None
    for i in range(depth):
        lw = _prep_layer(w_in[i], ckv_gain[i], w_uk[i], w_uv[i], w_proj_sb[i], w_proj_mla[i], w_out[i],
                         g_pre_mix[i], g_post_mix[i], g_pre_ffn[i], g_post_ffn[i], w_up[i], conv_w[i],
                         conv_b[i], w_down[i], w_ple_gate[i], w_ple_proj[i])
        q, k_bf, kt_all, vt_all, vt, qm, cm, ct, c_st, r_st = _proj(xp, tp, tabs_p, lw, kv_states=(i, depth, kv_p))
        kv_p = (kt_all, vt_all)
        o_sb = _sb_prompt(q, k_bf, vt, bp, tp)
        o_lat = _mla_prompt(qm, cm, ct, bp, tp)
        x1 = _mix(xp, o_sb, o_lat, lw)
        xp, conv_p = _ffn(x1, pp, i, zero_conv, bp, tp, lw)
        st_p.append((c_st.reshape(bp, tp, KV_LORA), r_st.reshape(bp, tp, MLA_ROPE), _conv_from_chunks(conv_p)))
        q, k_st, v_st, qm, cm, c_st, r_st = _proj(xs, ts, tabs_s, lw)
        o_sb = _sb_sample(q, k_st, v_st, ck_t, cv_t, i, bs, ts)
        o_lat = _mla_sample(qm, cm, cache_mla_ckv, kr_t, i, bs, ts)
        x1 = _mix(xs, o_sb, o_lat, lw)
        xs, conv_s = _ffn(x1, ps, i, _conv_to_chunks(state_ffn_conv[i]), bs, ts, lw)
        st_s.append((k_st.reshape(bs, ts, N_HEADS, SB_HEAD_DIM), v_st.reshape(bs, ts, N_HEADS, SB_HEAD_DIM),
                     c_st.reshape(bs, ts, KV_LORA), r_st.reshape(bs, ts, MLA_ROPE), _conv_from_chunks(conv_s)))
    stack = lambda sts, k: jnp.stack([s[k] for s in sts])
    heads_last = lambda a: a.reshape(depth, bp, N_HEADS, SB_HEAD_DIM, tp).transpose(0, 1, 4, 2, 3)
    return ((xp.reshape(bp, tp, D_MODEL), xs.reshape(bs, ts, D_MODEL), heads_last(kv_p[0]), heads_last(kv_p[1]))
            + tuple(stack(st_p, k) for k in range(3)) + tuple(stack(st_s, k) for k in range(5)))
```

```python
import functools
import math

import jax
import jax.numpy as jnp
import numpy as np
from jax import lax
from jax.experimental import pallas as pl
from jax.experimental.pallas import tpu as pltpu

D_MODEL = 1024
N_HEADS = 8
SB_HEAD_DIM = 64
SB_W = N_HEADS * SB_HEAD_DIM
MLA_NOPE = 64
MLA_ROPE = 32
ROPE_HALF = MLA_ROPE // 2
KV_LORA = 128
MLA_V = 64
D_FF = 2816
CONV_W = 3
CHUNK = 64
D_PLE = 256
ROPE_THETA = 10000.0
EPS = 1e-6
SB_SCALE = SB_HEAD_DIM ** -0.5
MLA_SCALE = (MLA_NOPE + MLA_ROPE) ** -0.5
LOG2E = math.log2(math.e)

LANES = 128
SUBLANES = 8
MXU_DIM = 256
VMEM_LIMIT_BYTES = 56 * 1024 * 1024

BLK = MXU_DIM
TOK_TILE = 2 * MXU_DIM
FF_CHUNK = MXU_DIM
N_FF_CHUNKS = D_FF // FF_CHUNK
FF_DOWN_GROUP = 4
QM_W = 2 * LANES
DEC_KEYS = 2048
MLA_DEC_KEYS = 512
HEAD_GROUP = MXU_DIM // SB_HEAD_DIM
N_GROUPS = N_HEADS // HEAD_GROUP

BF16 = jnp.bfloat16
F32 = jnp.float32


def _dot(a, b):
    return jnp.dot(a, b, preferred_element_type=F32)


def _dot_nt(a, b):
    return lax.dot_general(a, b, (((1,), (1,)), ((), ())), preferred_element_type=F32)


def _rmsnorm(x, g):
    ms = jnp.mean(x * x, axis=-1, keepdims=True)
    return x * lax.rsqrt(ms + EPS) * g


def _sigmoid(x):
    return 1.0 / (1.0 + jnp.exp(-x))


def _softplus2(z2):
    return jnp.maximum(z2, 0.0) + jnp.log2(1.0 + jnp.exp2(-jnp.abs(z2)))


def _gelu_tanh(x):
    c = math.sqrt(2.0 / math.pi)
    return 0.5 * x * (1.0 + jnp.tanh(c * (x + 0.044715 * (x * x * x))))


def _params(*sem):
    return pltpu.CompilerParams(dimension_semantics=sem, vmem_limit_bytes=VMEM_LIMIT_BYTES)


def _const_spec(shape):
    return pl.BlockSpec(shape, lambda *_: (0,) * len(shape), pipeline_mode=pl.Buffered(1))


def _proj_kernel(x_ref, g_ref, cos_ref, sin_ref, wqkv_ref, wkvt_ref, wqn_ref, wqr_ref, wckv_ref, wkx_ref,
                 gain_ref, wa_ref, wb_ref, eye_ref, *refs, transposed, n_alias):
    out_refs = refs[n_alias:]
    if transposed:
        (q_ref, kbf_ref, ktst_ref, vtst_ref, vt_ref, qm_ref, cm_ref, ct_ref, cst_ref, rst_ref) = out_refs
    else:
        (q_ref, kst_ref, vst_ref, qm_ref, cm_ref, cst_ref, rst_ref) = out_refs
    hb = _rmsnorm(x_ref[...], g_ref[...]).astype(BF16)
    nsub = x_ref.shape[0] // BLK

    if transposed:
        qk = _dot(hb, wqkv_ref[:, :2 * SB_W])
        q_ref[...] = (qk[:, :SB_W] * SB_SCALE).astype(BF16)
        kb = qk[:, SB_W:].astype(BF16)
        even = lax.broadcasted_iota(jnp.int32, kb.shape, 1) % LANES < SB_HEAD_DIM
        kbf_ref[0] = jnp.where(even, kb, jnp.zeros_like(kb))
        kbf_ref[1] = jnp.where(even, jnp.zeros_like(kb), kb)
        kt = _dot_nt(wkvt_ref[:SB_W, :], hb)
        vt = _dot_nt(wkvt_ref[SB_W:, :], hb)
        ktst_ref[0, 0] = kt
        vtst_ref[0, 0] = vt
        for s in range(nsub):
            vt_ref[0, s] = vt[:, s * BLK:(s + 1) * BLK].astype(BF16)
    else:
        qkv = _dot(hb, wqkv_ref[...])
        q_ref[...] = (qkv[:, :SB_W] * SB_SCALE).astype(BF16)
        kst_ref[...] = qkv[:, SB_W:2 * SB_W]
        vst_ref[...] = qkv[:, 2 * SB_W:]

    cos = cos_ref[...]
    sin = sin_ref[...]
    qn = _dot(hb, wqn_ref[...]).astype(BF16)
    qr = _dot(hb, wqr_ref[...])
    x1, x2 = qr[:, :LANES], qr[:, LANES:]
    r12 = jnp.concatenate([x1 * cos - x2 * sin, x1 * sin + x2 * cos], axis=1).astype(BF16)
    for h in range(N_HEADS):
        p = h // 2
        qm_ref[h] = (_dot(qn[:, p * LANES:(p + 1) * LANES], wa_ref[h]) + _dot(r12, wb_ref[h])).astype(BF16)

    ckv = _rmsnorm(_dot(hb, wckv_ref[...]), gain_ref[...])
    cst_ref[...] = ckv
    kx = _dot(hb, wkx_ref[...])
    k1, k2 = kx[:, :LANES], kx[:, LANES:]
    kr1 = k1 * cos - k2 * sin
    kr2 = k1 * sin + k2 * cos
    lane = lax.broadcasted_iota(jnp.int32, kr1.shape, 1)
    kr = jnp.where(lane < ROPE_HALF, kr1, pltpu.roll(kr2, ROPE_HALF, axis=1))
    rst_ref[...] = kr[:, :MLA_ROPE]
    ckv_b = ckv.astype(BF16)
    cm_ref[...] = jnp.concatenate([ckv_b, kr.astype(BF16)], axis=1)
    if transposed:
        for s in range(nsub):
            ct_ref[0, s] = _dot_nt(eye_ref[...], ckv_b[s * BLK:(s + 1) * BLK, :]).astype(BF16)


def _proj(x2d, seq_len, tabs, lw, *, kv_states=None):
    n = x2d.shape[0]
    tm = TOK_TILE
    nt = n // tm
    cos, sin = tabs
    tab_blocks = cos.shape[0] // tm
    transposed = kv_states is not None
    tok = lambda w: pl.BlockSpec((tm, w), lambda i: (i, 0))
    tab = pl.BlockSpec((tm, LANES), lambda i: (i % tab_blocks, 0))
    in_specs = [tok(D_MODEL), _const_spec((1, D_MODEL)), tab, tab,
                _const_spec(lw["w_qkv"].shape), _const_spec(lw["w_kvt"].shape),
                _const_spec(lw["w_qn"].shape), _const_spec(lw["w_qr"].shape),
                _const_spec(lw["w_ckv"].shape), _const_spec(lw["w_kx"].shape), _const_spec((1, KV_LORA)),
                _const_spec(lw["wa"].shape), _const_spec(lw["wb"].shape), _const_spec((LANES, LANES))]
    args = [x2d, lw["g_pre_mix"], cos, sin, lw["w_qkv"], lw["w_kvt"], lw["w_qn"], lw["w_qr"], lw["w_ckv"],
            lw["w_kx"], lw["ckv_gain"], lw["wa"], lw["wb"], lw["eye"]]
    aliases = {}
    if transposed:
        layer, depth, prev = kv_states
        assert seq_len % tm == 0
        batch = n // seq_len
        nb = seq_len // tm
        sub = tm // BLK
        state = jax.ShapeDtypeStruct((depth, batch, SB_W, seq_len), F32)
        state_spec = pl.BlockSpec((1, 1, SB_W, tm), lambda i: (layer, i // nb, 0, i % nb))
        blocks = lambda w: (jax.ShapeDtypeStruct((batch, nb * sub, w, BLK), BF16),
                            pl.BlockSpec((1, sub, w, BLK), lambda i: (i // nb, i % nb, 0, 0)))
        outs = [(jax.ShapeDtypeStruct((n, SB_W), BF16), tok(SB_W)),
                (jax.ShapeDtypeStruct((2, n, SB_W), BF16),
                 pl.BlockSpec((2, tm, SB_W), lambda i: (0, i, 0))),
                (state, state_spec), (state, state_spec), blocks(SB_W)]
        if prev is not None:
            aliases = {len(args): 2, len(args) + 1: 3}
            args += list(prev)
            in_specs += [pl.BlockSpec(memory_space=pl.ANY)] * 2
    else:
        outs = [(jax.ShapeDtypeStruct((n, SB_W), BF16), tok(SB_W)),
                (jax.ShapeDtypeStruct((n, SB_W), F32), tok(SB_W)), (jax.ShapeDtypeStruct((n, SB_W), F32), tok(SB_W))]
    outs += [(jax.ShapeDtypeStruct((N_HEADS, n, QM_W), BF16), pl.BlockSpec((N_HEADS, tm, QM_W), lambda i: (0, i, 0))),
             (jax.ShapeDtypeStruct((n, QM_W), BF16), tok(QM_W))]
    if transposed:
        outs.append(blocks(KV_LORA))
    outs += [(jax.ShapeDtypeStruct((n, KV_LORA), F32), tok(KV_LORA)),
             (jax.ShapeDtypeStruct((n, MLA_ROPE), F32), tok(MLA_ROPE))]
    return pl.pallas_call(
        functools.partial(_proj_kernel, transposed=transposed, n_alias=len(aliases)),
        grid=(nt,), in_specs=in_specs, out_specs=[o[1] for o in outs], out_shape=[o[0] for o in outs],
        input_output_aliases=aliases,
        compiler_params=_params("parallel"), name="proj_t" if transposed else "proj",
    )(*args)


def _sb_scores(q_ref, k_ref, qrows, j, z_out):
    krows = pl.ds(pl.multiple_of(j * BLK, BLK), BLK)
    for h in range(N_HEADS):
        p, hh = h // 2, h % 2
        cols = slice(p * LANES, (p + 1) * LANES)
        kh = k_ref[hh, 0, krows, cols]
        z_out[h] = _dot_nt(kh, q_ref[0, qrows, cols]) * LOG2E


def _sb_block(z_in, vt_ref, j, acc_ref, car_ref, umat, vis):
    heads = range(N_HEADS)
    mms = []
    for h in heads:
        sp = _softplus2(z_in[h])
        if vis is not None:
            sp = jnp.where(vis, sp, 0.0)
        mms.append(_dot(umat, sp.astype(BF16)))
    for h in heads:
        car = car_ref[h, 0:1, :]
        a = jnp.exp2(z_in[h] + mms[h] + car)
        if vis is not None:
            a = jnp.where(vis, a, 0.0)
        rows = slice(h * SB_HEAD_DIM, (h + 1) * SB_HEAD_DIM)
        acc_ref[rows, :] += _dot(vt_ref[0, j, rows, :], a.astype(BF16))
        car_ref[h, 0:1, :] = car + mms[h][0:1, :]


def _sb_kernel(q_ref, k_ref, vt_ref, o_ref, acc_ref, car_ref, z0_ref, z1_ref, *, nblk):
    kidx = lax.broadcasted_iota(jnp.int32, (BLK, BLK), 0)
    qidx = lax.broadcasted_iota(jnp.int32, (BLK, BLK), 1)
    umat = jnp.where(qidx >= kidx, -1.0, 0.0).astype(BF16)
    vis = kidx < qidx

    def q_block(i, carry):
        acc_ref[...] = jnp.zeros_like(acc_ref)
        car_ref[...] = jnp.zeros_like(car_ref)
        qrows = pl.ds(pl.multiple_of(i * BLK, BLK), BLK)

        def step(j, z_cur, z_next, mask):
            _sb_scores(q_ref, k_ref, qrows, jnp.maximum(j - 1, 0), z_next)
            _sb_block(z_cur, vt_ref, j, acc_ref, car_ref, umat, mask)

        _sb_scores(q_ref, k_ref, qrows, i, z0_ref)
        step(i, z0_ref, z1_ref, vis)

        def k_pair(m, c):
            j = i - 1 - 2 * m
            step(j, z1_ref, z0_ref, None)
            step(j - 1, z0_ref, z1_ref, None)
            return c

        lax.fori_loop(0, i // 2, k_pair, 0)

        @pl.when(i % 2 == 1)
        def _():
            step(0, z1_ref, z0_ref, None)

        o_ref[0, qrows, :] = acc_ref[...].T.astype(BF16)
        return carry

    lax.fori_loop(0, nblk, q_block, 0)


def _sb_prompt(q, kbf, vt, batch, seq_len):
    nblk = seq_len // BLK
    q3 = q.reshape(batch, seq_len, SB_W)
    k4 = kbf.reshape(2, batch, seq_len, SB_W)
    row = pl.BlockSpec((1, seq_len, SB_W), lambda b: (b, 0, 0))
    out = pl.pallas_call(
        functools.partial(_sb_kernel, nblk=nblk),
        grid=(batch,),
        in_specs=[row, pl.BlockSpec((2, 1, seq_len, SB_W), lambda b: (0, b, 0, 0)),
                  pl.BlockSpec((1, nblk, SB_W, BLK), lambda b: (b, 0, 0, 0))],
        out_specs=row,
        out_shape=jax.ShapeDtypeStruct((batch, seq_len, SB_W), BF16),
        scratch_shapes=[pltpu.VMEM((SB_W, BLK), F32), pltpu.VMEM((N_HEADS, SUBLANES, BLK), F32),
                        pltpu.VMEM((N_HEADS, BLK, BLK), F32), pltpu.VMEM((N_HEADS, BLK, BLK), F32)],
        compiler_params=_params("parallel"), name="sb_prompt",
    )(q3, k4, vt)
    return out.reshape(batch * seq_len, SB_W)


def _mla_scores(q_ref, c_ref, qrows, j, s_out):
    c = c_ref[pl.ds(pl.multiple_of(j * BLK, BLK), BLK), :]
    for h in range(N_HEADS):
        s_out[h] = _dot_nt(c, q_ref[h, qrows, :])


def _mla_block(s_in, ct_ref, j, acc_ref, st_ref, vis):
    ct = ct_ref[0, j]
    for h in range(N_HEADS):
        s = s_in[h]
        if vis is not None:
            s = jnp.where(vis, s, -jnp.inf)
        m_old = st_ref[h, 0:1, :]
        m_new = jnp.maximum(m_old, jnp.max(s, axis=0, keepdims=True))
        p = jnp.exp2((s - m_new) * (MLA_SCALE * LOG2E))
        alpha = jnp.exp2((m_old - m_new) * (MLA_SCALE * LOG2E))
        st_ref[h, 1:2, :] = alpha * st_ref[h, 1:2, :] + jnp.sum(p, axis=0, keepdims=True)
        st_ref[h, 0:1, :] = m_new
        acc_ref[h] = alpha * acc_ref[h] + _dot(ct, p.astype(BF16))


def _mla_kernel(q_ref, c_ref, ct_ref, o_ref, acc_ref, st_ref, s0_ref, s1_ref, *, nblk):
    kidx = lax.broadcasted_iota(jnp.int32, (BLK, BLK), 0)
    qidx = lax.broadcasted_iota(jnp.int32, (BLK, BLK), 1)
    vis = (kidx // CHUNK) <= (qidx // CHUNK)

    def q_block(i, carry):
        acc_ref[...] = jnp.zeros_like(acc_ref)
        for h in range(N_HEADS):
            st_ref[h, 0:1, :] = jnp.full((1, BLK), -jnp.inf, F32)
            st_ref[h, 1:2, :] = jnp.zeros((1, BLK), F32)
        rows = pl.ds(pl.multiple_of(i * BLK, BLK), BLK)
        def step(j, j_next, s_cur, s_next, mask):
            _mla_scores(q_ref, c_ref, rows, j_next, s_next)
            _mla_block(s_cur, ct_ref, j, acc_ref, st_ref, mask)

        nxt = lambda j: jnp.minimum(j + 1, nblk - 1)
        _mla_scores(q_ref, c_ref, rows, i, s0_ref)
        step(i, 0, s0_ref, s1_ref, vis)

        def k_pair(m, c):
            j = 2 * m
            step(j, nxt(j), s1_ref, s0_ref, None)
            step(j + 1, nxt(j + 1), s0_ref, s1_ref, None)
            return c

        lax.fori_loop(0, i // 2, k_pair, 0)

        @pl.when(i % 2 == 1)
        def _():
            step(i - 1, nxt(i - 1), s1_ref, s0_ref, None)

        for h in range(N_HEADS):
            o = acc_ref[h] * (1.0 / st_ref[h, 1:2, :])
            o_ref[rows, h * KV_LORA:(h + 1) * KV_LORA] = o.T.astype(BF16)
        return carry

    lax.fori_loop(0, nblk, q_block, 0)


def _mla_prompt(qm, cm, ct, batch, seq_len):
    nblk = seq_len // BLK
    return pl.pallas_call(
        functools.partial(_mla_kernel, nblk=nblk),
        grid=(batch,),
        in_specs=[pl.BlockSpec((N_HEADS, seq_len, QM_W), lambda b: (0, b, 0)),
                  pl.BlockSpec((seq_len, QM_W), lambda b: (b, 0)),
                  pl.BlockSpec((1, nblk, KV_LORA, BLK), lambda b: (b, 0, 0, 0))],
        out_specs=pl.BlockSpec((seq_len, N_HEADS * KV_LORA), lambda b: (b, 0)),
        out_shape=jax.ShapeDtypeStruct((batch * seq_len, N_HEADS * KV_LORA), BF16),
        scratch_shapes=[pltpu.VMEM((N_HEADS, KV_LORA, BLK), F32), pltpu.VMEM((N_HEADS, SUBLANES, BLK), F32),
                        pltpu.VMEM((N_HEADS, BLK, BLK), F32), pltpu.VMEM((N_HEADS, BLK, BLK), F32)],
        compiler_params=_params("parallel"), name="mla_prompt",
    )(qm, cm, ct)


def _sbdec_blocks(score, av, nblocks, t, acc_ref, car_ref, wn, vis):
    heads = range(N_HEADS)
    zs = [jnp.concatenate([score(b, h) for h in heads], axis=0) * LOG2E
          for b in range(nblocks)]
    mms = []
    for b in range(nblocks):
        sp = _softplus2(zs[b])
        if vis is not None:
            sp = jnp.where(vis, sp, 0.0)
        mms.append(_dot(sp.astype(BF16), wn))
    car = car_ref[:, 0:1]
    for b in range(nblocks):
        a = jnp.exp2(zs[b] + mms[b] + car)
        if vis is not None:
            a = jnp.where(vis, a, 0.0)
        a = a.astype(BF16)
        for h in heads:
            acc_ref[h] += av(b, h, a[h * t:(h + 1) * t, :])
        car = car + mms[b][:, 0:1]
    car_ref[...] = jnp.broadcast_to(car, car_ref.shape)


def _sbdec_kernel(q_ref, kn_ref, vn_ref, ck_ref, cv_ref, o_ref, qh_ref, acc_ref, car_ref, *, t_new):
    j = pl.program_id(1)
    hcols = lambda h: slice(h * SB_HEAD_DIM, (h + 1) * SB_HEAD_DIM)

    @pl.when(j == 0)
    def _():
        acc_ref[...] = jnp.zeros_like(acc_ref)
        car_ref[...] = jnp.zeros_like(car_ref)
        for h in range(N_HEADS):
            qh_ref[h] = q_ref[:, hcols(h)]
        kp = lax.broadcasted_iota(jnp.int32, (t_new, t_new), 0)
        kc = lax.broadcasted_iota(jnp.int32, (t_new, t_new), 1)
        wn_new = jnp.where(kp >= kc, -1.0, 0.0).astype(BF16)
        qi = lax.rem(lax.broadcasted_iota(jnp.int32, (N_HEADS * t_new, t_new), 0), t_new)
        ki = lax.broadcasted_iota(jnp.int32, (N_HEADS * t_new, t_new), 1)
        _sbdec_blocks(lambda b, h: _dot_nt(qh_ref[h], kn_ref[:, hcols(h)].astype(BF16)),
                      lambda b, h, a: _dot(a, vn_ref[:, hcols(h)].astype(BF16)),
                      1, t_new, acc_ref, car_ref, wn_new, ki < qi)

    kp = lax.broadcasted_iota(jnp.int32, (BLK, BLK), 0)
    kc = lax.broadcasted_iota(jnp.int32, (BLK, BLK), 1)
    wn = jnp.where(kp >= kc, -1.0, 0.0).astype(BF16)
    nsub = DEC_KEYS // BLK
    keys = lambda b: slice((nsub - 1 - b) * BLK, (nsub - b) * BLK)
    _sbdec_blocks(lambda b, h: _dot(qh_ref[h], ck_ref[0, 0, h, :, keys(b)].astype(BF16)),
                  lambda b, h, a: _dot_nt(a, cv_ref[0, 0, h, :, keys(b)].astype(BF16)),
                  nsub, t_new, acc_ref, car_ref, wn, None)

    @pl.when(j == pl.num_programs(1) - 1)
    def _():
        for h in range(N_HEADS):
            o_ref[:, hcols(h)] = acc_ref[h].astype(BF16)


def _sb_sample(q, k_new, v_new, cache_k, cache_v, layer, batch, t_new):
    past = cache_k.shape[4]
    assert past % DEC_KEYS == 0
    nkb = past // DEC_KEYS
    tok = lambda: pl.BlockSpec((t_new, SB_W), lambda b, j: (b, 0))
    cache = lambda: pl.BlockSpec((1, 1, N_HEADS, SB_HEAD_DIM, DEC_KEYS), lambda b, j: (layer, b, 0, 0, nkb - 1 - j))
    return pl.pallas_call(
        functools.partial(_sbdec_kernel, t_new=t_new),
        grid=(batch, nkb),
        in_specs=[tok(), tok(), tok(), cache(), cache()],
        out_specs=tok(),
        out_shape=jax.ShapeDtypeStruct((batch * t_new, SB_W), BF16),
        scratch_shapes=[pltpu.VMEM((N_HEADS, t_new, SB_HEAD_DIM), BF16),
                        pltpu.VMEM((N_HEADS, t_new, SB_HEAD_DIM), F32),
                        pltpu.VMEM((N_HEADS * t_new, LANES), F32)],
        compiler_params=_params("parallel", "arbitrary"), name="sb_sample",
    )(q, k_new, v_new, cache_k, cache_v)


def _mladec_kernel(q_ref, cn_ref, ck_ref, kr_ref, o_ref, *, t_new, past, new_visible):
    q = q_ref[...].reshape(N_HEADS * t_new, QM_W)
    q_lat = q[:, :KV_LORA]
    q_rope = q[:, KV_LORA:KV_LORA + MLA_ROPE]
    c2 = MLA_SCALE * LOG2E

    cn = cn_ref[...]
    s = _dot_nt(q, cn)
    if not new_visible:
        qi = past + lax.broadcasted_iota(jnp.int32, s.shape, 0) % t_new
        ki = past + lax.broadcasted_iota(jnp.int32, s.shape, 1)
        s = jnp.where((ki // CHUNK) <= (qi // CHUNK), s, -jnp.inf)
    chunks = [slice(c * MLA_DEC_KEYS, (c + 1) * MLA_DEC_KEYS) for c in range(past // MLA_DEC_KEYS)]
    cks = [ck_ref[0, 0, rows, :].astype(BF16) for rows in chunks]
    ss = [_dot_nt(q_lat, ck) + _dot(q_rope, kr_ref[0, 0, :, rows].astype(BF16))
          for ck, rows in zip(cks, chunks)]
    m = jnp.max(s, axis=1, keepdims=True)
    for sc in ss:
        m = jnp.maximum(m, jnp.max(sc, axis=1, keepdims=True))
    p = jnp.exp2((s - m) * c2)
    l = jnp.sum(p, axis=1, keepdims=True)
    acc = _dot(p.astype(BF16), cn[:, :KV_LORA])
    for sc, ck in zip(ss, cks):
        p = jnp.exp2((sc - m) * c2)
        l = l + jnp.sum(p, axis=1, keepdims=True)
        acc = acc + _dot(p.astype(BF16), ck)
    o = (acc * (1.0 / l)).astype(BF16)
    for h in range(N_HEADS):
        o_ref[:, h * KV_LORA:(h + 1) * KV_LORA] = o[h * t_new:(h + 1) * t_new, :]


def _mla_sample(qm, cm, cache_ckv, cache_kr, layer, batch, t_new):
    past = cache_ckv.shape[2]
    assert past % MLA_DEC_KEYS == 0 and past % CHUNK == 0
    q_pos = past + np.arange(t_new)
    new_visible = bool(((q_pos[None, :] // CHUNK) <= (q_pos[:, None] // CHUNK)).all())
    return pl.pallas_call(
        functools.partial(_mladec_kernel, t_new=t_new, past=past, new_visible=new_visible),
        grid=(batch,),
        in_specs=[pl.BlockSpec((N_HEADS, t_new, QM_W), lambda b: (0, b, 0)),
                  pl.BlockSpec((t_new, QM_W), lambda b: (b, 0)),
                  pl.BlockSpec((1, 1, past, KV_LORA), lambda b: (layer, b, 0, 0)),
                  pl.BlockSpec((1, 1, MLA_ROPE, past), lambda b: (layer, b, 0, 0))],
        out_specs=pl.BlockSpec((t_new, N_HEADS * KV_LORA), lambda b: (b, 0)),
        out_shape=jax.ShapeDtypeStruct((batch * t_new, N_HEADS * KV_LORA), BF16),
        compiler_params=_params("parallel"), name="mla_sample",
    )(qm, cm, cache_ckv, cache_kr)


def _mix_kernel(x_ref, gpre_ref, osb_ref, olat_ref, wg_ref, wpsb_ref, wuv_ref, wpmla_ref, wout_ref,
                gpost_ref, o_ref):
    x = x_ref[...]
    hb = _rmsnorm(x, gpre_ref[...]).astype(BF16)
    gates = _dot(hb, wg_ref[...])
    y_a = _dot(osb_ref[...], wpsb_ref[...])
    o_mla = _dot(olat_ref[...], wuv_ref[...]).astype(BF16)
    y_b = _dot(o_mla, wpmla_ref[...])
    mixin = _sigmoid(gates[:, :D_MODEL]) * y_a + _sigmoid(gates[:, D_MODEL:]) * y_b
    mix = _dot(mixin.astype(BF16), wout_ref[...])
    o_ref[...] = x + _rmsnorm(mix, gpost_ref[...])


def _mix(x2d, o_sb, o_lat, lw):
    n = x2d.shape[0]
    tm = TOK_TILE
    tok = lambda w: pl.BlockSpec((tm, w), lambda i: (i, 0))
    return pl.pallas_call(
        _mix_kernel, grid=(n // tm,),
        in_specs=[tok(D_MODEL), _const_spec((1, D_MODEL)), tok(SB_W), tok(N_HEADS * KV_LORA),
                  _const_spec(lw["w_gates"].shape), _const_spec(lw["w_proj_sb"].shape),
                  _const_spec(lw["w_uv_bd"].shape), _const_spec(lw["w_proj_mla"].shape),
                  _const_spec(lw["w_out"].shape), _const_spec((1, D_MODEL))],
        out_specs=tok(D_MODEL), out_shape=jax.ShapeDtypeStruct((n, D_MODEL), F32),
        compiler_params=_params("parallel"), name="mix",
    )(x2d, lw["g_pre_mix"], o_sb, o_lat, lw["w_gates"], lw["w_proj_sb"], lw["w_uv_bd"], lw["w_proj_mla"],
      lw["w_out"], lw["g_post_mix"])


def _ffn_kernel(x_ref, gpre_ref, wup_ref, cw_ref, past_ref, wdown_ref, gpost_ref, wpg_ref, p_ref, wpp_ref,
                o_ref, conv_ref, *u_scr, nseq, rows):
    t = pl.program_id(1)
    span = SUBLANES + rows
    lead = SUBLANES - (CONV_W - 1)

    @pl.when(t == 0)
    def _():
        for c in range(2 * N_FF_CHUNKS):
            for s in range(nseq):
                u_scr[c][s * span + lead:s * span + SUBLANES, :] = past_ref[s, c]

    x = x_ref[...]
    hb = _rmsnorm(x, gpre_ref[...]).astype(BF16)

    def up(c):
        for cc in (c, N_FF_CHUNKS + c):
            u = _dot(hb, wup_ref[:, cc * FF_CHUNK:(cc + 1) * FF_CHUNK])
            for s in range(nseq):
                u_scr[cc][s * span + SUBLANES:(s + 1) * span, :] = u[s * rows:(s + 1) * rows, :]

    def conv(c):
        cw = cw_ref[c]
        u = u_scr[c]
        out = []
        for s in range(nseq):
            full = u[s * span:(s + 1) * span, :]
            uc = cw[3:4, :]
            for i in range(CONV_W):
                back = CONV_W - 1 - i
                shifted = full if back == 0 else pltpu.roll(full, back, axis=0)
                uc = uc + shifted[SUBLANES:, :] * cw[i:i + 1, :]
            conv_ref[s, c] = u[(s + 1) * span - (CONV_W - 1):(s + 1) * span, :]
            u[s * span:s * span + SUBLANES, :] = u[s * span + rows:(s + 1) * span, :]
            out.append(uc)
        return out[0] if nseq == 1 else jnp.concatenate(out, axis=0)

    up(0)
    f = None
    acts = []
    for c in range(N_FF_CHUNKS):
        if c + 1 < N_FF_CHUNKS:
            up(c + 1)
        acts.append((_gelu_tanh(conv(c)) * conv(N_FF_CHUNKS + c)).astype(BF16))
        if len(acts) == FF_DOWN_GROUP or c + 1 == N_FF_CHUNKS:
            k1 = (c + 1) * FF_CHUNK
            k0 = k1 - len(acts) * FF_CHUNK
            part = _dot(jnp.concatenate(acts, axis=1), wdown_ref[k0:k1, :])
            f = part if f is None else f + part
            acts = []
    x2 = x + _rmsnorm(f, gpost_ref[...])
    ple = _sigmoid(_dot(x2.astype(BF16), wpg_ref[...])) * _dot(p_ref[0].astype(BF16), wpp_ref[...])
    o_ref[...] = x2 + ple


def _ffn(x2d, p3d, layer, conv_past, batch, seq_len, lw):
    tm = TOK_TILE
    rows = min(tm, seq_len)
    nseq = tm // rows
    nt = seq_len // rows
    assert batch % nseq == 0 and (nseq == 1 or nt == 1)
    nch = 2 * N_FF_CHUNKS
    tok = lambda w: pl.BlockSpec((tm, w), lambda b, t: (b * nt + t, 0))
    ple = pl.BlockSpec((1, tm, D_PLE), lambda b, t: (layer, b * nt + t, 0))
    state = pl.BlockSpec((nseq, nch, CONV_W - 1, FF_CHUNK), lambda b, t: (b, 0, 0, 0))
    return pl.pallas_call(
        functools.partial(_ffn_kernel, nseq=nseq, rows=rows),
        grid=(batch // nseq, nt),
        in_specs=[tok(D_MODEL), _const_spec((1, D_MODEL)), _const_spec(lw["w_up"].shape),
                  _const_spec(lw["conv"].shape), state, _const_spec(lw["w_down"].shape),
                  _const_spec((1, D_MODEL)), _const_spec(lw["w_ple_gate"].shape), ple,
                  _const_spec(lw["w_ple_proj"].shape)],
        out_specs=[tok(D_MODEL), state],
        out_shape=[jax.ShapeDtypeStruct(x2d.shape, F32), jax.ShapeDtypeStruct(conv_past.shape, F32)],
        scratch_shapes=[pltpu.VMEM((nseq * (SUBLANES + rows), FF_CHUNK), F32) for _ in range(nch)],
        compiler_params=_params("arbitrary", "arbitrary"), name="ffn",
    )(x2d, lw["g_pre_ffn"], lw["w_up"], lw["conv"], conv_past, lw["w_down"], lw["g_post_ffn"],
      lw["w_ple_gate"], p3d, lw["w_ple_proj"])


def _prep_layer(w_in, ckv_gain, w_uk, w_uv, w_proj_sb, w_proj_mla, w_out, g_pre_mix, g_post_mix, g_pre_ffn,
                g_post_ffn, w_up, conv_w, conv_b, w_down, w_ple_gate, w_ple_proj):
    o_qm = 3 * SB_W
    o_ckv = o_qm + N_HEADS * (MLA_NOPE + MLA_ROPE)
    o_kr = o_ckv + KV_LORA
    o_g = o_kr + MLA_ROPE
    wqm = w_in[:, o_qm:o_ckv].reshape(D_MODEL, N_HEADS, MLA_NOPE + MLA_ROPE)
    w_qn = wqm[:, :, :MLA_NOPE].reshape(D_MODEL, N_HEADS * MLA_NOPE)
    w_qr = jnp.concatenate([wqm[:, :, MLA_NOPE:MLA_NOPE + ROPE_HALF].reshape(D_MODEL, LANES),
                            wqm[:, :, MLA_NOPE + ROPE_HALF:].reshape(D_MODEL, LANES)], axis=1)
    w_kr = w_in[:, o_kr:o_g]
    w_kx = jnp.zeros((D_MODEL, 2 * LANES), F32)
    w_kx = w_kx.at[:, :ROPE_HALF].set(w_kr[:, :ROPE_HALF]).at[:, LANES:LANES + ROPE_HALF].set(w_kr[:, ROPE_HALF:])
    wa = jnp.zeros((N_HEADS, LANES, QM_W), F32)
    wb = np.zeros((N_HEADS, 2 * LANES, QM_W), np.float32)
    for h in range(N_HEADS):
        r0 = (h % 2) * MLA_NOPE
        wa = wa.at[h, r0:r0 + MLA_NOPE, :KV_LORA].set(w_uk[:, h, :].T)
        for i in range(ROPE_HALF):
            wb[h, h * ROPE_HALF + i, KV_LORA + i] = 1.0
            wb[h, LANES + h * ROPE_HALF + i, KV_LORA + ROPE_HALF + i] = 1.0
    w_uv_bd = jnp.zeros((N_HEADS * KV_LORA, N_HEADS * MLA_V), F32)
    for h in range(N_HEADS):
        w_uv_bd = w_uv_bd.at[h * KV_LORA:(h + 1) * KV_LORA, h * MLA_V:(h + 1) * MLA_V].set(w_uv[:, h, :])
    nch = 2 * N_FF_CHUNKS
    conv = jnp.concatenate([conv_w, conv_b[None, :], jnp.zeros((SUBLANES - CONV_W - 1, 2 * D_FF), F32)], axis=0)
    row = lambda g: g.reshape(1, -1)
    return dict(
        w_qkv=w_in[:, :o_qm].astype(BF16), w_kvt=w_in[:, SB_W:o_qm].T.astype(BF16),
        w_qn=w_qn.astype(BF16), w_qr=w_qr.astype(BF16),
        w_ckv=w_in[:, o_ckv:o_kr].astype(BF16), w_kx=w_kx.astype(BF16), ckv_gain=row(ckv_gain),
        wa=wa.astype(BF16), wb=jnp.asarray(wb, BF16), eye=jnp.eye(LANES, dtype=BF16),
        w_gates=w_in[:, o_g:].astype(BF16), w_proj_sb=w_proj_sb.astype(BF16), w_uv_bd=w_uv_bd.astype(BF16),
        w_proj_mla=w_proj_mla.astype(BF16), w_out=w_out.astype(BF16),
        g_pre_mix=row(g_pre_mix), g_post_mix=row(g_post_mix), g_pre_ffn=row(g_pre_ffn), g_post_ffn=row(g_post_ffn),
        w_up=w_up.astype(BF16),
        conv=conv.reshape(SUBLANES, nch, FF_CHUNK).transpose(1, 0, 2),
        w_down=w_down.astype(BF16),
        w_ple_gate=w_ple_gate.astype(BF16), w_ple_proj=w_ple_proj.astype(BF16))


def _rope_tables(pos, rows):
    freqs = jnp.power(ROPE_THETA, -jnp.arange(ROPE_HALF, dtype=F32) / ROPE_HALF)
    ang = pos.astype(F32)[:, None] * freqs[None, :]
    reps = max(1, rows // pos.shape[0])
    tile = lambda a: jnp.tile(a, (reps, LANES // ROPE_HALF))
    return tile(jnp.cos(ang)), tile(jnp.sin(ang))


def _conv_to_chunks(state):
    b = state.shape[0]
    return state.reshape(b, CONV_W - 1, 2 * N_FF_CHUNKS, FF_CHUNK).transpose(0, 2, 1, 3)


def _conv_from_chunks(state):
    b = state.shape[0]
    return state.transpose(0, 2, 1, 3).reshape(b, CONV_W - 1, 2 * D_FF)


def kernel(x_prompt, x_sample, p_prompt, p_sample, cache_sb_k, cache_sb_v, cache_mla_ckv, cache_mla_krope,
           state_ffn_conv, w_in, ckv_gain, w_uk, w_uv, w_proj_sb, w_proj_mla, w_out, g_pre_mix, g_post_mix,
           g_pre_ffn, g_post_ffn, w_up, conv_w, conv_b, w_down, w_ple_gate, w_ple_proj):
    bp, tp, _ = x_prompt.shape
    bs, ts, _ = x_sample.shape
    depth = w_in.shape[0]
    past = cache_sb_k.shape[2]
    assert tp % TOK_TILE == 0 and (bs * ts) % TOK_TILE == 0 and TOK_TILE % ts == 0
    tabs_p = _rope_tables(jnp.arange(tp, dtype=jnp.int32), TOK_TILE)
    tabs_s = _rope_tables(past + jnp.arange(ts, dtype=jnp.int32), TOK_TILE)
    xp = x_prompt.reshape(bp * tp, D_MODEL)
    xs = x_sample.reshape(bs * ts, D_MODEL)
    pp = p_prompt.reshape(depth, bp * tp, D_PLE)
    ps = p_sample.reshape(depth, bs * ts, D_PLE)
    ck_t = cache_sb_k.transpose(0, 1, 3, 4, 2)
    cv_t = cache_sb_v.transpose(0, 1, 3, 4, 2)
    kr_t = cache_mla_krope.transpose(0, 1, 3, 2)
    zero_conv = jnp.zeros((bp, 2 * N_FF_CHUNKS, CONV_W - 1, FF_CHUNK), F32)
    st_p, st_s = [], []
    kv_p = None
    for i in range(depth):
        lw = _prep_layer(w_in[i], ckv_gain[i], w_uk[i], w_uv[i], w_proj_sb[i], w_proj_mla[i], w_out[i],
                         g_pre_mix[i], g_post_mix[i], g_pre_ffn[i], g_post_ffn[i], w_up[i], conv_w[i],
                         conv_b[i], w_down[i], w_ple_gate[i], w_ple_proj[i])
        q, k_bf, kt_all, vt_all, vt, qm, cm, ct, c_st, r_st = _proj(xp, tp, tabs_p, lw, kv_states=(i, depth, kv_p))
        kv_p = (kt_all, vt_all)
        o_sb = _sb_prompt(q, k_bf, vt, bp, tp)
        o_lat = _mla_prompt(qm, cm, ct, bp, tp)
        x1 = _mix(xp, o_sb, o_lat, lw)
        xp, conv_p = _ffn(x1, pp, i, zero_conv, bp, tp, lw)
        st_p.append((c_st.reshape(bp, tp, KV_LORA), r_st.reshape(bp, tp, MLA_ROPE), _conv_from_chunks(conv_p)))
        q, k_st, v_st, qm, cm, c_st, r_st = _proj(xs, ts, tabs_s, lw)
        o_sb = _sb_sample(q, k_st, v_st, ck_t, cv_t, i, bs, ts)
        o_lat = _mla_sample(qm, cm, cache_mla_ckv, kr_t, i, bs, ts)
        x1 = _mix(xs, o_sb, o_lat, lw)
        xs, conv_s = _ffn(x1, ps, i, _conv_to_chunks(state_ffn_conv[i]), bs, ts, lw)
        st_s.append((k_st.reshape(bs, ts, N_HEADS, SB_HEAD_DIM), v_st.reshape(bs, ts, N_HEADS, SB_HEAD_DIM),
                     c_st.reshape(bs, ts, KV_LORA), r_st.reshape(bs, ts, MLA_ROPE), _conv_from_chunks(conv_s)))
    stack = lambda sts, k: jnp.stack([s[k] for s in sts])
    heads_last = lambda a: a.reshape(depth, bp, N_HEADS, SB_HEAD_DIM, tp).transpose(0, 1, 4, 2, 3)
    return ((xp.reshape(bp, tp, D_MODEL), xs.reshape(bs, ts, D_MODEL), heads_last(kv_p[0]), heads_last(kv_p[1]))
            + tuple(stack(st_p, k) for k in range(3)) + tuple(stack(st_s, k) for k in range(5)))
```

```python
import functools
import math

import jax
import jax.numpy as jnp
import numpy as np
from jax import lax
from jax.experimental import pallas as pl
from jax.experimental.pallas import tpu as pltpu

D_MODEL = 1024
N_HEADS = 8
SB_HEAD_DIM = 64
SB_W = N_HEADS * SB_HEAD_DIM
MLA_NOPE = 64
MLA_ROPE = 32
ROPE_HALF = MLA_ROPE // 2
KV_LORA = 128
MLA_V = 64
D_FF = 2816
CONV_W = 3
CHUNK = 64
D_PLE = 256
ROPE_THETA = 10000.0
EPS = 1e-6
SB_SCALE = SB_HEAD_DIM ** -0.5
MLA_SCALE = (MLA_NOPE + MLA_ROPE) ** -0.5
LOG2E = math.log2(math.e)

LANES = 128
SUBLANES = 8
MXU_DIM = 256
VMEM_LIMIT_BYTES = 56 * 1024 * 1024

BLK = MXU_DIM
TOK_TILE = 2 * MXU_DIM
MIX_TILE = 4 * MXU_DIM
FF_CHUNK = MXU_DIM
N_FF_CHUNKS = D_FF // FF_CHUNK
FF_DOWN_GROUP = 4
SB_STAGE_HEADS = 8
QM_W = 2 * LANES
DEC_KEYS = 2048
MLA_DEC_KEYS = 512

BF16 = jnp.bfloat16
F32 = jnp.float32


def _dot(a, b):
    return jnp.dot(a, b, preferred_element_type=F32)


def _dot_nt(a, b):
    return lax.dot_general(a, b, (((1,), (1,)), ((), ())), preferred_element_type=F32)


def _rmsnorm(x, g):
    ms = jnp.mean(x * x, axis=-1, keepdims=True)
    return x * lax.rsqrt(ms + EPS) * g


def _sigmoid(x):
    return 1.0 / (1.0 + jnp.exp(-x))


def _softplus2(z2):
    return jnp.maximum(z2, 0.0) + jnp.log2(1.0 + jnp.exp2(-jnp.abs(z2)))


def _gelu_tanh(x):
    c = math.sqrt(2.0 / math.pi)
    return 0.5 * x * (1.0 + jnp.tanh(c * (x + 0.044715 * (x * x * x))))


def _params(*sem):
    return pltpu.CompilerParams(dimension_semantics=sem, vmem_limit_bytes=VMEM_LIMIT_BYTES)


def _const_spec(shape):
    return pl.BlockSpec(shape, lambda *_: (0,) * len(shape), pipeline_mode=pl.Buffered(1))


def _proj_kernel(x_ref, g_ref, cos_ref, sin_ref, wqkv_ref, wkvt_ref, wqn_ref, wqr_ref, wckv_ref, wkx_ref,
                 gain_ref, wa_ref, wb_ref, eye_ref, *refs, transposed, n_alias):
    out_refs = refs[n_alias:]
    if transposed:
        (q_ref, kbf_ref, ktst_ref, vtst_ref, vt_ref, qm_ref, cm_ref, ct_ref, cst_ref, rst_ref) = out_refs
    else:
        (q_ref, kst_ref, vst_ref, qm_ref, cm_ref, cst_ref, rst_ref) = out_refs
    hb = _rmsnorm(x_ref[...], g_ref[...]).astype(BF16)
    nsub = x_ref.shape[0] // BLK

    if transposed:
        qk = _dot(hb, wqkv_ref[:, :2 * SB_W])
        q_ref[...] = (qk[:, :SB_W] * SB_SCALE).astype(BF16)
        kb = qk[:, SB_W:].astype(BF16)
        even = lax.broadcasted_iota(jnp.int32, kb.shape, 1) % LANES < SB_HEAD_DIM
        kbf_ref[0] = jnp.where(even, kb, jnp.zeros_like(kb))
        kbf_ref[1] = jnp.where(even, jnp.zeros_like(kb), kb)
        kt = _dot_nt(wkvt_ref[:SB_W, :], hb)
        vt = _dot_nt(wkvt_ref[SB_W:, :], hb)
        ktst_ref[0, 0] = kt
        vtst_ref[0, 0] = vt
        for s in range(nsub):
            vt_ref[0, s] = vt[:, s * BLK:(s + 1) * BLK].astype(BF16)
    else:
        qkv = _dot(hb, wqkv_ref[...])
        q_ref[...] = (qkv[:, :SB_W] * SB_SCALE).astype(BF16)
        kst_ref[...] = qkv[:, SB_W:2 * SB_W]
        vst_ref[...] = qkv[:, 2 * SB_W:]

    cos = cos_ref[...]
    sin = sin_ref[...]
    qn = _dot(hb, wqn_ref[...]).astype(BF16)
    qr = _dot(hb, wqr_ref[...])
    x1, x2 = qr[:, :LANES], qr[:, LANES:]
    r12 = jnp.concatenate([x1 * cos - x2 * sin, x1 * sin + x2 * cos], axis=1).astype(BF16)
    for h in range(N_HEADS):
        p = h // 2
        qm_ref[h] = (_dot(qn[:, p * LANES:(p + 1) * LANES], wa_ref[h]) + _dot(r12, wb_ref[h])).astype(BF16)

    ckv = _rmsnorm(_dot(hb, wckv_ref[...]), gain_ref[...])
    cst_ref[...] = ckv
    kx = _dot(hb, wkx_ref[...])
    k1, k2 = kx[:, :LANES], kx[:, LANES:]
    kr1 = k1 * cos - k2 * sin
    kr2 = k1 * sin + k2 * cos
    lane = lax.broadcasted_iota(jnp.int32, kr1.shape, 1)
    kr = jnp.where(lane < ROPE_HALF, kr1, pltpu.roll(kr2, ROPE_HALF, axis=1))
    rst_ref[...] = kr[:, :MLA_ROPE]
    ckv_b = ckv.astype(BF16)
    cm_ref[...] = jnp.concatenate([ckv_b, kr.astype(BF16)], axis=1)
    if transposed:
        for s in range(nsub):
            ct_ref[0, s] = _dot_nt(eye_ref[...], ckv_b[s * BLK:(s + 1) * BLK, :]).astype(BF16)


def _proj(x2d, seq_len, tabs, lw, *, kv_states=None):
    n = x2d.shape[0]
    tm = TOK_TILE
    nt = n // tm
    cos, sin = tabs
    tab_blocks = cos.shape[0] // tm
    transposed = kv_states is not None
    tok = lambda w: pl.BlockSpec((tm, w), lambda i: (i, 0))
    tab = pl.BlockSpec((tm, LANES), lambda i: (i % tab_blocks, 0))
    in_specs = [tok(D_MODEL), _const_spec((1, D_MODEL)), tab, tab,
                _const_spec(lw["w_qkv"].shape), _const_spec(lw["w_kvt"].shape),
                _const_spec(lw["w_qn"].shape), _const_spec(lw["w_qr"].shape),
                _const_spec(lw["w_ckv"].shape), _const_spec(lw["w_kx"].shape), _const_spec((1, KV_LORA)),
                _const_spec(lw["wa"].shape), _const_spec(lw["wb"].shape), _const_spec((LANES, LANES))]
    args = [x2d, lw["g_pre_mix"], cos, sin, lw["w_qkv"], lw["w_kvt"], lw["w_qn"], lw["w_qr"], lw["w_ckv"],
            lw["w_kx"], lw["ckv_gain"], lw["wa"], lw["wb"], lw["eye"]]
    aliases = {}
    if transposed:
        layer, depth, prev = kv_states
        assert seq_len % tm == 0
        batch = n // seq_len
        nb = seq_len // tm
        sub = tm // BLK
        state = jax.ShapeDtypeStruct((depth, batch, SB_W, seq_len), F32)
        state_spec = pl.BlockSpec((1, 1, SB_W, tm), lambda i: (layer, i // nb, 0, i % nb))
        blocks = lambda w: (jax.ShapeDtypeStruct((batch, nb * sub, w, BLK), BF16),
                            pl.BlockSpec((1, sub, w, BLK), lambda i: (i // nb, i % nb, 0, 0)))
        outs = [(jax.ShapeDtypeStruct((n, SB_W), BF16), tok(SB_W)),
                (jax.ShapeDtypeStruct((2, n, SB_W), BF16),
                 pl.BlockSpec((2, tm, SB_W), lambda i: (0, i, 0))),
                (state, state_spec), (state, state_spec), blocks(SB_W)]
        if prev is not None:
            aliases = {len(args): 2, len(args) + 1: 3}
            args += list(prev)
            in_specs += [pl.BlockSpec(memory_space=pl.ANY)] * 2
    else:
        outs = [(jax.ShapeDtypeStruct((n, SB_W), BF16), tok(SB_W)),
                (jax.ShapeDtypeStruct((n, SB_W), F32), tok(SB_W)), (jax.ShapeDtypeStruct((n, SB_W), F32), tok(SB_W))]
    outs += [(jax.ShapeDtypeStruct((N_HEADS, n, QM_W), BF16), pl.BlockSpec((N_HEADS, tm, QM_W), lambda i: (0, i, 0))),
             (jax.ShapeDtypeStruct((n, QM_W), BF16), tok(QM_W))]
    if transposed:
        outs.append(blocks(KV_LORA))
    outs += [(jax.ShapeDtypeStruct((n, KV_LORA), F32), tok(KV_LORA)),
             (jax.ShapeDtypeStruct((n, MLA_ROPE), F32), tok(MLA_ROPE))]
    return pl.pallas_call(
        functools.partial(_proj_kernel, transposed=transposed, n_alias=len(aliases)),
        grid=(nt,), in_specs=in_specs, out_specs=[o[1] for o in outs], out_shape=[o[0] for o in outs],
        input_output_aliases=aliases,
        compiler_params=_params("parallel"), name="proj_t" if transposed else "proj",
    )(*args)


def _sb_scores(q_ref, k_ref, qrows, j, z_out):
    krows = pl.ds(pl.multiple_of(j * BLK, BLK), BLK)
    for h in range(N_HEADS):
        p, hh = h // 2, h % 2
        cols = slice(p * LANES, (p + 1) * LANES)
        kh = k_ref[hh, 0, krows, cols]
        z_out[h] = _dot_nt(kh, q_ref[0, qrows, cols]) * LOG2E


def _sb_block(z_in, vt_ref, j, acc_ref, car_ref, umat, vis):
    for h0 in range(0, N_HEADS, SB_STAGE_HEADS):
        heads = range(h0, h0 + SB_STAGE_HEADS)
        mms = {}
        for h in heads:
            sp = _softplus2(z_in[h])
            if vis is not None:
                sp = jnp.where(vis, sp, 0.0)
            mms[h] = _dot(umat, sp.astype(BF16))
        for h in heads:
            car = car_ref[h, 0:1, :]
            a = jnp.exp2(z_in[h] + mms[h] + car)
            if vis is not None:
                a = jnp.where(vis, a, 0.0)
            rows = slice(h * SB_HEAD_DIM, (h + 1) * SB_HEAD_DIM)
            acc_ref[rows, :] += _dot(vt_ref[0, j, rows, :], a.astype(BF16))
            car_ref[h, 0:1, :] = car + mms[h][0:1, :]


def _sb_kernel(q_ref, k_ref, vt_ref, o_ref, acc_ref, car_ref, z0_ref, z1_ref, *, nblk):
    kidx = lax.broadcasted_iota(jnp.int32, (BLK, BLK), 0)
    qidx = lax.broadcasted_iota(jnp.int32, (BLK, BLK), 1)
    umat = jnp.where(qidx >= kidx, -1.0, 0.0).astype(BF16)
    vis = kidx < qidx

    def q_block(i, carry):
        acc_ref[...] = jnp.zeros_like(acc_ref)
        car_ref[...] = jnp.zeros_like(car_ref)
        qrows = pl.ds(pl.multiple_of(i * BLK, BLK), BLK)

        def step(j, z_cur, z_next, mask):
            _sb_scores(q_ref, k_ref, qrows, jnp.maximum(j - 1, 0), z_next)
            _sb_block(z_cur, vt_ref, j, acc_ref, car_ref, umat, mask)

        _sb_scores(q_ref, k_ref, qrows, i, z0_ref)
        step(i, z0_ref, z1_ref, vis)

        def k_pair(m, c):
            j = i - 1 - 2 * m
            step(j, z1_ref, z0_ref, None)
            step(j - 1, z0_ref, z1_ref, None)
            return c

        lax.fori_loop(0, i // 2, k_pair, 0)

        @pl.when(i % 2 == 1)
        def _():
            step(0, z1_ref, z0_ref, None)

        o_ref[0, qrows, :] = acc_ref[...].T.astype(BF16)
        return carry

    lax.fori_loop(0, nblk, q_block, 0)


def _sb_prompt(q, kbf, vt, batch, seq_len):
    nblk = seq_len // BLK
    q3 = q.reshape(batch, seq_len, SB_W)
    k4 = kbf.reshape(2, batch, seq_len, SB_W)
    row = pl.BlockSpec((1, seq_len, SB_W), lambda b: (b, 0, 0))
    out = pl.pallas_call(
        functools.partial(_sb_kernel, nblk=nblk),
        grid=(batch,),
        in_specs=[row, pl.BlockSpec((2, 1, seq_len, SB_W), lambda b: (0, b, 0, 0)),
                  pl.BlockSpec((1, nblk, SB_W, BLK), lambda b: (b, 0, 0, 0))],
        out_specs=row,
        out_shape=jax.ShapeDtypeStruct((batch, seq_len, SB_W), BF16),
        scratch_shapes=[pltpu.VMEM((SB_W, BLK), F32), pltpu.VMEM((N_HEADS, SUBLANES, BLK), F32),
                        pltpu.VMEM((N_HEADS, BLK, BLK), F32), pltpu.VMEM((N_HEADS, BLK, BLK), F32)],
        compiler_params=_params("parallel"), name="sb_prompt",
    )(q3, k4, vt)
    return out.reshape(batch * seq_len, SB_W)


def _mla_scores(q_ref, c_ref, qrows, j, s_out):
    c = c_ref[pl.ds(pl.multiple_of(j * BLK, BLK), BLK), :]
    for h in range(N_HEADS):
        s_out[h] = _dot_nt(c, q_ref[h, qrows, :])


def _mla_block(s_in, ct_ref, j, acc_ref, st_ref, vis):
    ct = ct_ref[0, j]
    for h in range(N_HEADS):
        s = s_in[h]
        if vis is not None:
            s = jnp.where(vis, s, -jnp.inf)
        m_old = st_ref[h, 0:1, :]
        m_new = jnp.maximum(m_old, jnp.max(s, axis=0, keepdims=True))
        p = jnp.exp2((s - m_new) * (MLA_SCALE * LOG2E))
        alpha = jnp.exp2((m_old - m_new) * (MLA_SCALE * LOG2E))
        st_ref[h, 1:2, :] = alpha * st_ref[h, 1:2, :] + jnp.sum(p, axis=0, keepdims=True)
        st_ref[h, 0:1, :] = m_new
        acc_ref[h] = alpha * acc_ref[h] + _dot(ct, p.astype(BF16))


def _mla_kernel(q_ref, c_ref, ct_ref, o_ref, acc_ref, st_ref, s0_ref, s1_ref, *, nblk):
    kidx = lax.broadcasted_iota(jnp.int32, (BLK, BLK), 0)
    qidx = lax.broadcasted_iota(jnp.int32, (BLK, BLK), 1)
    vis = (kidx // CHUNK) <= (qidx // CHUNK)

    def q_block(i, carry):
        acc_ref[...] = jnp.zeros_like(acc_ref)
        for h in range(N_HEADS):
            st_ref[h, 0:1, :] = jnp.full((1, BLK), -jnp.inf, F32)
            st_ref[h, 1:2, :] = jnp.zeros((1, BLK), F32)
        rows = pl.ds(pl.multiple_of(i * BLK, BLK), BLK)
        def step(j, j_next, s_cur, s_next, mask):
            _mla_scores(q_ref, c_ref, rows, j_next, s_next)
            _mla_block(s_cur, ct_ref, j, acc_ref, st_ref, mask)

        nxt = lambda j: jnp.minimum(j + 1, nblk - 1)
        _mla_scores(q_ref, c_ref, rows, i, s0_ref)
        step(i, 0, s0_ref, s1_ref, vis)

        def k_pair(m, c):
            j = 2 * m
            step(j, nxt(j), s1_ref, s0_ref, None)
            step(j + 1, nxt(j + 1), s0_ref, s1_ref, None)
            return c

        lax.fori_loop(0, i // 2, k_pair, 0)

        @pl.when(i % 2 == 1)
        def _():
            step(i - 1, nxt(i - 1), s1_ref, s0_ref, None)

        for h in range(N_HEADS):
            o = acc_ref[h] * (1.0 / st_ref[h, 1:2, :])
            o_ref[rows, h * KV_LORA:(h + 1) * KV_LORA] = o.T.astype(BF16)
        return carry

    lax.fori_loop(0, nblk, q_block, 0)


def _mla_prompt(qm, cm, ct, batch, seq_len):
    nblk = seq_len // BLK
    return pl.pallas_call(
        functools.partial(_mla_kernel, nblk=nblk),
        grid=(batch,),
        in_specs=[pl.BlockSpec((N_HEADS, seq_len, QM_W), lambda b: (0, b, 0)),
                  pl.BlockSpec((seq_len, QM_W), lambda b: (b, 0)),
                  pl.BlockSpec((1, nblk, KV_LORA, BLK), lambda b: (b, 0, 0, 0))],
        out_specs=pl.BlockSpec((seq_len, N_HEADS * KV_LORA), lambda b: (b, 0)),
        out_shape=jax.ShapeDtypeStruct((batch * seq_len, N_HEADS * KV_LORA), BF16),
        scratch_shapes=[pltpu.VMEM((N_HEADS, KV_LORA, BLK), F32), pltpu.VMEM((N_HEADS, SUBLANES, BLK), F32),
                        pltpu.VMEM((N_HEADS, BLK, BLK), F32), pltpu.VMEM((N_HEADS, BLK, BLK), F32)],
        compiler_params=_params("parallel"), name="mla_prompt",
    )(qm, cm, ct)


def _sbdec_blocks(score, av, nblocks, t, acc_ref, car_ref, wn, vis):
    heads = range(N_HEADS)
    zs = [jnp.concatenate([score(b, h) for h in heads], axis=0) * LOG2E
          for b in range(nblocks)]
    mms = []
    for b in range(nblocks):
        sp = _softplus2(zs[b])
        if vis is not None:
            sp = jnp.where(vis, sp, 0.0)
        mms.append(_dot(sp.astype(BF16), wn))
    car = car_ref[:, 0:1]
    for b in range(nblocks):
        a = jnp.exp2(zs[b] + mms[b] + car)
        if vis is not None:
            a = jnp.where(vis, a, 0.0)
        a = a.astype(BF16)
        for h in heads:
            acc_ref[h] += av(b, h, a[h * t:(h + 1) * t, :])
        car = car + mms[b][:, 0:1]
    car_ref[...] = jnp.broadcast_to(car, car_ref.shape)


def _sbdec_kernel(q_ref, kn_ref, vn_ref, ck_ref, cv_ref, o_ref, qh_ref, acc_ref, car_ref, *, t_new):
    j = pl.program_id(1)
    hcols = lambda h: slice(h * SB_HEAD_DIM, (h + 1) * SB_HEAD_DIM)

    @pl.when(j == 0)
    def _():
        acc_ref[...] = jnp.zeros_like(acc_ref)
        car_ref[...] = jnp.zeros_like(car_ref)
        for h in range(N_HEADS):
            qh_ref[h] = q_ref[:, hcols(h)]
        kp = lax.broadcasted_iota(jnp.int32, (t_new, t_new), 0)
        kc = lax.broadcasted_iota(jnp.int32, (t_new, t_new), 1)
        wn_new = jnp.where(kp >= kc, -1.0, 0.0).astype(BF16)
        qi = lax.rem(lax.broadcasted_iota(jnp.int32, (N_HEADS * t_new, t_new), 0), t_new)
        ki = lax.broadcasted_iota(jnp.int32, (N_HEADS * t_new, t_new), 1)
        _sbdec_blocks(lambda b, h: _dot_nt(qh_ref[h], kn_ref[:, hcols(h)].astype(BF16)),
                      lambda b, h, a: _dot(a, vn_ref[:, hcols(h)].astype(BF16)),
                      1, t_new, acc_ref, car_ref, wn_new, ki < qi)

    kp = lax.broadcasted_iota(jnp.int32, (BLK, BLK), 0)
    kc = lax.broadcasted_iota(jnp.int32, (BLK, BLK), 1)
    wn = jnp.where(kp >= kc, -1.0, 0.0).astype(BF16)
    nsub = DEC_KEYS // BLK
    keys = lambda b: slice((nsub - 1 - b) * BLK, (nsub - b) * BLK)
    _sbdec_blocks(lambda b, h: _dot(qh_ref[h], ck_ref[0, 0, h, :, keys(b)].astype(BF16)),
                  lambda b, h, a: _dot_nt(a, cv_ref[0, 0, h, :, keys(b)].astype(BF16)),
                  nsub, t_new, acc_ref, car_ref, wn, None)

    @pl.when(j == pl.num_programs(1) - 1)
    def _():
        for h in range(N_HEADS):
            o_ref[:, hcols(h)] = acc_ref[h].astype(BF16)


def _sb_sample(q, k_new, v_new, cache_k, cache_v, layer, batch, t_new):
    past = cache_k.shape[4]
    assert past % DEC_KEYS == 0
    nkb = past // DEC_KEYS
    tok = lambda: pl.BlockSpec((t_new, SB_W), lambda b, j: (b, 0))
    cache = lambda: pl.BlockSpec((1, 1, N_HEADS, SB_HEAD_DIM, DEC_KEYS), lambda b, j: (layer, b, 0, 0, nkb - 1 - j))
    return pl.pallas_call(
        functools.partial(_sbdec_kernel, t_new=t_new),
        grid=(batch, nkb),
        in_specs=[tok(), tok(), tok(), cache(), cache()],
        out_specs=tok(),
        out_shape=jax.ShapeDtypeStruct((batch * t_new, SB_W), BF16),
        scratch_shapes=[pltpu.VMEM((N_HEADS, t_new, SB_HEAD_DIM), BF16),
                        pltpu.VMEM((N_HEADS, t_new, SB_HEAD_DIM), F32),
                        pltpu.VMEM((N_HEADS * t_new, LANES), F32)],
        compiler_params=_params("parallel", "arbitrary"), name="sb_sample",
    )(q, k_new, v_new, cache_k, cache_v)


def _mladec_kernel(q_ref, cn_ref, ck_ref, kr_ref, o_ref, *, t_new, past, new_visible):
    q = q_ref[...].reshape(N_HEADS * t_new, QM_W)
    q_lat = q[:, :KV_LORA]
    q_rope = q[:, KV_LORA:KV_LORA + MLA_ROPE]
    c2 = MLA_SCALE * LOG2E

    cn = cn_ref[...]
    s = _dot_nt(q, cn)
    if not new_visible:
        qi = past + lax.broadcasted_iota(jnp.int32, s.shape, 0) % t_new
        ki = past + lax.broadcasted_iota(jnp.int32, s.shape, 1)
        s = jnp.where((ki // CHUNK) <= (qi // CHUNK), s, -jnp.inf)
    chunks = [slice(c * MLA_DEC_KEYS, (c + 1) * MLA_DEC_KEYS) for c in range(past // MLA_DEC_KEYS)]
    cks = [ck_ref[0, 0, rows, :].astype(BF16) for rows in chunks]
    ss = [_dot_nt(q_lat, ck) + _dot(q_rope, kr_ref[0, 0, :, rows].astype(BF16))
          for ck, rows in zip(cks, chunks)]
    m = jnp.max(s, axis=1, keepdims=True)
    for sc in ss:
        m = jnp.maximum(m, jnp.max(sc, axis=1, keepdims=True))
    p = jnp.exp2((s - m) * c2)
    l = jnp.sum(p, axis=1, keepdims=True)
    acc = _dot(p.astype(BF16), cn[:, :KV_LORA])
    for sc, ck in zip(ss, cks):
        p = jnp.exp2((sc - m) * c2)
        l = l + jnp.sum(p, axis=1, keepdims=True)
        acc = acc + _dot(p.astype(BF16), ck)
    o = (acc * (1.0 / l)).astype(BF16)
    for h in range(N_HEADS):
        o_ref[:, h * KV_LORA:(h + 1) * KV_LORA] = o[h * t_new:(h + 1) * t_new, :]


def _mla_sample(qm, cm, cache_ckv, cache_kr, layer, batch, t_new):
    past = cache_ckv.shape[2]
    assert past % MLA_DEC_KEYS == 0 and past % CHUNK == 0
    q_pos = past + np.arange(t_new)
    new_visible = bool(((q_pos[None, :] // CHUNK) <= (q_pos[:, None] // CHUNK)).all())
    return pl.pallas_call(
        functools.partial(_mladec_kernel, t_new=t_new, past=past, new_visible=new_visible),
        grid=(batch,),
        in_specs=[pl.BlockSpec((N_HEADS, t_new, QM_W), lambda b: (0, b, 0)),
                  pl.BlockSpec((t_new, QM_W), lambda b: (b, 0)),
                  pl.BlockSpec((1, 1, past, KV_LORA), lambda b: (layer, b, 0, 0)),
                  pl.BlockSpec((1, 1, MLA_ROPE, past), lambda b: (layer, b, 0, 0))],
        out_specs=pl.BlockSpec((t_new, N_HEADS * KV_LORA), lambda b: (b, 0)),
        out_shape=jax.ShapeDtypeStruct((batch * t_new, N_HEADS * KV_LORA), BF16),
        compiler_params=_params("parallel"), name="mla_sample",
    )(qm, cm, cache_ckv, cache_kr)


def _mix_kernel(x_ref, gpre_ref, osb_ref, olat_ref, wg_ref, wpsb_ref, wuv_ref, wpmla_ref, wout_ref,
                gpost_ref, o_ref):
    x = x_ref[...]
    hb = _rmsnorm(x, gpre_ref[...]).astype(BF16)
    gates = _dot(hb, wg_ref[...])
    y_a = _dot(osb_ref[...], wpsb_ref[...])
    o_mla = _dot(olat_ref[...], wuv_ref[...]).astype(BF16)
    y_b = _dot(o_mla, wpmla_ref[...])
    mixin = _sigmoid(gates[:, :D_MODEL]) * y_a + _sigmoid(gates[:, D_MODEL:]) * y_b
    mix = _dot(mixin.astype(BF16), wout_ref[...])
    o_ref[...] = x + _rmsnorm(mix, gpost_ref[...])


def _mix(x2d, o_sb, o_lat, lw):
    n = x2d.shape[0]
    tm = MIX_TILE
    assert n % tm == 0
    tok = lambda w: pl.BlockSpec((tm, w), lambda i: (i, 0))
    return pl.pallas_call(
        _mix_kernel, grid=(n // tm,),
        in_specs=[tok(D_MODEL), _const_spec((1, D_MODEL)), tok(SB_W), tok(N_HEADS * KV_LORA),
                  _const_spec(lw["w_gates"].shape), _const_spec(lw["w_proj_sb"].shape),
                  _const_spec(lw["w_uv_bd"].shape), _const_spec(lw["w_proj_mla"].shape),
                  _const_spec(lw["w_out"].shape), _const_spec((1, D_MODEL))],
        out_specs=tok(D_MODEL), out_shape=jax.ShapeDtypeStruct((n, D_MODEL), F32),
        compiler_params=_params("parallel"), name="mix",
    )(x2d, lw["g_pre_mix"], o_sb, o_lat, lw["w_gates"], lw["w_proj_sb"], lw["w_uv_bd"], lw["w_proj_mla"],
      lw["w_out"], lw["g_post_mix"])


def _ffn_kernel(x_ref, gpre_ref, wup_ref, cw_ref, past_ref, wdown_ref, gpost_ref, wpg_ref, p_ref, wpp_ref,
                o_ref, conv_ref, prev_scr, *, nseq, rows):
    t = pl.program_id(1)
    lead = SUBLANES - (CONV_W - 1)

    @pl.when(t == 0)
    def _():
        for c in range(2 * N_FF_CHUNKS):
            for s in range(nseq):
                prev_scr[c, s * SUBLANES + lead:(s + 1) * SUBLANES, :] = past_ref[s, c]

    x = x_ref[...]
    hb = _rmsnorm(x, gpre_ref[...]).astype(BF16)
    row8 = lax.broadcasted_iota(jnp.int32, (SUBLANES, FF_CHUNK), 0)

    def up(c):
        return [_dot(hb, wup_ref[:, cc * FF_CHUNK:(cc + 1) * FF_CHUNK]) for cc in (c, N_FF_CHUNKS + c)]

    def conv(cc, u):
        cw = cw_ref[cc]
        out = []
        for s in range(nseq):
            us = u[s * rows:(s + 1) * rows, :]
            prev = prev_scr[cc, s * SUBLANES:(s + 1) * SUBLANES, :]
            uc = cw[3:4, :]
            for i in range(CONV_W):
                back = CONV_W - 1 - i
                if back == 0:
                    shifted = us
                else:
                    r = pltpu.roll(us, back, axis=0)
                    head = jnp.where(row8 < back, pltpu.roll(prev, back, axis=0), r[:SUBLANES, :])
                    shifted = jnp.concatenate([head, r[SUBLANES:, :]], axis=0)
                uc = uc + shifted * cw[i:i + 1, :]
            conv_ref[s, cc] = us[rows - (CONV_W - 1):, :]
            prev_scr[cc, s * SUBLANES:(s + 1) * SUBLANES, :] = us[rows - SUBLANES:, :]
            out.append(uc)
        return out[0] if nseq == 1 else jnp.concatenate(out, axis=0)

    u_next = up(0)
    f = None
    acts = []
    for c in range(N_FF_CHUNKS):
        u_gate, u_val = u_next
        if c + 1 < N_FF_CHUNKS:
            u_next = up(c + 1)
        acts.append((_gelu_tanh(conv(c, u_gate)) * conv(N_FF_CHUNKS + c, u_val)).astype(BF16))
        if len(acts) == FF_DOWN_GROUP or c + 1 == N_FF_CHUNKS:
            k1 = (c + 1) * FF_CHUNK
            k0 = k1 - len(acts) * FF_CHUNK
            part = _dot(jnp.concatenate(acts, axis=1), wdown_ref[k0:k1, :])
            f = part if f is None else f + part
            acts = []
    x2 = x + _rmsnorm(f, gpost_ref[...])
    ple = _sigmoid(_dot(x2.astype(BF16), wpg_ref[...])) * _dot(p_ref[0].astype(BF16), wpp_ref[...])
    o_ref[...] = x2 + ple


def _ffn(x2d, p3d, layer, conv_past, batch, seq_len, lw):
    tm = TOK_TILE
    rows = min(tm, seq_len)
    nseq = tm // rows
    nt = seq_len // rows
    assert batch % nseq == 0 and (nseq == 1 or nt == 1)
    nch = 2 * N_FF_CHUNKS
    tok = lambda w: pl.BlockSpec((tm, w), lambda b, t: (b * nt + t, 0))
    ple = pl.BlockSpec((1, tm, D_PLE), lambda b, t: (layer, b * nt + t, 0))
    state = pl.BlockSpec((nseq, nch, CONV_W - 1, FF_CHUNK), lambda b, t: (b, 0, 0, 0))
    return pl.pallas_call(
        functools.partial(_ffn_kernel, nseq=nseq, rows=rows),
        grid=(batch // nseq, nt),
        in_specs=[tok(D_MODEL), _const_spec((1, D_MODEL)), _const_spec(lw["w_up"].shape),
                  _const_spec(lw["conv"].shape), state, _const_spec(lw["w_down"].shape),
                  _const_spec((1, D_MODEL)), _const_spec(lw["w_ple_gate"].shape), ple,
                  _const_spec(lw["w_ple_proj"].shape)],
        out_specs=[tok(D_MODEL), state],
        out_shape=[jax.ShapeDtypeStruct(x2d.shape, F32), jax.ShapeDtypeStruct(conv_past.shape, F32)],
        scratch_shapes=[pltpu.VMEM((nch, nseq * SUBLANES, FF_CHUNK), F32)],
        compiler_params=_params("arbitrary", "arbitrary"), name="ffn",
    )(x2d, lw["g_pre_ffn"], lw["w_up"], lw["conv"], conv_past, lw["w_down"], lw["g_post_ffn"],
      lw["w_ple_gate"], p3d, lw["w_ple_proj"])


def _prep_layer(w_in, ckv_gain, w_uk, w_uv, w_proj_sb, w_proj_mla, w_out, g_pre_mix, g_post_mix, g_pre_ffn,
                g_post_ffn, w_up, conv_w, conv_b, w_down, w_ple_gate, w_ple_proj):
    o_qm = 3 * SB_W
    o_ckv = o_qm + N_HEADS * (MLA_NOPE + MLA_ROPE)
    o_kr = o_ckv + KV_LORA
    o_g = o_kr + MLA_ROPE
    wqm = w_in[:, o_qm:o_ckv].reshape(D_MODEL, N_HEADS, MLA_NOPE + MLA_ROPE)
    w_qn = wqm[:, :, :MLA_NOPE].reshape(D_MODEL, N_HEADS * MLA_NOPE)
    w_qr = jnp.concatenate([wqm[:, :, MLA_NOPE:MLA_NOPE + ROPE_HALF].reshape(D_MODEL, LANES),
                            wqm[:, :, MLA_NOPE + ROPE_HALF:].reshape(D_MODEL, LANES)], axis=1)
    w_kr = w_in[:, o_kr:o_g]
    w_kx = jnp.zeros((D_MODEL, 2 * LANES), F32)
    w_kx = w_kx.at[:, :ROPE_HALF].set(w_kr[:, :ROPE_HALF]).at[:, LANES:LANES + ROPE_HALF].set(w_kr[:, ROPE_HALF:])
    wa = jnp.zeros((N_HEADS, LANES, QM_W), F32)
    wb = np.zeros((N_HEADS, 2 * LANES, QM_W), np.float32)
    for h in range(N_HEADS):
        r0 = (h % 2) * MLA_NOPE
        wa = wa.at[h, r0:r0 + MLA_NOPE, :KV_LORA].set(w_uk[:, h, :].T)
        for i in range(ROPE_HALF):
            wb[h, h * ROPE_HALF + i, KV_LORA + i] = 1.0
            wb[h, LANES + h * ROPE_HALF + i, KV_LORA + ROPE_HALF + i] = 1.0
    w_uv_bd = jnp.zeros((N_HEADS * KV_LORA, N_HEADS * MLA_V), F32)
    for h in range(N_HEADS):
        w_uv_bd = w_uv_bd.at[h * KV_LORA:(h + 1) * KV_LORA, h * MLA_V:(h + 1) * MLA_V].set(w_uv[:, h, :])
    nch = 2 * N_FF_CHUNKS
    conv = jnp.concatenate([conv_w, conv_b[None, :], jnp.zeros((SUBLANES - CONV_W - 1, 2 * D_FF), F32)], axis=0)
    row = lambda g: g.reshape(1, -1)
    return dict(
        w_qkv=w_in[:, :o_qm].astype(BF16), w_kvt=w_in[:, SB_W:o_qm].T.astype(BF16),
        w_qn=w_qn.astype(BF16), w_qr=w_qr.astype(BF16),
        w_ckv=w_in[:, o_ckv:o_kr].astype(BF16), w_kx=w_kx.astype(BF16), ckv_gain=row(ckv_gain),
        wa=wa.astype(BF16), wb=jnp.asarray(wb, BF16), eye=jnp.eye(LANES, dtype=BF16),
        w_gates=w_in[:, o_g:].astype(BF16), w_proj_sb=w_proj_sb.astype(BF16), w_uv_bd=w_uv_bd.astype(BF16),
        w_proj_mla=w_proj_mla.astype(BF16), w_out=w_out.astype(BF16),
        g_pre_mix=row(g_pre_mix), g_post_mix=row(g_post_mix), g_pre_ffn=row(g_pre_ffn), g_post_ffn=row(g_post_ffn),
        w_up=w_up.astype(BF16),
        conv=conv.reshape(SUBLANES, nch, FF_CHUNK).transpose(1, 0, 2),
        w_down=w_down.astype(BF16),
        w_ple_gate=w_ple_gate.astype(BF16), w_ple_proj=w_ple_proj.astype(BF16))


def _rope_tables(pos, rows):
    freqs = jnp.power(ROPE_THETA, -jnp.arange(ROPE_HALF, dtype=F32) / ROPE_HALF)
    ang = pos.astype(F32)[:, None] * freqs[None, :]
    reps = max(1, rows // pos.shape[0])
    tile = lambda a: jnp.tile(a, (reps, LANES // ROPE_HALF))
    return tile(jnp.cos(ang)), tile(jnp.sin(ang))


def _conv_to_chunks(state):
    b = state.shape[0]
    return state.reshape(b, CONV_W - 1, 2 * N_FF_CHUNKS, FF_CHUNK).transpose(0, 2, 1, 3)


def _conv_from_chunks(state):
    b = state.shape[0]
    return state.transpose(0, 2, 1, 3).reshape(b, CONV_W - 1, 2 * D_FF)


def kernel(x_prompt, x_sample, p_prompt, p_sample, cache_sb_k, cache_sb_v, cache_mla_ckv, cache_mla_krope,
           state_ffn_conv, w_in, ckv_gain, w_uk, w_uv, w_proj_sb, w_proj_mla, w_out, g_pre_mix, g_post_mix,
           g_pre_ffn, g_post_ffn, w_up, conv_w, conv_b, w_down, w_ple_gate, w_ple_proj):
    bp, tp, _ = x_prompt.shape
    bs, ts, _ = x_sample.shape
    depth = w_in.shape[0]
    past = cache_sb_k.shape[2]
    assert tp % TOK_TILE == 0 and (bs * ts) % TOK_TILE == 0 and TOK_TILE % ts == 0
    tabs_p = _rope_tables(jnp.arange(tp, dtype=jnp.int32), TOK_TILE)
    tabs_s = _rope_tables(past + jnp.arange(ts, dtype=jnp.int32), TOK_TILE)
    xp = x_prompt.reshape(bp * tp, D_MODEL)
    xs = x_sample.reshape(bs * ts, D_MODEL)
    pp = p_prompt.reshape(depth, bp * tp, D_PLE)
    ps = p_sample.reshape(depth, bs * ts, D_PLE)
    ck_t = cache_sb_k.transpose(0, 1, 3, 4, 2)
    cv_t = cache_sb_v.transpose(0, 1, 3, 4, 2)
    kr_t = cache_mla_krope.transpose(0, 1, 3, 2)
    zero_conv = jnp.zeros((bp, 2 * N_FF_CHUNKS, CONV_W - 1, FF_CHUNK), F32)
    st_p, st_s = [], []
    kv_p = None
    for i in range(depth):
        lw = _prep_layer(w_in[i], ckv_gain[i], w_uk[i], w_uv[i], w_proj_sb[i], w_proj_mla[i], w_out[i],
                         g_pre_mix[i], g_post_mix[i], g_pre_ffn[i], g_post_ffn[i], w_up[i], conv_w[i],
                         conv_b[i], w_down[i], w_ple_gate[i], w_ple_proj[i])
        q, k_bf, kt_all, vt_all, vt, qm, cm, ct, c_st, r_st = _proj(xp, tp, tabs_p, lw, kv_states=(i, depth, kv_p))
        kv_p = (kt_all, vt_all)
        o_sb = _sb_prompt(q, k_bf, vt, bp, tp)
        o_lat = _mla_prompt(qm, cm, ct, bp, tp)
        x1 = _mix(xp, o_sb, o_lat, lw)
        xp, conv_p = _ffn(x1, pp, i, zero_conv, bp, tp, lw)
        st_p.append((c_st.reshape(bp, tp, KV_LORA), r_st.reshape(bp, tp, MLA_ROPE), _conv_from_chunks(conv_p)))
        q, k_st, v_st, qm, cm, c_st, r_st = _proj(xs, ts, tabs_s, lw)
        o_sb = _sb_sample(q, k_st, v_st, ck_t, cv_t, i, bs, ts)
        o_lat = _mla_sample(qm, cm, cache_mla_ckv, kr_t, i, bs, ts)
        x1 = _mix(xs, o_sb, o_lat, lw)
        xs, conv_s = _ffn(x1, ps, i, _conv_to_chunks(state_ffn_conv[i]), bs, ts, lw)
        st_s.append((k_st.reshape(bs, ts, N_HEADS, SB_HEAD_DIM), v_st.reshape(bs, ts, N_HEADS, SB_HEAD_DIM),
                     c_st.reshape(bs, ts, KV_LORA), r_st.reshape(bs, ts, MLA_ROPE), _conv_from_chunks(conv_s)))
    stack = lambda sts, k: jnp.stack([s[k] for s in sts])
    heads_last = lambda a: a.reshape(depth, bp, N_HEADS, SB_HEAD_DIM, tp).transpose(0, 1, 4, 2, 3)
    return ((xp.reshape(bp, tp, D_MODEL), xs.reshape(bs, ts, D_MODEL), heads_last(kv_p[0]), heads_last(kv_p[1]))
            + tuple(stack(st_p, k) for k in range(3)) + tuple(stack(st_s, k) for k in range(5)))
```

```python
import functools
import math

import jax
import jax.numpy as jnp
import numpy as np
from jax import lax
from jax.experimental import pallas as pl
from jax.experimental.pallas import tpu as pltpu

D_MODEL = 1024
N_HEADS = 8
SB_HEAD_DIM = 64
SB_W = N_HEADS * SB_HEAD_DIM
MLA_NOPE = 64
MLA_ROPE = 32
ROPE_HALF = MLA_ROPE // 2
KV_LORA = 128
MLA_V = 64
D_FF = 2816
CONV_W = 3
CHUNK = 64
D_PLE = 256
ROPE_THETA = 10000.0
EPS = 1e-6
SB_SCALE = SB_HEAD_DIM ** -0.5
MLA_SCALE = (MLA_NOPE + MLA_ROPE) ** -0.5
LOG2E = math.log2(math.e)

LANES = 128
SUBLANES = 8
MXU_DIM = 256
VMEM_LIMIT_BYTES = 56 * 1024 * 1024

BLK = MXU_DIM
TOK_TILE = 2 * MXU_DIM
MIX_TILE = 4 * MXU_DIM
FF_CHUNK = MXU_DIM
N_FF_CHUNKS = D_FF // FF_CHUNK
FF_DOWN_GROUP = 4
QM_W = 2 * LANES
DEC_KEYS = 2048
MLA_DEC_KEYS = 512

BF16 = jnp.bfloat16
F32 = jnp.float32


def _dot(a, b):
    return jnp.dot(a, b, preferred_element_type=F32)


def _dot_nt(a, b):
    return lax.dot_general(a, b, (((1,), (1,)), ((), ())), preferred_element_type=F32)


def _rmsnorm(x, g):
    ms = jnp.mean(x * x, axis=-1, keepdims=True)
    return x * lax.rsqrt(ms + EPS) * g


def _sigmoid(x):
    return 1.0 / (1.0 + jnp.exp(-x))


def _softplus2(z2):
    return jnp.maximum(z2, 0.0) + jnp.log2(1.0 + jnp.exp2(-jnp.abs(z2)))


def _gelu_tanh(x):
    c = math.sqrt(2.0 / math.pi)
    return 0.5 * x * (1.0 + jnp.tanh(c * (x + 0.044715 * (x * x * x))))


def _params(*sem):
    return pltpu.CompilerParams(dimension_semantics=sem, vmem_limit_bytes=VMEM_LIMIT_BYTES)


def _const_spec(shape):
    return pl.BlockSpec(shape, lambda *_: (0,) * len(shape), pipeline_mode=pl.Buffered(1))


def _proj_kernel(x_ref, g_ref, cos_ref, sin_ref, wqkv_ref, wkvt_ref, wqn_ref, wqr_ref, wckv_ref, wkx_ref,
                 gain_ref, wa_ref, wb_ref, eye_ref, *refs, transposed, n_alias):
    out_refs = refs[n_alias:]
    if transposed:
        (q_ref, kbf_ref, ktst_ref, vtst_ref, vt_ref, qm_ref, cm_ref, ct_ref, cst_ref, rst_ref) = out_refs
    else:
        (q_ref, kst_ref, vst_ref, qm_ref, cm_ref, cst_ref, rst_ref) = out_refs
    hb = _rmsnorm(x_ref[...], g_ref[...]).astype(BF16)
    nsub = x_ref.shape[0] // BLK

    if transposed:
        qk = _dot(hb, wqkv_ref[:, :2 * SB_W])
        q_ref[...] = (qk[:, :SB_W] * SB_SCALE).astype(BF16)
        kb = qk[:, SB_W:].astype(BF16)
        even = lax.broadcasted_iota(jnp.int32, kb.shape, 1) % LANES < SB_HEAD_DIM
        kbf_ref[0] = jnp.where(even, kb, jnp.zeros_like(kb))
        kbf_ref[1] = jnp.where(even, jnp.zeros_like(kb), kb)
        kt = _dot_nt(wkvt_ref[:SB_W, :], hb)
        vt = _dot_nt(wkvt_ref[SB_W:, :], hb)
        ktst_ref[0, 0] = kt
        vtst_ref[0, 0] = vt
        for s in range(nsub):
            vt_ref[0, s] = vt[:, s * BLK:(s + 1) * BLK].astype(BF16)
    else:
        qkv = _dot(hb, wqkv_ref[...])
        q_ref[...] = (qkv[:, :SB_W] * SB_SCALE).astype(BF16)
        kst_ref[...] = qkv[:, SB_W:2 * SB_W]
        vst_ref[...] = qkv[:, 2 * SB_W:]

    cos = cos_ref[...]
    sin = sin_ref[...]
    qn = _dot(hb, wqn_ref[...]).astype(BF16)
    qr = _dot(hb, wqr_ref[...])
    x1, x2 = qr[:, :LANES], qr[:, LANES:]
    r12 = jnp.concatenate([x1 * cos - x2 * sin, x1 * sin + x2 * cos], axis=1).astype(BF16)
    for h in range(N_HEADS):
        p = h // 2
        qm_ref[h] = (_dot(qn[:, p * LANES:(p + 1) * LANES], wa_ref[h]) + _dot(r12, wb_ref[h])).astype(BF16)

    ckv = _rmsnorm(_dot(hb, wckv_ref[...]), gain_ref[...])
    cst_ref[...] = ckv
    kx = _dot(hb, wkx_ref[...])
    k1, k2 = kx[:, :LANES], kx[:, LANES:]
    kr1 = k1 * cos - k2 * sin
    kr2 = k1 * sin + k2 * cos
    lane = lax.broadcasted_iota(jnp.int32, kr1.shape, 1)
    kr = jnp.where(lane < ROPE_HALF, kr1, pltpu.roll(kr2, ROPE_HALF, axis=1))
    rst_ref[...] = kr[:, :MLA_ROPE]
    ckv_b = ckv.astype(BF16)
    cm_ref[...] = jnp.concatenate([ckv_b, kr.astype(BF16)], axis=1)
    if transposed:
        for s in range(nsub):
            ct_ref[0, s] = _dot_nt(eye_ref[...], ckv_b[s * BLK:(s + 1) * BLK, :]).astype(BF16)


def _proj(x2d, seq_len, tabs, lw, *, kv_states=None):
    n = x2d.shape[0]
    tm = TOK_TILE
    nt = n // tm
    cos, sin = tabs
    tab_blocks = cos.shape[0] // tm
    transposed = kv_states is not None
    tok = lambda w: pl.BlockSpec((tm, w), lambda i: (i, 0))
    tab = pl.BlockSpec((tm, LANES), lambda i: (i % tab_blocks, 0))
    in_specs = [tok(D_MODEL), _const_spec((1, D_MODEL)), tab, tab,
                _const_spec(lw["w_qkv"].shape), _const_spec(lw["w_kvt"].shape),
                _const_spec(lw["w_qn"].shape), _const_spec(lw["w_qr"].shape),
                _const_spec(lw["w_ckv"].shape), _const_spec(lw["w_kx"].shape), _const_spec((1, KV_LORA)),
                _const_spec(lw["wa"].shape), _const_spec(lw["wb"].shape), _const_spec((LANES, LANES))]
    args = [x2d, lw["g_pre_mix"], cos, sin, lw["w_qkv"], lw["w_kvt"], lw["w_qn"], lw["w_qr"], lw["w_ckv"],
            lw["w_kx"], lw["ckv_gain"], lw["wa"], lw["wb"], lw["eye"]]
    aliases = {}
    if transposed:
        layer, depth, prev = kv_states
        assert seq_len % tm == 0
        batch = n // seq_len
        nb = seq_len // tm
        sub = tm // BLK
        state = jax.ShapeDtypeStruct((depth, batch, SB_W, seq_len), F32)
        state_spec = pl.BlockSpec((1, 1, SB_W, tm), lambda i: (layer, i // nb, 0, i % nb))
        blocks = lambda w: (jax.ShapeDtypeStruct((batch, nb * sub, w, BLK), BF16),
                            pl.BlockSpec((1, sub, w, BLK), lambda i: (i // nb, i % nb, 0, 0)))
        outs = [(jax.ShapeDtypeStruct((n, SB_W), BF16), tok(SB_W)),
                (jax.ShapeDtypeStruct((2, n, SB_W), BF16),
                 pl.BlockSpec((2, tm, SB_W), lambda i: (0, i, 0))),
                (state, state_spec), (state, state_spec), blocks(SB_W)]
        if prev is not None:
            aliases = {len(args): 2, len(args) + 1: 3}
            args += list(prev)
            in_specs += [pl.BlockSpec(memory_space=pl.ANY)] * 2
    else:
        outs = [(jax.ShapeDtypeStruct((n, SB_W), BF16), tok(SB_W)),
                (jax.ShapeDtypeStruct((n, SB_W), F32), tok(SB_W)), (jax.ShapeDtypeStruct((n, SB_W), F32), tok(SB_W))]
    outs += [(jax.ShapeDtypeStruct((N_HEADS, n, QM_W), BF16), pl.BlockSpec((N_HEADS, tm, QM_W), lambda i: (0, i, 0))),
             (jax.ShapeDtypeStruct((n, QM_W), BF16), tok(QM_W))]
    if transposed:
        outs.append(blocks(KV_LORA))
    outs += [(jax.ShapeDtypeStruct((n, KV_LORA), F32), tok(KV_LORA)),
             (jax.ShapeDtypeStruct((n, MLA_ROPE), F32), tok(MLA_ROPE))]
    return pl.pallas_call(
        functools.partial(_proj_kernel, transposed=transposed, n_alias=len(aliases)),
        grid=(nt,), in_specs=in_specs, out_specs=[o[1] for o in outs], out_shape=[o[0] for o in outs],
        input_output_aliases=aliases,
        compiler_params=_params("parallel"), name="proj_t" if transposed else "proj",
    )(*args)


def _sb_scores(q_ref, k_ref, qrows, j, z_out):
    krows = pl.ds(pl.multiple_of(j * BLK, BLK), BLK)
    for h in range(N_HEADS):
        p, hh = h // 2, h % 2
        cols = slice(p * LANES, (p + 1) * LANES)
        kh = k_ref[hh, 0, krows, cols]
        z_out[h] = _dot_nt(kh, q_ref[0, qrows, cols]) * LOG2E


def _sb_block(z_in, vt_ref, j, acc_ref, car_ref, umat, vis):
    heads = range(N_HEADS)
    mms = []
    for h in heads:
        sp = _softplus2(z_in[h])
        if vis is not None:
            sp = jnp.where(vis, sp, 0.0)
        mms.append(_dot(umat, sp.astype(BF16)))
    for h in heads:
        car = car_ref[h, 0:1, :]
        a = jnp.exp2(z_in[h] + mms[h] + car)
        if vis is not None:
            a = jnp.where(vis, a, 0.0)
        rows = slice(h * SB_HEAD_DIM, (h + 1) * SB_HEAD_DIM)
        acc_ref[rows, :] += _dot(vt_ref[0, j, rows, :], a.astype(BF16))
        car_ref[h, 0:1, :] = car + mms[h][0:1, :]


def _sb_kernel(q_ref, k_ref, vt_ref, o_ref, acc_ref, car_ref, z0_ref, z1_ref, *, nblk):
    kidx = lax.broadcasted_iota(jnp.int32, (BLK, BLK), 0)
    qidx = lax.broadcasted_iota(jnp.int32, (BLK, BLK), 1)
    umat = jnp.where(qidx >= kidx, -1.0, 0.0).astype(BF16)
    vis = kidx < qidx

    def q_block(i, carry):
        acc_ref[...] = jnp.zeros_like(acc_ref)
        car_ref[...] = jnp.zeros_like(car_ref)
        qrows = pl.ds(pl.multiple_of(i * BLK, BLK), BLK)

        def step(j, z_cur, z_next, mask):
            _sb_scores(q_ref, k_ref, qrows, jnp.maximum(j - 1, 0), z_next)
            _sb_block(z_cur, vt_ref, j, acc_ref, car_ref, umat, mask)

        _sb_scores(q_ref, k_ref, qrows, i, z0_ref)
        step(i, z0_ref, z1_ref, vis)

        def k_pair(m, c):
            j = i - 1 - 2 * m
            step(j, z1_ref, z0_ref, None)
            step(j - 1, z0_ref, z1_ref, None)
            return c

        lax.fori_loop(0, i // 2, k_pair, 0)

        @pl.when(i % 2 == 1)
        def _():
            step(0, z1_ref, z0_ref, None)

        o_ref[0, qrows, :] = acc_ref[...].T.astype(BF16)
        return carry

    lax.fori_loop(0, nblk, q_block, 0)


def _sb_prompt(q, kbf, vt, batch, seq_len):
    nblk = seq_len // BLK
    q3 = q.reshape(batch, seq_len, SB_W)
    k4 = kbf.reshape(2, batch, seq_len, SB_W)
    row = pl.BlockSpec((1, seq_len, SB_W), lambda b: (b, 0, 0))
    out = pl.pallas_call(
        functools.partial(_sb_kernel, nblk=nblk),
        grid=(batch,),
        in_specs=[row, pl.BlockSpec((2, 1, seq_len, SB_W), lambda b: (0, b, 0, 0)),
                  pl.BlockSpec((1, nblk, SB_W, BLK), lambda b: (b, 0, 0, 0))],
        out_specs=row,
        out_shape=jax.ShapeDtypeStruct((batch, seq_len, SB_W), BF16),
        scratch_shapes=[pltpu.VMEM((SB_W, BLK), F32), pltpu.VMEM((N_HEADS, SUBLANES, BLK), F32),
                        pltpu.VMEM((N_HEADS, BLK, BLK), F32), pltpu.VMEM((N_HEADS, BLK, BLK), F32)],
        compiler_params=_params("parallel"), name="sb_prompt",
    )(q3, k4, vt)
    return out.reshape(batch * seq_len, SB_W)


def _mla_scores(q_ref, c_ref, qrows, j, s_out):
    c = c_ref[pl.ds(pl.multiple_of(j * BLK, BLK), BLK), :]
    for h in range(N_HEADS):
        s_out[h] = _dot_nt(c, q_ref[h, qrows, :])


def _mla_block(s_in, ct_ref, j, acc_ref, st_ref, vis):
    ct = ct_ref[0, j]
    for h in range(N_HEADS):
        s = s_in[h]
        if vis is not None:
            s = jnp.where(vis, s, -jnp.inf)
        m_old = st_ref[h, 0:1, :]
        m_new = jnp.maximum(m_old, jnp.max(s, axis=0, keepdims=True))
        p = jnp.exp2((s - m_new) * (MLA_SCALE * LOG2E))
        alpha = jnp.exp2((m_old - m_new) * (MLA_SCALE * LOG2E))
        st_ref[h, 1:2, :] = alpha * st_ref[h, 1:2, :] + jnp.sum(p, axis=0, keepdims=True)
        st_ref[h, 0:1, :] = m_new
        acc_ref[h] = alpha * acc_ref[h] + _dot(ct, p.astype(BF16))


def _mla_kernel(q_ref, c_ref, ct_ref, o_ref, acc_ref, st_ref, s0_ref, s1_ref, *, nblk):
    kidx = lax.broadcasted_iota(jnp.int32, (BLK, BLK), 0)
    qidx = lax.broadcasted_iota(jnp.int32, (BLK, BLK), 1)
    vis = (kidx // CHUNK) <= (qidx // CHUNK)

    def q_block(i, carry):
        acc_ref[...] = jnp.zeros_like(acc_ref)
        for h in range(N_HEADS):
            st_ref[h, 0:1, :] = jnp.full((1, BLK), -jnp.inf, F32)
            st_ref[h, 1:2, :] = jnp.zeros((1, BLK), F32)
        rows = pl.ds(pl.multiple_of(i * BLK, BLK), BLK)
        def step(j, j_next, s_cur, s_next, mask):
            _mla_scores(q_ref, c_ref, rows, j_next, s_next)
            _mla_block(s_cur, ct_ref, j, acc_ref, st_ref, mask)

        nxt = lambda j: jnp.minimum(j + 1, nblk - 1)
        _mla_scores(q_ref, c_ref, rows, i, s0_ref)
        step(i, 0, s0_ref, s1_ref, vis)

        def k_pair(m, c):
            j = 2 * m
            step(j, nxt(j), s1_ref, s0_ref, None)
            step(j + 1, nxt(j + 1), s0_ref, s1_ref, None)
            return c

        lax.fori_loop(0, i // 2, k_pair, 0)

        @pl.when(i % 2 == 1)
        def _():
            step(i - 1, nxt(i - 1), s1_ref, s0_ref, None)

        for h in range(N_HEADS):
            o = acc_ref[h] * (1.0 / st_ref[h, 1:2, :])
            o_ref[rows, h * KV_LORA:(h + 1) * KV_LORA] = o.T.astype(BF16)
        return carry

    lax.fori_loop(0, nblk, q_block, 0)


def _mla_prompt(qm, cm, ct, batch, seq_len):
    nblk = seq_len // BLK
    return pl.pallas_call(
        functools.partial(_mla_kernel, nblk=nblk),
        grid=(batch,),
        in_specs=[pl.BlockSpec((N_HEADS, seq_len, QM_W), lambda b: (0, b, 0)),
                  pl.BlockSpec((seq_len, QM_W), lambda b: (b, 0)),
                  pl.BlockSpec((1, nblk, KV_LORA, BLK), lambda b: (b, 0, 0, 0))],
        out_specs=pl.BlockSpec((seq_len, N_HEADS * KV_LORA), lambda b: (b, 0)),
        out_shape=jax.ShapeDtypeStruct((batch * seq_len, N_HEADS * KV_LORA), BF16),
        scratch_shapes=[pltpu.VMEM((N_HEADS, KV_LORA, BLK), F32), pltpu.VMEM((N_HEADS, SUBLANES, BLK), F32),
                        pltpu.VMEM((N_HEADS, BLK, BLK), F32), pltpu.VMEM((N_HEADS, BLK, BLK), F32)],
        compiler_params=_params("parallel"), name="mla_prompt",
    )(qm, cm, ct)


def _sbdec_blocks(score, av, nblocks, t, acc_ref, car_ref, wn, vis):
    heads = range(N_HEADS)
    zs = [jnp.concatenate([score(b, h) for h in heads], axis=0) * LOG2E
          for b in range(nblocks)]
    mms = []
    for b in range(nblocks):
        sp = _softplus2(zs[b])
        if vis is not None:
            sp = jnp.where(vis, sp, 0.0)
        mms.append(_dot(sp.astype(BF16), wn))
    car = car_ref[:, 0:1]
    for b in range(nblocks):
        a = jnp.exp2(zs[b] + mms[b] + car)
        if vis is not None:
            a = jnp.where(vis, a, 0.0)
        a = a.astype(BF16)
        for h in heads:
            acc_ref[h] += av(b, h, a[h * t:(h + 1) * t, :])
        car = car + mms[b][:, 0:1]
    car_ref[...] = jnp.broadcast_to(car, car_ref.shape)


def _sbdec_kernel(q_ref, kn_ref, vn_ref, ck_ref, cv_ref, o_ref, qh_ref, acc_ref, car_ref, *, t_new):
    j = pl.program_id(1)
    hcols = lambda h: slice(h * SB_HEAD_DIM, (h + 1) * SB_HEAD_DIM)

    @pl.when(j == 0)
    def _():
        acc_ref[...] = jnp.zeros_like(acc_ref)
        car_ref[...] = jnp.zeros_like(car_ref)
        for h in range(N_HEADS):
            qh_ref[h] = q_ref[:, hcols(h)]
        kp = lax.broadcasted_iota(jnp.int32, (t_new, t_new), 0)
        kc = lax.broadcasted_iota(jnp.int32, (t_new, t_new), 1)
        wn_new = jnp.where(kp >= kc, -1.0, 0.0).astype(BF16)
        qi = lax.rem(lax.broadcasted_iota(jnp.int32, (N_HEADS * t_new, t_new), 0), t_new)
        ki = lax.broadcasted_iota(jnp.int32, (N_HEADS * t_new, t_new), 1)
        _sbdec_blocks(lambda b, h: _dot_nt(qh_ref[h], kn_ref[:, hcols(h)].astype(BF16)),
                      lambda b, h, a: _dot(a, vn_ref[:, hcols(h)].astype(BF16)),
                      1, t_new, acc_ref, car_ref, wn_new, ki < qi)

    kp = lax.broadcasted_iota(jnp.int32, (BLK, BLK), 0)
    kc = lax.broadcasted_iota(jnp.int32, (BLK, BLK), 1)
    wn = jnp.where(kp >= kc, -1.0, 0.0).astype(BF16)
    nsub = DEC_KEYS // BLK
    keys = lambda b: slice((nsub - 1 - b) * BLK, (nsub - b) * BLK)
    _sbdec_blocks(lambda b, h: _dot(qh_ref[h], ck_ref[0, 0, h, :, keys(b)].astype(BF16)),
                  lambda b, h, a: _dot_nt(a, cv_ref[0, 0, h, :, keys(b)].astype(BF16)),
                  nsub, t_new, acc_ref, car_ref, wn, None)

    @pl.when(j == pl.num_programs(1) - 1)
    def _():
        for h in range(N_HEADS):
            o_ref[:, hcols(h)] = acc_ref[h].astype(BF16)


def _sb_sample(q, k_new, v_new, cache_k, cache_v, layer, batch, t_new):
    past = cache_k.shape[4]
    assert past % DEC_KEYS == 0
    nkb = past // DEC_KEYS
    tok = lambda: pl.BlockSpec((t_new, SB_W), lambda b, j: (b, 0))
    cache = lambda: pl.BlockSpec((1, 1, N_HEADS, SB_HEAD_DIM, DEC_KEYS), lambda b, j: (layer, b, 0, 0, nkb - 1 - j))
    return pl.pallas_call(
        functools.partial(_sbdec_kernel, t_new=t_new),
        grid=(batch, nkb),
        in_specs=[tok(), tok(), tok(), cache(), cache()],
        out_specs=tok(),
        out_shape=jax.ShapeDtypeStruct((batch * t_new, SB_W), BF16),
        scratch_shapes=[pltpu.VMEM((N_HEADS, t_new, SB_HEAD_DIM), BF16),
                        pltpu.VMEM((N_HEADS, t_new, SB_HEAD_DIM), F32),
                        pltpu.VMEM((N_HEADS * t_new, LANES), F32)],
        compiler_params=_params("parallel", "arbitrary"), name="sb_sample",
    )(q, k_new, v_new, cache_k, cache_v)


def _mladec_kernel(q_ref, cn_ref, ck_ref, kr_ref, o_ref, *, t_new, past, new_visible):
    q = q_ref[...].reshape(N_HEADS * t_new, QM_W)
    q_lat = q[:, :KV_LORA]
    q_rope = q[:, KV_LORA:KV_LORA + MLA_ROPE]
    c2 = MLA_SCALE * LOG2E

    cn = cn_ref[...]
    s = _dot_nt(q, cn)
    if not new_visible:
        qi = past + lax.broadcasted_iota(jnp.int32, s.shape, 0) % t_new
        ki = past + lax.broadcasted_iota(jnp.int32, s.shape, 1)
        s = jnp.where((ki // CHUNK) <= (qi // CHUNK), s, -jnp.inf)
    chunks = [slice(c * MLA_DEC_KEYS, (c + 1) * MLA_DEC_KEYS) for c in range(past // MLA_DEC_KEYS)]
    cks = [ck_ref[0, 0, rows, :].astype(BF16) for rows in chunks]
    ss = [_dot_nt(q_lat, ck) + _dot(q_rope, kr_ref[0, 0, :, rows].astype(BF16))
          for ck, rows in zip(cks, chunks)]
    m = jnp.max(s, axis=1, keepdims=True)
    for sc in ss:
        m = jnp.maximum(m, jnp.max(sc, axis=1, keepdims=True))
    p = jnp.exp2((s - m) * c2)
    l = jnp.sum(p, axis=1, keepdims=True)
    acc = _dot(p.astype(BF16), cn[:, :KV_LORA])
    for sc, ck in zip(ss, cks):
        p = jnp.exp2((sc - m) * c2)
        l = l + jnp.sum(p, axis=1, keepdims=True)
        acc = acc + _dot(p.astype(BF16), ck)
    o = (acc * (1.0 / l)).astype(BF16)
    for h in range(N_HEADS):
        o_ref[:, h * KV_LORA:(h + 1) * KV_LORA] = o[h * t_new:(h + 1) * t_new, :]


def _mla_sample(qm, cm, cache_ckv, cache_kr, layer, batch, t_new):
    past = cache_ckv.shape[2]
    assert past % MLA_DEC_KEYS == 0 and past % CHUNK == 0
    q_pos = past + np.arange(t_new)
    new_visible = bool(((q_pos[None, :] // CHUNK) <= (q_pos[:, None] // CHUNK)).all())
    return pl.pallas_call(
        functools.partial(_mladec_kernel, t_new=t_new, past=past, new_visible=new_visible),
        grid=(batch,),
        in_specs=[pl.BlockSpec((N_HEADS, t_new, QM_W), lambda b: (0, b, 0)),
                  pl.BlockSpec((t_new, QM_W), lambda b: (b, 0)),
                  pl.BlockSpec((1, 1, past, KV_LORA), lambda b: (layer, b, 0, 0)),
                  pl.BlockSpec((1, 1, MLA_ROPE, past), lambda b: (layer, b, 0, 0))],
        out_specs=pl.BlockSpec((t_new, N_HEADS * KV_LORA), lambda b: (b, 0)),
        out_shape=jax.ShapeDtypeStruct((batch * t_new, N_HEADS * KV_LORA), BF16),
        compiler_params=_params("parallel"), name="mla_sample",
    )(qm, cm, cache_ckv, cache_kr)


def _mix_kernel(x_ref, gpre_ref, osb_ref, olat_ref, wg_ref, wpsb_ref, wuv_ref, wpmla_ref, wout_ref,
                gpost_ref, o_ref):
    x = x_ref[...]
    hb = _rmsnorm(x, gpre_ref[...]).astype(BF16)
    gates = _dot(hb, wg_ref[...])
    y_a = _dot(osb_ref[...], wpsb_ref[...])
    o_mla = _dot(olat_ref[...], wuv_ref[...]).astype(BF16)
    y_b = _dot(o_mla, wpmla_ref[...])
    mixin = _sigmoid(gates[:, :D_MODEL]) * y_a + _sigmoid(gates[:, D_MODEL:]) * y_b
    mix = _dot(mixin.astype(BF16), wout_ref[...])
    o_ref[...] = x + _rmsnorm(mix, gpost_ref[...])


def _mix(x2d, o_sb, o_lat, lw):
    n = x2d.shape[0]
    tm = MIX_TILE
    assert n % tm == 0
    tok = lambda w: pl.BlockSpec((tm, w), lambda i: (i, 0))
    return pl.pallas_call(
        _mix_kernel, grid=(n // tm,),
        in_specs=[tok(D_MODEL), _const_spec((1, D_MODEL)), tok(SB_W), tok(N_HEADS * KV_LORA),
                  _const_spec(lw["w_gates"].shape), _const_spec(lw["w_proj_sb"].shape),
                  _const_spec(lw["w_uv_bd"].shape), _const_spec(lw["w_proj_mla"].shape),
                  _const_spec(lw["w_out"].shape), _const_spec((1, D_MODEL))],
        out_specs=tok(D_MODEL), out_shape=jax.ShapeDtypeStruct((n, D_MODEL), F32),
        compiler_params=_params("parallel"), name="mix",
    )(x2d, lw["g_pre_mix"], o_sb, o_lat, lw["w_gates"], lw["w_proj_sb"], lw["w_uv_bd"], lw["w_proj_mla"],
      lw["w_out"], lw["g_post_mix"])


def _ffn_kernel(x_ref, gpre_ref, wup_ref, cw_ref, past_ref, wdown_ref, gpost_ref, wpg_ref, p_ref, wpp_ref,
                o_ref, conv_ref, *u_scr, nseq, rows):
    t = pl.program_id(1)
    span = SUBLANES + rows
    lead = SUBLANES - (CONV_W - 1)

    @pl.when(t == 0)
    def _():
        for c in range(2 * N_FF_CHUNKS):
            for s in range(nseq):
                u_scr[c][s * span + lead:s * span + SUBLANES, :] = past_ref[s, c]

    x = x_ref[...]
    hb = _rmsnorm(x, gpre_ref[...]).astype(BF16)

    def up(c):
        for cc in (c, N_FF_CHUNKS + c):
            u = _dot(hb, wup_ref[:, cc * FF_CHUNK:(cc + 1) * FF_CHUNK])
            for s in range(nseq):
                u_scr[cc][s * span + SUBLANES:(s + 1) * span, :] = u[s * rows:(s + 1) * rows, :]

    def conv(c):
        cw = cw_ref[c]
        u = u_scr[c]
        out = []
        for s in range(nseq):
            full = u[s * span:(s + 1) * span, :]
            uc = cw[3:4, :]
            for i in range(CONV_W):
                back = CONV_W - 1 - i
                shifted = full if back == 0 else pltpu.roll(full, back, axis=0)
                uc = uc + shifted[SUBLANES:, :] * cw[i:i + 1, :]
            conv_ref[s, c] = u[(s + 1) * span - (CONV_W - 1):(s + 1) * span, :]
            u[s * span:s * span + SUBLANES, :] = u[s * span + rows:(s + 1) * span, :]
            out.append(uc)
        return out[0] if nseq == 1 else jnp.concatenate(out, axis=0)

    up(0)
    f = None
    acts = []
    for c in range(N_FF_CHUNKS):
        if c + 1 < N_FF_CHUNKS:
            up(c + 1)
        acts.append((_gelu_tanh(conv(c)) * conv(N_FF_CHUNKS + c)).astype(BF16))
        if len(acts) == FF_DOWN_GROUP or c + 1 == N_FF_CHUNKS:
            k1 = (c + 1) * FF_CHUNK
            k0 = k1 - len(acts) * FF_CHUNK
            part = _dot(jnp.concatenate(acts, axis=1), wdown_ref[k0:k1, :])
            f = part if f is None else f + part
            acts = []
    x2 = x + _rmsnorm(f, gpost_ref[...])
    ple = _sigmoid(_dot(x2.astype(BF16), wpg_ref[...])) * _dot(p_ref[0].astype(BF16), wpp_ref[...])
    o_ref[...] = x2 + ple


def _ffn(x2d, p3d, layer, conv_past, batch, seq_len, lw):
    tm = TOK_TILE
    rows = min(tm, seq_len)
    nseq = tm // rows
    nt = seq_len // rows
    assert batch % nseq == 0 and (nseq == 1 or nt == 1)
    nch = 2 * N_FF_CHUNKS
    tok = lambda w: pl.BlockSpec((tm, w), lambda b, t: (b * nt + t, 0))
    ple = pl.BlockSpec((1, tm, D_PLE), lambda b, t: (layer, b * nt + t, 0))
    state = pl.BlockSpec((nseq, nch, CONV_W - 1, FF_CHUNK), lambda b, t: (b, 0, 0, 0))
    return pl.pallas_call(
        functools.partial(_ffn_kernel, nseq=nseq, rows=rows),
        grid=(batch // nseq, nt),
        in_specs=[tok(D_MODEL), _const_spec((1, D_MODEL)), _const_spec(lw["w_up"].shape),
                  _const_spec(lw["conv"].shape), state, _const_spec(lw["w_down"].shape),
                  _const_spec((1, D_MODEL)), _const_spec(lw["w_ple_gate"].shape), ple,
                  _const_spec(lw["w_ple_proj"].shape)],
        out_specs=[tok(D_MODEL), state],
        out_shape=[jax.ShapeDtypeStruct(x2d.shape, F32), jax.ShapeDtypeStruct(conv_past.shape, F32)],
        scratch_shapes=[pltpu.VMEM((nseq * (SUBLANES + rows), FF_CHUNK), F32) for _ in range(nch)],
        compiler_params=_params("arbitrary", "arbitrary"), name="ffn",
    )(x2d, lw["g_pre_ffn"], lw["w_up"], lw["conv"], conv_past, lw["w_down"], lw["g_post_ffn"],
      lw["w_ple_gate"], p3d, lw["w_ple_proj"])


def _prep_layer(w_in, ckv_gain, w_uk, w_uv, w_proj_sb, w_proj_mla, w_out, g_pre_mix, g_post_mix, g_pre_ffn,
                g_post_ffn, w_up, conv_w, conv_b, w_down, w_ple_gate, w_ple_proj):
    o_qm = 3 * SB_W
    o_ckv = o_qm + N_HEADS * (MLA_NOPE + MLA_ROPE)
    o_kr = o_ckv + KV_LORA
    o_g = o_kr + MLA_ROPE
    wqm = w_in[:, o_qm:o_ckv].reshape(D_MODEL, N_HEADS, MLA_NOPE + MLA_ROPE)
    w_qn = wqm[:, :, :MLA_NOPE].reshape(D_MODEL, N_HEADS * MLA_NOPE)
    w_qr = jnp.concatenate([wqm[:, :, MLA_NOPE:MLA_NOPE + ROPE_HALF].reshape(D_MODEL, LANES),
                            wqm[:, :, MLA_NOPE + ROPE_HALF:].reshape(D_MODEL, LANES)], axis=1)
    w_kr = w_in[:, o_kr:o_g]
    w_kx = jnp.zeros((D_MODEL, 2 * LANES), F32)
    w_kx = w_kx.at[:, :ROPE_HALF].set(w_kr[:, :ROPE_HALF]).at[:, LANES:LANES + ROPE_HALF].set(w_kr[:, ROPE_HALF:])
    wa = jnp.zeros((N_HEADS, LANES, QM_W), F32)
    wb = np.zeros((N_HEADS, 2 * LANES, QM_W), np.float32)
    for h in range(N_HEADS):
        r0 = (h % 2) * MLA_NOPE
        wa = wa.at[h, r0:r0 + MLA_NOPE, :KV_LORA].set(w_uk[:, h, :].T)
        for i in range(ROPE_HALF):
            wb[h, h * ROPE_HALF + i, KV_LORA + i] = 1.0
            wb[h, LANES + h * ROPE_HALF + i, KV_LORA + ROPE_HALF + i] = 1.0
    w_uv_bd = jnp.zeros((N_HEADS * KV_LORA, N_HEADS * MLA_V), F32)
    for h in range(N_HEADS):
        w_uv_bd = w_uv_bd.at[h * KV_LORA:(h + 1) * KV_LORA, h * MLA_V:(h + 1) * MLA_V].set(w_uv[:, h, :])
    nch = 2 * N_FF_CHUNKS
    conv = jnp.concatenate([conv_w, conv_b[None, :], jnp.zeros((SUBLANES - CONV_W - 1, 2 * D_FF), F32)], axis=0)
    row = lambda g: g.reshape(1, -1)
    return dict(
        w_qkv=w_in[:, :o_qm].astype(BF16), w_kvt=w_in[:, SB_W:o_qm].T.astype(BF16),
        w_qn=w_qn.astype(BF16), w_qr=w_qr.astype(BF16),
        w_ckv=w_in[:, o_ckv:o_kr].astype(BF16), w_kx=w_kx.astype(BF16), ckv_gain=row(ckv_gain),
        wa=wa.astype(BF16), wb=jnp.asarray(wb, BF16), eye=jnp.eye(LANES, dtype=BF16),
        w_gates=w_in[:, o_g:].astype(BF16), w_proj_sb=w_proj_sb.astype(BF16), w_uv_bd=w_uv_bd.astype(BF16),
        w_proj_mla=w_proj_mla.astype(BF16), w_out=w_out.astype(BF16),
        g_pre_mix=row(g_pre_mix), g_post_mix=row(g_post_mix), g_pre_ffn=row(g_pre_ffn), g_post_ffn=row(g_post_ffn),
        w_up=w_up.astype(BF16),
        conv=conv.reshape(SUBLANES, nch, FF_CHUNK).transpose(1, 0, 2),
        w_down=w_down.astype(BF16),
        w_ple_gate=w_ple_gate.astype(BF16), w_ple_proj=w_ple_proj.astype(BF16))


def _rope_tables(pos, rows):
    freqs = jnp.power(ROPE_THETA, -jnp.arange(ROPE_HALF, dtype=F32) / ROPE_HALF)
    ang = pos.astype(F32)[:, None] * freqs[None, :]
    reps = max(1, rows // pos.shape[0])
    tile = lambda a: jnp.tile(a, (reps, LANES // ROPE_HALF))
    return tile(jnp.cos(ang)), tile(jnp.sin(ang))


def _conv_to_chunks(state):
    b = state.shape[0]
    return state.reshape(b, CONV_W - 1, 2 * N_FF_CHUNKS, FF_CHUNK).transpose(0, 2, 1, 3)


def _conv_from_chunks(state):
    b = state.shape[0]
    return state.transpose(0, 2, 1, 3).reshape(b, CONV_W - 1, 2 * D_FF)


def kernel(x_prompt, x_sample, p_prompt, p_sample, cache_sb_k, cache_sb_v, cache_mla_ckv, cache_mla_krope,
           state_ffn_conv, w_in, ckv_gain, w_uk, w_uv, w_proj_sb, w_proj_mla, w_out, g_pre_mix, g_post_mix,
           g_pre_ffn, g_post_ffn, w_up, conv_w, conv_b, w_down, w_ple_gate, w_ple_proj):
    bp, tp, _ = x_prompt.shape
    bs, ts, _ = x_sample.shape
    depth = w_in.shape[0]
    past = cache_sb_k.shape[2]
    assert tp % TOK_TILE == 0 and (bs * ts) % TOK_TILE == 0 and TOK_TILE % ts == 0
    tabs_p = _rope_tables(jnp.arange(tp, dtype=jnp.int32), TOK_TILE)
    tabs_s = _rope_tables(past + jnp.arange(ts, dtype=jnp.int32), TOK_TILE)
    xp = x_prompt.reshape(bp * tp, D_MODEL)
    xs = x_sample.reshape(bs * ts, D_MODEL)
    pp = p_prompt.reshape(depth, bp * tp, D_PLE)
    ps = p_sample.reshape(depth, bs * ts, D_PLE)
    ck_t = cache_sb_k.transpose(0, 1, 3, 4, 2)
    cv_t = cache_sb_v.transpose(0, 1, 3, 4, 2)
    kr_t = cache_mla_krope.transpose(0, 1, 3, 2)
    zero_conv = jnp.zeros((bp, 2 * N_FF_CHUNKS, CONV_W - 1, FF_CHUNK), F32)
    st_p, st_s = [], []
    kv_p = None
    for i in range(depth):
        lw = _prep_layer(w_in[i], ckv_gain[i], w_uk[i], w_uv[i], w_proj_sb[i], w_proj_mla[i], w_out[i],
                         g_pre_mix[i], g_post_mix[i], g_pre_ffn[i], g_post_ffn[i], w_up[i], conv_w[i],
                         conv_b[i], w_down[i], w_ple_gate[i], w_ple_proj[i])
        q, k_bf, kt_all, vt_all, vt, qm, cm, ct, c_st, r_st = _proj(xp, tp, tabs_p, lw, kv_states=(i, depth, kv_p))
        kv_p = (kt_all, vt_all)
        o_sb = _sb_prompt(q, k_bf, vt, bp, tp)
        o_lat = _mla_prompt(qm, cm, ct, bp, tp)
        x1 = _mix(xp, o_sb, o_lat, lw)
        xp, conv_p = _ffn(x1, pp, i, zero_conv, bp, tp, lw)
        st_p.append((c_st.reshape(bp, tp, KV_LORA), r_st.reshape(bp, tp, MLA_ROPE), _conv_from_chunks(conv_p)))
        q, k_st, v_st, qm, cm, c_st, r_st = _proj(xs, ts, tabs_s, lw)
        o_sb = _sb_sample(q, k_st, v_st, ck_t, cv_t, i, bs, ts)
        o_lat = _mla_sample(qm, cm, cache_mla_ckv, kr_t, i, bs, ts)
        x1 = _mix(xs, o_sb, o_lat, lw)
        xs, conv_s = _ffn(x1, ps, i, _conv_to_chunks(state_ffn_conv[i]), bs, ts, lw)
        st_s.append((k_st.reshape(bs, ts, N_HEADS, SB_HEAD_DIM), v_st.reshape(bs, ts, N_HEADS, SB_HEAD_DIM),
                     c_st.reshape(bs, ts, KV_LORA), r_st.reshape(bs, ts, MLA_ROPE), _conv_from_chunks(conv_s)))
    stack = lambda sts, k: jnp.stack([s[k] for s in sts])
    heads_last = lambda a: a.reshape(depth, bp, N_HEADS, SB_HEAD_DIM, tp).transpose(0, 1, 4, 2, 3)
    return ((xp.reshape(bp, tp, D_MODEL), xs.reshape(bs, ts, D_MODEL), heads_last(kv_p[0]), heads_last(kv_p[1]))
            + tuple(stack(st_p, k) for k in range(3)) + tuple(stack(st_s, k) for k in range(5)))
```

```python
import functools
import math

import jax
import jax.numpy as jnp
import numpy as np
from jax import lax
from jax.experimental import pallas as pl
from jax.experimental.pallas import tpu as pltpu

D_MODEL = 1024
N_HEADS = 8
SB_HEAD_DIM = 64
SB_W = N_HEADS * SB_HEAD_DIM
MLA_NOPE = 64
MLA_ROPE = 32
ROPE_HALF = MLA_ROPE // 2
KV_LORA = 128
MLA_V = 64
D_FF = 2816
CONV_W = 3
CHUNK = 64
D_PLE = 256
ROPE_THETA = 10000.0
EPS = 1e-6
SB_SCALE = SB_HEAD_DIM ** -0.5
MLA_SCALE = (MLA_NOPE + MLA_ROPE) ** -0.5
LOG2E = math.log2(math.e)

LANES = 128
SUBLANES = 8
MXU_DIM = 256
VMEM_LIMIT_BYTES = 56 * 1024 * 1024

BLK = MXU_DIM
TOK_TILE = 2 * MXU_DIM
MIX_TILE = 4 * MXU_DIM
FF_CHUNK = MXU_DIM
N_FF_CHUNKS = D_FF // FF_CHUNK
FF_DOWN_GROUP = 4
QM_W = 2 * LANES
DEC_KEYS = 2048
MLA_DEC_KEYS = 512

BF16 = jnp.bfloat16
F32 = jnp.float32


def _dot(a, b):
    return jnp.dot(a, b, preferred_element_type=F32)


def _dot_nt(a, b):
    return lax.dot_general(a, b, (((1,), (1,)), ((), ())), preferred_element_type=F32)


def _rmsnorm(x, g):
    ms = jnp.mean(x * x, axis=-1, keepdims=True)
    return x * lax.rsqrt(ms + EPS) * g


def _sigmoid(x):
    return 1.0 / (1.0 + jnp.exp(-x))


def _softplus2(z2):
    return jnp.maximum(z2, 0.0) + jnp.log2(1.0 + jnp.exp2(-jnp.abs(z2)))


def _gelu_tanh(x):
    c = math.sqrt(2.0 / math.pi)
    return 0.5 * x * (1.0 + jnp.tanh(c * (x + 0.044715 * (x * x * x))))


def _params(*sem):
    return pltpu.CompilerParams(dimension_semantics=sem, vmem_limit_bytes=VMEM_LIMIT_BYTES)


def _const_spec(shape):
    return pl.BlockSpec(shape, lambda *_: (0,) * len(shape), pipeline_mode=pl.Buffered(1))


def _proj_kernel(x_ref, g_ref, cos_ref, sin_ref, wqkv_ref, wkvt_ref, wqn_ref, wqr_ref, wckv_ref, wkx_ref,
                 gain_ref, wa_ref, wb_ref, eye_ref, *refs, transposed, n_alias, state_slab=0):
    out_refs = refs[n_alias:]
    if transposed:
        (q_ref, kbf_ref, ktst_ref, vtst_ref, vt_ref, qm_ref, cm_ref, ct_ref, cst_ref, rst_ref) = out_refs
    else:
        (q_ref, kst_ref, vst_ref, qm_ref, cm_ref, cst_ref, rst_ref) = out_refs
    hb = _rmsnorm(x_ref[...], g_ref[...]).astype(BF16)
    nsub = x_ref.shape[0] // BLK

    if transposed:
        qk = _dot(hb, wqkv_ref[:, :2 * SB_W])
        q_ref[...] = (qk[:, :SB_W] * SB_SCALE).astype(BF16)
        kb = qk[:, SB_W:].astype(BF16)
        even = lax.broadcasted_iota(jnp.int32, kb.shape, 1) % LANES < SB_HEAD_DIM
        kbf_ref[0] = jnp.where(even, kb, jnp.zeros_like(kb))
        kbf_ref[1] = jnp.where(even, jnp.zeros_like(kb), kb)
        kt = _dot_nt(wkvt_ref[:SB_W, :], hb)
        vt = _dot_nt(wkvt_ref[SB_W:, :], hb)
        for st_ref, val in ((ktst_ref, kt), (vtst_ref, vt)):
            for d in range(st_ref.shape[0]):
                st_ref[d, 0] = val if d == state_slab else jnp.zeros_like(val)
        for s in range(nsub):
            vt_ref[0, s] = vt[:, s * BLK:(s + 1) * BLK].astype(BF16)
    else:
        qkv = _dot(hb, wqkv_ref[...])
        q_ref[...] = (qkv[:, :SB_W] * SB_SCALE).astype(BF16)
        kst_ref[...] = qkv[:, SB_W:2 * SB_W]
        vst_ref[...] = qkv[:, 2 * SB_W:]

    cos = cos_ref[...]
    sin = sin_ref[...]
    qn = _dot(hb, wqn_ref[...]).astype(BF16)
    qr = _dot(hb, wqr_ref[...])
    x1, x2 = qr[:, :LANES], qr[:, LANES:]
    r12 = jnp.concatenate([x1 * cos - x2 * sin, x1 * sin + x2 * cos], axis=1).astype(BF16)
    for h in range(N_HEADS):
        p = h // 2
        qm_ref[h] = (_dot(qn[:, p * LANES:(p + 1) * LANES], wa_ref[h]) + _dot(r12, wb_ref[h])).astype(BF16)

    ckv = _rmsnorm(_dot(hb, wckv_ref[...]), gain_ref[...])
    cst_ref[...] = ckv
    kx = _dot(hb, wkx_ref[...])
    k1, k2 = kx[:, :LANES], kx[:, LANES:]
    kr1 = k1 * cos - k2 * sin
    kr2 = k1 * sin + k2 * cos
    lane = lax.broadcasted_iota(jnp.int32, kr1.shape, 1)
    kr = jnp.where(lane < ROPE_HALF, kr1, pltpu.roll(kr2, ROPE_HALF, axis=1))
    rst_ref[...] = kr[:, :MLA_ROPE]
    ckv_b = ckv.astype(BF16)
    cm_ref[...] = jnp.concatenate([ckv_b, kr.astype(BF16)], axis=1)
    if transposed:
        for s in range(nsub):
            ct_ref[0, s] = _dot_nt(eye_ref[...], ckv_b[s * BLK:(s + 1) * BLK, :]).astype(BF16)


def _proj(x2d, seq_len, tabs, lw, *, kv_states=None):
    n = x2d.shape[0]
    tm = TOK_TILE
    nt = n // tm
    cos, sin = tabs
    tab_blocks = cos.shape[0] // tm
    transposed = kv_states is not None
    tok = lambda w: pl.BlockSpec((tm, w), lambda i: (i, 0))
    tab = pl.BlockSpec((tm, LANES), lambda i: (i % tab_blocks, 0))
    in_specs = [tok(D_MODEL), _const_spec((1, D_MODEL)), tab, tab,
                _const_spec(lw["w_qkv"].shape), _const_spec(lw["w_kvt"].shape),
                _const_spec(lw["w_qn"].shape), _const_spec(lw["w_qr"].shape),
                _const_spec(lw["w_ckv"].shape), _const_spec(lw["w_kx"].shape), _const_spec((1, KV_LORA)),
                _const_spec(lw["wa"].shape), _const_spec(lw["wb"].shape), _const_spec((LANES, LANES))]
    args = [x2d, lw["g_pre_mix"], cos, sin, lw["w_qkv"], lw["w_kvt"], lw["w_qn"], lw["w_qr"], lw["w_ckv"],
            lw["w_kx"], lw["ckv_gain"], lw["wa"], lw["wb"], lw["eye"]]
    aliases = {}
    if transposed:
        layer, depth, prev = kv_states
        assert seq_len % tm == 0
        batch = n // seq_len
        nb = seq_len // tm
        sub = tm // BLK
        state = jax.ShapeDtypeStruct((depth, batch, SB_W, seq_len), F32)
        if prev is None:
            state_slab = layer
            state_spec = pl.BlockSpec((depth, 1, SB_W, tm), lambda i: (0, i // nb, 0, i % nb))
        else:
            state_slab = 0
            state_spec = pl.BlockSpec((1, 1, SB_W, tm), lambda i: (layer, i // nb, 0, i % nb))
        blocks = lambda w: (jax.ShapeDtypeStruct((batch, nb * sub, w, BLK), BF16),
                            pl.BlockSpec((1, sub, w, BLK), lambda i: (i // nb, i % nb, 0, 0)))
        outs = [(jax.ShapeDtypeStruct((n, SB_W), BF16), tok(SB_W)),
                (jax.ShapeDtypeStruct((2, n, SB_W), BF16),
                 pl.BlockSpec((2, tm, SB_W), lambda i: (0, i, 0))),
                (state, state_spec), (state, state_spec), blocks(SB_W)]
        if prev is not None:
            aliases = {len(args): 2, len(args) + 1: 3}
            args += list(prev)
            in_specs += [pl.BlockSpec(memory_space=pl.ANY)] * 2
    else:
        outs = [(jax.ShapeDtypeStruct((n, SB_W), BF16), tok(SB_W)),
                (jax.ShapeDtypeStruct((n, SB_W), F32), tok(SB_W)), (jax.ShapeDtypeStruct((n, SB_W), F32), tok(SB_W))]
    outs += [(jax.ShapeDtypeStruct((N_HEADS, n, QM_W), BF16), pl.BlockSpec((N_HEADS, tm, QM_W), lambda i: (0, i, 0))),
             (jax.ShapeDtypeStruct((n, QM_W), BF16), tok(QM_W))]
    if transposed:
        outs.append(blocks(KV_LORA))
    outs += [(jax.ShapeDtypeStruct((n, KV_LORA), F32), tok(KV_LORA)),
             (jax.ShapeDtypeStruct((n, MLA_ROPE), F32), tok(MLA_ROPE))]
    return pl.pallas_call(
        functools.partial(_proj_kernel, transposed=transposed, n_alias=len(aliases),
                          state_slab=state_slab if transposed else 0),
        grid=(nt,), in_specs=in_specs, out_specs=[o[1] for o in outs], out_shape=[o[0] for o in outs],
        input_output_aliases=aliases,
        compiler_params=_params("parallel"), name="proj_t" if transposed else "proj",
    )(*args)


def _sb_scores(q_ref, k_ref, qrows, j, z_out):
    krows = pl.ds(pl.multiple_of(j * BLK, BLK), BLK)
    for h in range(N_HEADS):
        p, hh = h // 2, h % 2
        cols = slice(p * LANES, (p + 1) * LANES)
        kh = k_ref[hh, 0, krows, cols]
        z_out[h] = _dot_nt(kh, q_ref[0, qrows, cols]) * LOG2E


def _sb_block(z_in, vt_ref, j, acc_ref, car_ref, umat, vis):
    heads = range(N_HEADS)
    mms = []
    for h in heads:
        sp = _softplus2(z_in[h])
        if vis is not None:
            sp = jnp.where(vis, sp, 0.0)
        mms.append(_dot(umat, sp.astype(BF16)))
    for h in heads:
        car = car_ref[h, 0:1, :]
        a = jnp.exp2(z_in[h] + mms[h] + car)
        if vis is not None:
            a = jnp.where(vis, a, 0.0)
        rows = slice(h * SB_HEAD_DIM, (h + 1) * SB_HEAD_DIM)
        acc_ref[rows, :] += _dot(vt_ref[0, j, rows, :], a.astype(BF16))
        car_ref[h, 0:1, :] = car + mms[h][0:1, :]


def _sb_kernel(q_ref, k_ref, vt_ref, o_ref, acc_ref, car_ref, z0_ref, z1_ref, *, nblk):
    kidx = lax.broadcasted_iota(jnp.int32, (BLK, BLK), 0)
    qidx = lax.broadcasted_iota(jnp.int32, (BLK, BLK), 1)
    umat = jnp.where(qidx >= kidx, -1.0, 0.0).astype(BF16)
    vis = kidx < qidx

    def q_block(i, carry):
        acc_ref[...] = jnp.zeros_like(acc_ref)
        car_ref[...] = jnp.zeros_like(car_ref)
        qrows = pl.ds(pl.multiple_of(i * BLK, BLK), BLK)

        def step(j, z_cur, z_next, mask):
            _sb_scores(q_ref, k_ref, qrows, jnp.maximum(j - 1, 0), z_next)
            _sb_block(z_cur, vt_ref, j, acc_ref, car_ref, umat, mask)

        _sb_scores(q_ref, k_ref, qrows, i, z0_ref)
        step(i, z0_ref, z1_ref, vis)

        def k_pair(m, c):
            j = i - 1 - 2 * m
            step(j, z1_ref, z0_ref, None)
            step(j - 1, z0_ref, z1_ref, None)
            return c

        lax.fori_loop(0, i // 2, k_pair, 0)

        @pl.when(i % 2 == 1)
        def _():
            step(0, z1_ref, z0_ref, None)

        o_ref[0, qrows, :] = acc_ref[...].T.astype(BF16)
        return carry

    lax.fori_loop(0, nblk, q_block, 0)


def _sb_prompt(q, kbf, vt, batch, seq_len):
    nblk = seq_len // BLK
    q3 = q.reshape(batch, seq_len, SB_W)
    k4 = kbf.reshape(2, batch, seq_len, SB_W)
    row = pl.BlockSpec((1, seq_len, SB_W), lambda b: (b, 0, 0))
    out = pl.pallas_call(
        functools.partial(_sb_kernel, nblk=nblk),
        grid=(batch,),
        in_specs=[row, pl.BlockSpec((2, 1, seq_len, SB_W), lambda b: (0, b, 0, 0)),
                  pl.BlockSpec((1, nblk, SB_W, BLK), lambda b: (b, 0, 0, 0))],
        out_specs=row,
        out_shape=jax.ShapeDtypeStruct((batch, seq_len, SB_W), BF16),
        scratch_shapes=[pltpu.VMEM((SB_W, BLK), F32), pltpu.VMEM((N_HEADS, SUBLANES, BLK), F32),
                        pltpu.VMEM((N_HEADS, BLK, BLK), F32), pltpu.VMEM((N_HEADS, BLK, BLK), F32)],
        compiler_params=_params("parallel"), name="sb_prompt",
    )(q3, k4, vt)
    return out.reshape(batch * seq_len, SB_W)


def _mla_scores(q_ref, c_ref, qrows, j, s_out):
    c = c_ref[pl.ds(pl.multiple_of(j * BLK, BLK), BLK), :]
    for h in range(N_HEADS):
        s_out[h] = _dot_nt(c, q_ref[h, qrows, :])


def _mla_block(s_in, ct_ref, j, acc_ref, st_ref, vis):
    ct = ct_ref[0, j]
    for h in range(N_HEADS):
        s = s_in[h]
        if vis is not None:
            s = jnp.where(vis, s, -jnp.inf)
        m_old = st_ref[h, 0:1, :]
        m_new = jnp.maximum(m_old, jnp.max(s, axis=0, keepdims=True))
        p = jnp.exp2((s - m_new) * (MLA_SCALE * LOG2E))
        alpha = jnp.exp2((m_old - m_new) * (MLA_SCALE * LOG2E))
        st_ref[h, 1:2, :] = alpha * st_ref[h, 1:2, :] + jnp.sum(p, axis=0, keepdims=True)
        st_ref[h, 0:1, :] = m_new
        acc_ref[h] = alpha * acc_ref[h] + _dot(ct, p.astype(BF16))


def _mla_kernel(q_ref, c_ref, ct_ref, o_ref, acc_ref, st_ref, s0_ref, s1_ref, *, nblk):
    kidx = lax.broadcasted_iota(jnp.int32, (BLK, BLK), 0)
    qidx = lax.broadcasted_iota(jnp.int32, (BLK, BLK), 1)
    vis = (kidx // CHUNK) <= (qidx // CHUNK)

    def q_block(i, carry):
        acc_ref[...] = jnp.zeros_like(acc_ref)
        for h in range(N_HEADS):
            st_ref[h, 0:1, :] = jnp.full((1, BLK), -jnp.inf, F32)
            st_ref[h, 1:2, :] = jnp.zeros((1, BLK), F32)
        rows = pl.ds(pl.multiple_of(i * BLK, BLK), BLK)
        def step(j, j_next, s_cur, s_next, mask):
            _mla_scores(q_ref, c_ref, rows, j_next, s_next)
            _mla_block(s_cur, ct_ref, j, acc_ref, st_ref, mask)

        nxt = lambda j: jnp.minimum(j + 1, nblk - 1)
        _mla_scores(q_ref, c_ref, rows, i, s0_ref)
        step(i, 0, s0_ref, s1_ref, vis)

        def k_pair(m, c):
            j = 2 * m
            step(j, nxt(j), s1_ref, s0_ref, None)
            step(j + 1, nxt(j + 1), s0_ref, s1_ref, None)
            return c

        lax.fori_loop(0, i // 2, k_pair, 0)

        @pl.when(i % 2 == 1)
        def _():
            step(i - 1, nxt(i - 1), s1_ref, s0_ref, None)

        for h in range(N_HEADS):
            o = acc_ref[h] * (1.0 / st_ref[h, 1:2, :])
            o_ref[rows, h * KV_LORA:(h + 1) * KV_LORA] = o.T.astype(BF16)
        return carry

    lax.fori_loop(0, nblk, q_block, 0)


def _mla_prompt(qm, cm, ct, batch, seq_len):
    nblk = seq_len // BLK
    return pl.pallas_call(
        functools.partial(_mla_kernel, nblk=nblk),
        grid=(batch,),
        in_specs=[pl.BlockSpec((N_HEADS, seq_len, QM_W), lambda b: (0, b, 0)),
                  pl.BlockSpec((seq_len, QM_W), lambda b: (b, 0)),
                  pl.BlockSpec((1, nblk, KV_LORA, BLK), lambda b: (b, 0, 0, 0))],
        out_specs=pl.BlockSpec((seq_len, N_HEADS * KV_LORA), lambda b: (b, 0)),
        out_shape=jax.ShapeDtypeStruct((batch * seq_len, N_HEADS * KV_LORA), BF16),
        scratch_shapes=[pltpu.VMEM((N_HEADS, KV_LORA, BLK), F32), pltpu.VMEM((N_HEADS, SUBLANES, BLK), F32),
                        pltpu.VMEM((N_HEADS, BLK, BLK), F32), pltpu.VMEM((N_HEADS, BLK, BLK), F32)],
        compiler_params=_params("parallel"), name="mla_prompt",
    )(qm, cm, ct)


def _sbdec_blocks(score, av, nblocks, t, acc_ref, car_ref, wn, vis):
    heads = range(N_HEADS)
    zs = [jnp.concatenate([score(b, h) for h in heads], axis=0) * LOG2E
          for b in range(nblocks)]
    mms = []
    for b in range(nblocks):
        sp = _softplus2(zs[b])
        if vis is not None:
            sp = jnp.where(vis, sp, 0.0)
        mms.append(_dot(sp.astype(BF16), wn))
    car = car_ref[:, 0:1]
    for b in range(nblocks):
        a = jnp.exp2(zs[b] + mms[b] + car)
        if vis is not None:
            a = jnp.where(vis, a, 0.0)
        a = a.astype(BF16)
        for h in heads:
            acc_ref[h] += av(b, h, a[h * t:(h + 1) * t, :])
        car = car + mms[b][:, 0:1]
    car_ref[...] = jnp.broadcast_to(car, car_ref.shape)


def _sbdec_kernel(q_ref, kn_ref, vn_ref, ck_ref, cv_ref, o_ref, qh_ref, acc_ref, car_ref, *, t_new):
    j = pl.program_id(1)
    hcols = lambda h: slice(h * SB_HEAD_DIM, (h + 1) * SB_HEAD_DIM)

    @pl.when(j == 0)
    def _():
        acc_ref[...] = jnp.zeros_like(acc_ref)
        car_ref[...] = jnp.zeros_like(car_ref)
        for h in range(N_HEADS):
            qh_ref[h] = q_ref[:, hcols(h)]
        kp = lax.broadcasted_iota(jnp.int32, (t_new, t_new), 0)
        kc = lax.broadcasted_iota(jnp.int32, (t_new, t_new), 1)
        wn_new = jnp.where(kp >= kc, -1.0, 0.0).astype(BF16)
        qi = lax.rem(lax.broadcasted_iota(jnp.int32, (N_HEADS * t_new, t_new), 0), t_new)
        ki = lax.broadcasted_iota(jnp.int32, (N_HEADS * t_new, t_new), 1)
        _sbdec_blocks(lambda b, h: _dot_nt(qh_ref[h], kn_ref[:, hcols(h)].astype(BF16)),
                      lambda b, h, a: _dot(a, vn_ref[:, hcols(h)].astype(BF16)),
                      1, t_new, acc_ref, car_ref, wn_new, ki < qi)

    kp = lax.broadcasted_iota(jnp.int32, (BLK, BLK), 0)
    kc = lax.broadcasted_iota(jnp.int32, (BLK, BLK), 1)
    wn = jnp.where(kp >= kc, -1.0, 0.0).astype(BF16)
    nsub = DEC_KEYS // BLK
    keys = lambda b: slice((nsub - 1 - b) * BLK, (nsub - b) * BLK)
    _sbdec_blocks(lambda b, h: _dot(qh_ref[h], ck_ref[0, 0, h, :, keys(b)].astype(BF16)),
                  lambda b, h, a: _dot_nt(a, cv_ref[0, 0, h, :, keys(b)].astype(BF16)),
                  nsub, t_new, acc_ref, car_ref, wn, None)

    @pl.when(j == pl.num_programs(1) - 1)
    def _():
        for h in range(N_HEADS):
            o_ref[:, hcols(h)] = acc_ref[h].astype(BF16)


def _sb_sample(q, k_new, v_new, cache_k, cache_v, layer, batch, t_new):
    past = cache_k.shape[4]
    assert past % DEC_KEYS == 0
    nkb = past // DEC_KEYS
    tok = lambda: pl.BlockSpec((t_new, SB_W), lambda b, j: (b, 0))
    cache = lambda: pl.BlockSpec((1, 1, N_HEADS, SB_HEAD_DIM, DEC_KEYS), lambda b, j: (layer, b, 0, 0, nkb - 1 - j))
    return pl.pallas_call(
        functools.partial(_sbdec_kernel, t_new=t_new),
        grid=(batch, nkb),
        in_specs=[tok(), tok(), tok(), cache(), cache()],
        out_specs=tok(),
        out_shape=jax.ShapeDtypeStruct((batch * t_new, SB_W), BF16),
        scratch_shapes=[pltpu.VMEM((N_HEADS, t_new, SB_HEAD_DIM), BF16),
                        pltpu.VMEM((N_HEADS, t_new, SB_HEAD_DIM), F32),
                        pltpu.VMEM((N_HEADS * t_new, LANES), F32)],
        compiler_params=_params("parallel", "arbitrary"), name="sb_sample",
    )(q, k_new, v_new, cache_k, cache_v)


def _mladec_kernel(q_ref, cn_ref, ck_ref, kr_ref, o_ref, *, t_new, past, new_visible):
    q = q_ref[...].reshape(N_HEADS * t_new, QM_W)
    q_lat = q[:, :KV_LORA]
    q_rope = q[:, KV_LORA:KV_LORA + MLA_ROPE]
    c2 = MLA_SCALE * LOG2E

    cn = cn_ref[...]
    s = _dot_nt(q, cn)
    if not new_visible:
        qi = past + lax.broadcasted_iota(jnp.int32, s.shape, 0) % t_new
        ki = past + lax.broadcasted_iota(jnp.int32, s.shape, 1)
        s = jnp.where((ki // CHUNK) <= (qi // CHUNK), s, -jnp.inf)
    chunks = [slice(c * MLA_DEC_KEYS, (c + 1) * MLA_DEC_KEYS) for c in range(past // MLA_DEC_KEYS)]
    cks = [ck_ref[0, 0, rows, :].astype(BF16) for rows in chunks]
    ss = [_dot_nt(q_lat, ck) + _dot(q_rope, kr_ref[0, 0, :, rows].astype(BF16))
          for ck, rows in zip(cks, chunks)]
    m = jnp.max(s, axis=1, keepdims=True)
    for sc in ss:
        m = jnp.maximum(m, jnp.max(sc, axis=1, keepdims=True))
    p = jnp.exp2((s - m) * c2)
    l = jnp.sum(p, axis=1, keepdims=True)
    acc = _dot(p.astype(BF16), cn[:, :KV_LORA])
    for sc, ck in zip(ss, cks):
        p = jnp.exp2((sc - m) * c2)
        l = l + jnp.sum(p, axis=1, keepdims=True)
        acc = acc + _dot(p.astype(BF16), ck)
    o = (acc * (1.0 / l)).astype(BF16)
    for h in range(N_HEADS):
        o_ref[:, h * KV_LORA:(h + 1) * KV_LORA] = o[h * t_new:(h + 1) * t_new, :]


def _mla_sample(qm, cm, cache_ckv, cache_kr, layer, batch, t_new):
    past = cache_ckv.shape[2]
    assert past % MLA_DEC_KEYS == 0 and past % CHUNK == 0
    q_pos = past + np.arange(t_new)
    new_visible = bool(((q_pos[None, :] // CHUNK) <= (q_pos[:, None] // CHUNK)).all())
    return pl.pallas_call(
        functools.partial(_mladec_kernel, t_new=t_new, past=past, new_visible=new_visible),
        grid=(batch,),
        in_specs=[pl.BlockSpec((N_HEADS, t_new, QM_W), lambda b: (0, b, 0)),
                  pl.BlockSpec((t_new, QM_W), lambda b: (b, 0)),
                  pl.BlockSpec((1, 1, past, KV_LORA), lambda b: (layer, b, 0, 0)),
                  pl.BlockSpec((1, 1, MLA_ROPE, past), lambda b: (layer, b, 0, 0))],
        out_specs=pl.BlockSpec((t_new, N_HEADS * KV_LORA), lambda b: (b, 0)),
        out_shape=jax.ShapeDtypeStruct((batch * t_new, N_HEADS * KV_LORA), BF16),
        compiler_params=_params("parallel"), name="mla_sample",
    )(qm, cm, cache_ckv, cache_kr)


def _mix_kernel(x_ref, gpre_ref, osb_ref, olat_ref, wg_ref, wpsb_ref, wuv_ref, wpmla_ref, wout_ref,
                gpost_ref, o_ref):
    x = x_ref[...]
    hb = _rmsnorm(x, gpre_ref[...]).astype(BF16)
    gates = _dot(hb, wg_ref[...])
    y_a = _dot(osb_ref[...], wpsb_ref[...])
    o_mla = _dot(olat_ref[...], wuv_ref[...]).astype(BF16)
    y_b = _dot(o_mla, wpmla_ref[...])
    mixin = _sigmoid(gates[:, :D_MODEL]) * y_a + _sigmoid(gates[:, D_MODEL:]) * y_b
    mix = _dot(mixin.astype(BF16), wout_ref[...])
    o_ref[...] = x + _rmsnorm(mix, gpost_ref[...])


def _mix(x2d, o_sb, o_lat, lw):
    n = x2d.shape[0]
    tm = MIX_TILE
    assert n % tm == 0
    tok = lambda w: pl.BlockSpec((tm, w), lambda i: (i, 0))
    return pl.pallas_call(
        _mix_kernel, grid=(n // tm,),
        in_specs=[tok(D_MODEL), _const_spec((1, D_MODEL)), tok(SB_W), tok(N_HEADS * KV_LORA),
                  _const_spec(lw["w_gates"].shape), _const_spec(lw["w_proj_sb"].shape),
                  _const_spec(lw["w_uv_bd"].shape), _const_spec(lw["w_proj_mla"].shape),
                  _const_spec(lw["w_out"].shape), _const_spec((1, D_MODEL))],
        out_specs=tok(D_MODEL), out_shape=jax.ShapeDtypeStruct((n, D_MODEL), F32),
        compiler_params=_params("parallel"), name="mix",
    )(x2d, lw["g_pre_mix"], o_sb, o_lat, lw["w_gates"], lw["w_proj_sb"], lw["w_uv_bd"], lw["w_proj_mla"],
      lw["w_out"], lw["g_post_mix"])


def _ffn_kernel(x_ref, gpre_ref, wup_ref, cw_ref, past_ref, wdown_ref, gpost_ref, wpg_ref, p_ref, wpp_ref,
                o_ref, conv_ref, *u_scr, nseq, rows):
    t = pl.program_id(1)
    span = SUBLANES + rows
    lead = SUBLANES - (CONV_W - 1)

    @pl.when(t == 0)
    def _():
        for c in range(2 * N_FF_CHUNKS):
            for s in range(nseq):
                u_scr[c][s * span + lead:s * span + SUBLANES, :] = past_ref[s, c]

    x = x_ref[...]
    hb = _rmsnorm(x, gpre_ref[...]).astype(BF16)

    def up(c):
        for cc in (c, N_FF_CHUNKS + c):
            u = _dot(hb, wup_ref[:, cc * FF_CHUNK:(cc + 1) * FF_CHUNK])
            for s in range(nseq):
                u_scr[cc][s * span + SUBLANES:(s + 1) * span, :] = u[s * rows:(s + 1) * rows, :]

    def conv(c):
        cw = cw_ref[c]
        u = u_scr[c]
        out = []
        for s in range(nseq):
            full = u[s * span:(s + 1) * span, :]
            uc = cw[3:4, :]
            for i in range(CONV_W):
                back = CONV_W - 1 - i
                shifted = full if back == 0 else pltpu.roll(full, back, axis=0)
                uc = uc + shifted[SUBLANES:, :] * cw[i:i + 1, :]
            conv_ref[s, c] = u[(s + 1) * span - (CONV_W - 1):(s + 1) * span, :]
            u[s * span:s * span + SUBLANES, :] = u[s * span + rows:(s + 1) * span, :]
            out.append(uc)
        return out[0] if nseq == 1 else jnp.concatenate(out, axis=0)

    up(0)
    f = None
    acts = []
    for c in range(N_FF_CHUNKS):
        if c + 1 < N_FF_CHUNKS:
            up(c + 1)
        acts.append((_gelu_tanh(conv(c)) * conv(N_FF_CHUNKS + c)).astype(BF16))
        if len(acts) == FF_DOWN_GROUP or c + 1 == N_FF_CHUNKS:
            k1 = (c + 1) * FF_CHUNK
            k0 = k1 - len(acts) * FF_CHUNK
            part = _dot(jnp.concatenate(acts, axis=1), wdown_ref[k0:k1, :])
            f = part if f is None else f + part
            acts = []
    x2 = x + _rmsnorm(f, gpost_ref[...])
    ple = _sigmoid(_dot(x2.astype(BF16), wpg_ref[...])) * _dot(p_ref[0].astype(BF16), wpp_ref[...])
    o_ref[...] = x2 + ple


def _ffn(x2d, p3d, layer, conv_past, batch, seq_len, lw):
    tm = TOK_TILE
    rows = min(tm, seq_len)
    nseq = tm // rows
    nt = seq_len // rows
    assert batch % nseq == 0 and (nseq == 1 or nt == 1)
    nch = 2 * N_FF_CHUNKS
    tok = lambda w: pl.BlockSpec((tm, w), lambda b, t: (b * nt + t, 0))
    ple = pl.BlockSpec((1, tm, D_PLE), lambda b, t: (layer, b * nt + t, 0))
    state = pl.BlockSpec((nseq, nch, CONV_W - 1, FF_CHUNK), lambda b, t: (b, 0, 0, 0))
    return pl.pallas_call(
        functools.partial(_ffn_kernel, nseq=nseq, rows=rows),
        grid=(batch // nseq, nt),
        in_specs=[tok(D_MODEL), _const_spec((1, D_MODEL)), _const_spec(lw["w_up"].shape),
                  _const_spec(lw["conv"].shape), state, _const_spec(lw["w_down"].shape),
                  _const_spec((1, D_MODEL)), _const_spec(lw["w_ple_gate"].shape), ple,
                  _const_spec(lw["w_ple_proj"].shape)],
        out_specs=[tok(D_MODEL), state],
        out_shape=[jax.ShapeDtypeStruct(x2d.shape, F32), jax.ShapeDtypeStruct(conv_past.shape, F32)],
        scratch_shapes=[pltpu.VMEM((nseq * (SUBLANES + rows), FF_CHUNK), F32) for _ in range(nch)],
        compiler_params=_params("arbitrary", "arbitrary"), name="ffn",
    )(x2d, lw["g_pre_ffn"], lw["w_up"], lw["conv"], conv_past, lw["w_down"], lw["g_post_ffn"],
      lw["w_ple_gate"], p3d, lw["w_ple_proj"])


def _prep_layer(w_in, ckv_gain, w_uk, w_uv, w_proj_sb, w_proj_mla, w_out, g_pre_mix, g_post_mix, g_pre_ffn,
                g_post_ffn, w_up, conv_w, conv_b, w_down, w_ple_gate, w_ple_proj):
    o_qm = 3 * SB_W
    o_ckv = o_qm + N_HEADS * (MLA_NOPE + MLA_ROPE)
    o_kr = o_ckv + KV_LORA
    o_g = o_kr + MLA_ROPE
    wqm = w_in[:, o_qm:o_ckv].reshape(D_MODEL, N_HEADS, MLA_NOPE + MLA_ROPE)
    w_qn = wqm[:, :, :MLA_NOPE].reshape(D_MODEL, N_HEADS * MLA_NOPE)
    w_qr = jnp.concatenate([wqm[:, :, MLA_NOPE:MLA_NOPE + ROPE_HALF].reshape(D_MODEL, LANES),
                            wqm[:, :, MLA_NOPE + ROPE_HALF:].reshape(D_MODEL, LANES)], axis=1)
    w_kr = w_in[:, o_kr:o_g]
    w_kx = jnp.zeros((D_MODEL, 2 * LANES), F32)
    w_kx = w_kx.at[:, :ROPE_HALF].set(w_kr[:, :ROPE_HALF]).at[:, LANES:LANES + ROPE_HALF].set(w_kr[:, ROPE_HALF:])
    wa = jnp.zeros((N_HEADS, LANES, QM_W), F32)
    wb = np.zeros((N_HEADS, 2 * LANES, QM_W), np.float32)
    for h in range(N_HEADS):
        r0 = (h % 2) * MLA_NOPE
        wa = wa.at[h, r0:r0 + MLA_NOPE, :KV_LORA].set(w_uk[:, h, :].T)
        for i in range(ROPE_HALF):
            wb[h, h * ROPE_HALF + i, KV_LORA + i] = 1.0
            wb[h, LANES + h * ROPE_HALF + i, KV_LORA + ROPE_HALF + i] = 1.0
    w_uv_bd = jnp.zeros((N_HEADS * KV_LORA, N_HEADS * MLA_V), F32)
    for h in range(N_HEADS):
        w_uv_bd = w_uv_bd.at[h * KV_LORA:(h + 1) * KV_LORA, h * MLA_V:(h + 1) * MLA_V].set(w_uv[:, h, :])
    nch = 2 * N_FF_CHUNKS
    conv = jnp.concatenate([conv_w, conv_b[None, :], jnp.zeros((SUBLANES - CONV_W - 1, 2 * D_FF), F32)], axis=0)
    row = lambda g: g.reshape(1, -1)
    return dict(
        w_qkv=w_in[:, :o_qm].astype(BF16), w_kvt=w_in[:, SB_W:o_qm].T.astype(BF16),
        w_qn=w_qn.astype(BF16), w_qr=w_qr.astype(BF16),
        w_ckv=w_in[:, o_ckv:o_kr].astype(BF16), w_kx=w_kx.astype(BF16), ckv_gain=row(ckv_gain),
        wa=wa.astype(BF16), wb=jnp.asarray(wb, BF16), eye=jnp.eye(LANES, dtype=BF16),
        w_gates=w_in[:, o_g:].astype(BF16), w_proj_sb=w_proj_sb.astype(BF16), w_uv_bd=w_uv_bd.astype(BF16),
        w_proj_mla=w_proj_mla.astype(BF16), w_out=w_out.astype(BF16),
        g_pre_mix=row(g_pre_mix), g_post_mix=row(g_post_mix), g_pre_ffn=row(g_pre_ffn), g_post_ffn=row(g_post_ffn),
        w_up=w_up.astype(BF16),
        conv=conv.reshape(SUBLANES, nch, FF_CHUNK).transpose(1, 0, 2),
        w_down=w_down.astype(BF16),
        w_ple_gate=w_ple_gate.astype(BF16), w_ple_proj=w_ple_proj.astype(BF16))


def _rope_tables(pos, rows):
    freqs = jnp.power(ROPE_THETA, -jnp.arange(ROPE_HALF, dtype=F32) / ROPE_HALF)
    ang = pos.astype(F32)[:, None] * freqs[None, :]
    reps = max(1, rows // pos.shape[0])
    tile = lambda a: jnp.tile(a, (reps, LANES // ROPE_HALF))
    return tile(jnp.cos(ang)), tile(jnp.sin(ang))


def _conv_to_chunks(state):
    b = state.shape[0]
    return state.reshape(b, CONV_W - 1, 2 * N_FF_CHUNKS, FF_CHUNK).transpose(0, 2, 1, 3)


def _conv_from_chunks(state):
    b = state.shape[0]
    return state.transpose(0, 2, 1, 3).reshape(b, CONV_W - 1, 2 * D_FF)


def kernel(x_prompt, x_sample, p_prompt, p_sample, cache_sb_k, cache_sb_v, cache_mla_ckv, cache_mla_krope,
           state_ffn_conv, w_in, ckv_gain, w_uk, w_uv, w_proj_sb, w_proj_mla, w_out, g_pre_mix, g_post_mix,
           g_pre_ffn, g_post_ffn, w_up, conv_w, conv_b, w_down, w_ple_gate, w_ple_proj):
    bp, tp, _ = x_prompt.shape
    bs, ts, _ = x_sample.shape
    depth = w_in.shape[0]
    past = cache_sb_k.shape[2]
    assert tp % TOK_TILE == 0 and (bs * ts) % TOK_TILE == 0 and TOK_TILE % ts == 0
    tabs_p = _rope_tables(jnp.arange(tp, dtype=jnp.int32), TOK_TILE)
    tabs_s = _rope_tables(past + jnp.arange(ts, dtype=jnp.int32), TOK_TILE)
    xp = x_prompt.reshape(bp * tp, D_MODEL)
    xs = x_sample.reshape(bs * ts, D_MODEL)
    pp = p_prompt.reshape(depth, bp * tp, D_PLE)
    ps = p_sample.reshape(depth, bs * ts, D_PLE)
    ck_t = cache_sb_k.transpose(0, 1, 3, 4, 2)
    cv_t = cache_sb_v.transpose(0, 1, 3, 4, 2)
    kr_t = cache_mla_krope.transpose(0, 1, 3, 2)
    zero_conv = jnp.zeros((bp, 2 * N_FF_CHUNKS, CONV_W - 1, FF_CHUNK), F32)
    st_p, st_s = [], []
    kv_p = None
    for i in range(depth):
        lw = _prep_layer(w_in[i], ckv_gain[i], w_uk[i], w_uv[i], w_proj_sb[i], w_proj_mla[i], w_out[i],
                         g_pre_mix[i], g_post_mix[i], g_pre_ffn[i], g_post_ffn[i], w_up[i], conv_w[i],
                         conv_b[i], w_down[i], w_ple_gate[i], w_ple_proj[i])
        q, k_bf, kt_all, vt_all, vt, qm, cm, ct, c_st, r_st = _proj(xp, tp, tabs_p, lw, kv_states=(i, depth, kv_p))
        kv_p = (kt_all, vt_all)
        o_sb = _sb_prompt(q, k_bf, vt, bp, tp)
        o_lat = _mla_prompt(qm, cm, ct, bp, tp)
        x1 = _mix(xp, o_sb, o_lat, lw)
        xp, conv_p = _ffn(x1, pp, i, zero_conv, bp, tp, lw)
        st_p.append((c_st.reshape(bp, tp, KV_LORA), r_st.reshape(bp, tp, MLA_ROPE), _conv_from_chunks(conv_p)))
        q, k_st, v_st, qm, cm, c_st, r_st = _proj(xs, ts, tabs_s, lw)
        o_sb = _sb_sample(q, k_st, v_st, ck_t, cv_t, i, bs, ts)
        o_lat = _mla_sample(qm, cm, cache_mla_ckv, kr_t, i, bs, ts)
        x1 = _mix(xs, o_sb, o_lat, lw)
        xs, conv_s = _ffn(x1, ps, i, _conv_to_chunks(state_ffn_conv[i]), bs, ts, lw)
        st_s.append((k_st.reshape(bs, ts, N_HEADS, SB_HEAD_DIM), v_st.reshape(bs, ts, N_HEADS, SB_HEAD_DIM),
                     c_st.reshape(bs, ts, KV_LORA), r_st.reshape(bs, ts, MLA_ROPE), _conv_from_chunks(conv_s)))
    stack = lambda sts, k: jnp.stack([s[k] for s in sts])
    heads_last = lambda a: a.reshape(depth, bp, N_HEADS, SB_HEAD_DIM, tp).transpose(0, 1, 4, 2, 3)
    return ((xp.reshape(bp, tp, D_MODEL), xs.reshape(bs, ts, D_MODEL), heads_last(kv_p[0]), heads_last(kv_p[1]))
            + tuple(stack(st_p, k) for k in range(3)) + tuple(stack(st_s, k) for k in range(5)))
```

```python
import functools
import math

import jax
import jax.numpy as jnp
import numpy as np
from jax import lax
from jax.experimental import pallas as pl
from jax.experimental.pallas import tpu as pltpu

D_MODEL = 1024
N_HEADS = 8
SB_HEAD_DIM = 64
SB_W = N_HEADS * SB_HEAD_DIM
MLA_NOPE = 64
MLA_ROPE = 32
ROPE_HALF = MLA_ROPE // 2
KV_LORA = 128
MLA_V = 64
D_FF = 2816
CONV_W = 3
CHUNK = 64
D_PLE = 256
ROPE_THETA = 10000.0
EPS = 1e-6
SB_SCALE = SB_HEAD_DIM ** -0.5
MLA_SCALE = (MLA_NOPE + MLA_ROPE) ** -0.5
LOG2E = math.log2(math.e)

LANES = 128
SUBLANES = 8
MXU_DIM = 256
VMEM_LIMIT_BYTES = 56 * 1024 * 1024

BLK = MXU_DIM
TOK_TILE = 2 * MXU_DIM
MIX_TILE = 4 * MXU_DIM
FF_CHUNK = MXU_DIM
N_FF_CHUNKS = D_FF // FF_CHUNK
FF_DOWN_GROUP = 4
QM_W = 2 * LANES
DEC_KEYS = 2048
MLA_DEC_KEYS = 512

BF16 = jnp.bfloat16
F32 = jnp.float32


def _dot(a, b):
    return jnp.dot(a, b, preferred_element_type=F32)


def _dot_nt(a, b):
    return lax.dot_general(a, b, (((1,), (1,)), ((), ())), preferred_element_type=F32)


def _rmsnorm(x, g):
    ms = jnp.mean(x * x, axis=-1, keepdims=True)
    return x * lax.rsqrt(ms + EPS) * g


def _sigmoid(x):
    return 1.0 / (1.0 + jnp.exp(-x))


def _softplus2(z2):
    return jnp.maximum(z2, 0.0) + jnp.log2(1.0 + jnp.exp2(-jnp.abs(z2)))


def _gelu_tanh(x):
    c = math.sqrt(2.0 / math.pi)
    return 0.5 * x * (1.0 + jnp.tanh(c * (x + 0.044715 * (x * x * x))))


def _params(*sem):
    return pltpu.CompilerParams(dimension_semantics=sem, vmem_limit_bytes=VMEM_LIMIT_BYTES)


def _const_spec(shape):
    return pl.BlockSpec(shape, lambda *_: (0,) * len(shape), pipeline_mode=pl.Buffered(1))


def _proj_kernel(x_ref, g_ref, cos_ref, sin_ref, wqkv_ref, wkvt_ref, wqn_ref, wqr_ref, wckv_ref, wkx_ref,
                 gain_ref, wa_ref, wb_ref, eye_ref, *refs, transposed, n_alias, state_slab=0):
    out_refs = refs[n_alias:]
    if transposed:
        (q_ref, kbf_ref, ktst_ref, vtst_ref, vt_ref, qm_ref, cm_ref, ct_ref, cst_ref, rst_ref) = out_refs
    else:
        (q_ref, kst_ref, vst_ref, qm_ref, cm_ref, cst_ref, rst_ref) = out_refs
    hb = _rmsnorm(x_ref[...], g_ref[...]).astype(BF16)
    nsub = x_ref.shape[0] // BLK

    if transposed:
        qk = _dot(hb, wqkv_ref[:, :2 * SB_W])
        q_ref[...] = (qk[:, :SB_W] * SB_SCALE).astype(BF16)
        kb = qk[:, SB_W:].astype(BF16)
        even = lax.broadcasted_iota(jnp.int32, kb.shape, 1) % LANES < SB_HEAD_DIM
        kbf_ref[0] = jnp.where(even, kb, jnp.zeros_like(kb))
        kbf_ref[1] = jnp.where(even, jnp.zeros_like(kb), kb)
        kt = _dot_nt(wkvt_ref[:SB_W, :], hb)
        vt = _dot_nt(wkvt_ref[SB_W:, :], hb)
        for st_ref, val in ((ktst_ref, kt), (vtst_ref, vt)):
            for d in range(st_ref.shape[0]):
                st_ref[d, 0] = val if d == state_slab else jnp.zeros_like(val)
        for s in range(nsub):
            vt_ref[0, s] = vt[:, s * BLK:(s + 1) * BLK].astype(BF16)
    else:
        qkv = _dot(hb, wqkv_ref[...])
        q_ref[...] = (qkv[:, :SB_W] * SB_SCALE).astype(BF16)
        kst_ref[...] = qkv[:, SB_W:2 * SB_W]
        vst_ref[...] = qkv[:, 2 * SB_W:]

    cos = cos_ref[...]
    sin = sin_ref[...]
    qn = _dot(hb, wqn_ref[...]).astype(BF16)
    qr = _dot(hb, wqr_ref[...])
    x1, x2 = qr[:, :LANES], qr[:, LANES:]
    r12 = jnp.concatenate([x1 * cos - x2 * sin, x1 * sin + x2 * cos], axis=1).astype(BF16)
    for h in range(N_HEADS):
        p = h // 2
        qm_ref[h] = (_dot(qn[:, p * LANES:(p + 1) * LANES], wa_ref[h]) + _dot(r12, wb_ref[h])).astype(BF16)

    ckv = _rmsnorm(_dot(hb, wckv_ref[...]), gain_ref[...])
    cst_ref[...] = ckv
    kx = _dot(hb, wkx_ref[...])
    k1, k2 = kx[:, :LANES], kx[:, LANES:]
    kr1 = k1 * cos - k2 * sin
    kr2 = k1 * sin + k2 * cos
    lane = lax.broadcasted_iota(jnp.int32, kr1.shape, 1)
    kr = jnp.where(lane < ROPE_HALF, kr1, pltpu.roll(kr2, ROPE_HALF, axis=1))
    rst_ref[...] = kr[:, :MLA_ROPE]
    ckv_b = ckv.astype(BF16)
    cm_ref[...] = jnp.concatenate([ckv_b, kr.astype(BF16)], axis=1)
    if transposed:
        for s in range(nsub):
            ct_ref[0, s] = _dot_nt(eye_ref[...], ckv_b[s * BLK:(s + 1) * BLK, :]).astype(BF16)


def _proj(x2d, seq_len, tabs, lw, *, kv_states=None):
    n = x2d.shape[0]
    tm = TOK_TILE
    nt = n // tm
    cos, sin = tabs
    tab_blocks = cos.shape[0] // tm
    transposed = kv_states is not None
    tok = lambda w: pl.BlockSpec((tm, w), lambda i: (i, 0))
    tab = pl.BlockSpec((tm, LANES), lambda i: (i % tab_blocks, 0))
    in_specs = [tok(D_MODEL), _const_spec((1, D_MODEL)), tab, tab,
                _const_spec(lw["w_qkv"].shape), _const_spec(lw["w_kvt"].shape),
                _const_spec(lw["w_qn"].shape), _const_spec(lw["w_qr"].shape),
                _const_spec(lw["w_ckv"].shape), _const_spec(lw["w_kx"].shape), _const_spec((1, KV_LORA)),
                _const_spec(lw["wa"].shape), _const_spec(lw["wb"].shape), _const_spec((LANES, LANES))]
    args = [x2d, lw["g_pre_mix"], cos, sin, lw["w_qkv"], lw["w_kvt"], lw["w_qn"], lw["w_qr"], lw["w_ckv"],
            lw["w_kx"], lw["ckv_gain"], lw["wa"], lw["wb"], lw["eye"]]
    aliases = {}
    if transposed:
        layer, depth, prev = kv_states
        assert seq_len % tm == 0
        batch = n // seq_len
        nb = seq_len // tm
        sub = tm // BLK
        state = jax.ShapeDtypeStruct((depth, batch, SB_W, seq_len), F32)
        if prev is None:
            state_slab = layer
            state_spec = pl.BlockSpec((depth, 1, SB_W, tm), lambda i: (0, i // nb, 0, i % nb))
        else:
            state_slab = 0
            state_spec = pl.BlockSpec((1, 1, SB_W, tm), lambda i: (layer, i // nb, 0, i % nb))
        blocks = lambda w: (jax.ShapeDtypeStruct((batch, nb * sub, w, BLK), BF16),
                            pl.BlockSpec((1, sub, w, BLK), lambda i: (i // nb, i % nb, 0, 0)))
        outs = [(jax.ShapeDtypeStruct((n, SB_W), BF16), tok(SB_W)),
                (jax.ShapeDtypeStruct((2, n, SB_W), BF16),
                 pl.BlockSpec((2, tm, SB_W), lambda i: (0, i, 0))),
                (state, state_spec), (state, state_spec), blocks(SB_W)]
        if prev is not None:
            aliases = {len(args): 2, len(args) + 1: 3}
            args += list(prev)
            in_specs += [pl.BlockSpec(memory_space=pl.ANY)] * 2
    else:
        outs = [(jax.ShapeDtypeStruct((n, SB_W), BF16), tok(SB_W)),
                (jax.ShapeDtypeStruct((n, SB_W), F32), tok(SB_W)), (jax.ShapeDtypeStruct((n, SB_W), F32), tok(SB_W))]
    outs += [(jax.ShapeDtypeStruct((N_HEADS, n, QM_W), BF16), pl.BlockSpec((N_HEADS, tm, QM_W), lambda i: (0, i, 0))),
             (jax.ShapeDtypeStruct((n, QM_W), BF16), tok(QM_W))]
    if transposed:
        outs.append(blocks(KV_LORA))
    outs += [(jax.ShapeDtypeStruct((n, KV_LORA), F32), tok(KV_LORA)),
             (jax.ShapeDtypeStruct((n, MLA_ROPE), F32), tok(MLA_ROPE))]
    return pl.pallas_call(
        functools.partial(_proj_kernel, transposed=transposed, n_alias=len(aliases),
                          state_slab=state_slab if transposed else 0),
        grid=(nt,), in_specs=in_specs, out_specs=[o[1] for o in outs], out_shape=[o[0] for o in outs],
        input_output_aliases=aliases,
        compiler_params=_params("parallel"), name="proj_t" if transposed else "proj",
    )(*args)


def _sb_scores(q_ref, k_ref, qrows, j, z_out):
    krows = pl.ds(pl.multiple_of(j * BLK, BLK), BLK)
    for h in range(N_HEADS):
        p, hh = h // 2, h % 2
        cols = slice(p * LANES, (p + 1) * LANES)
        kh = k_ref[hh, 0, krows, cols]
        z_out[h] = _dot_nt(kh, q_ref[0, qrows, cols]) * LOG2E


def _sb_block(z_in, vt_ref, j, acc_ref, car_ref, umat, vis):
    heads = range(N_HEADS)
    mms, lsig, sp0 = [], [], []
    for h in heads:
        z = z_in[h]
        sp = _softplus2(z)
        lsig.append(z - sp)
        if vis is not None:
            sp = jnp.where(vis, sp, 0.0)
        sp0.append(sp[0:1, :])
        mms.append(_dot(umat, sp.astype(BF16)))
    for h in heads:
        car = car_ref[h, 0:1, :]
        a = jnp.exp2(lsig[h] + mms[h] + car)
        if vis is not None:
            a = jnp.where(vis, a, 0.0)
        rows = slice(h * SB_HEAD_DIM, (h + 1) * SB_HEAD_DIM)
        acc_ref[rows, :] += _dot(vt_ref[0, j, rows, :], a.astype(BF16))
        car_ref[h, 0:1, :] = car + mms[h][0:1, :] - sp0[h]


def _sb_kernel(q_ref, k_ref, vt_ref, o_ref, acc_ref, car_ref, z0_ref, z1_ref, *, nblk):
    kidx = lax.broadcasted_iota(jnp.int32, (BLK, BLK), 0)
    qidx = lax.broadcasted_iota(jnp.int32, (BLK, BLK), 1)
    umat = jnp.where(qidx > kidx, -1.0, 0.0).astype(BF16)
    vis = kidx < qidx

    def q_block(i, carry):
        acc_ref[...] = jnp.zeros_like(acc_ref)
        car_ref[...] = jnp.zeros_like(car_ref)
        qrows = pl.ds(pl.multiple_of(i * BLK, BLK), BLK)

        def step(j, z_cur, z_next, mask):
            _sb_scores(q_ref, k_ref, qrows, jnp.maximum(j - 1, 0), z_next)
            _sb_block(z_cur, vt_ref, j, acc_ref, car_ref, umat, mask)

        _sb_scores(q_ref, k_ref, qrows, i, z0_ref)
        step(i, z0_ref, z1_ref, vis)

        def k_pair(m, c):
            j = i - 1 - 2 * m
            step(j, z1_ref, z0_ref, None)
            step(j - 1, z0_ref, z1_ref, None)
            return c

        lax.fori_loop(0, i // 2, k_pair, 0)

        @pl.when(i % 2 == 1)
        def _():
            step(0, z1_ref, z0_ref, None)

        o_ref[0, qrows, :] = acc_ref[...].T.astype(BF16)
        return carry

    lax.fori_loop(0, nblk, q_block, 0)


def _sb_prompt(q, kbf, vt, batch, seq_len):
    nblk = seq_len // BLK
    q3 = q.reshape(batch, seq_len, SB_W)
    k4 = kbf.reshape(2, batch, seq_len, SB_W)
    row = pl.BlockSpec((1, seq_len, SB_W), lambda b: (b, 0, 0))
    out = pl.pallas_call(
        functools.partial(_sb_kernel, nblk=nblk),
        grid=(batch,),
        in_specs=[row, pl.BlockSpec((2, 1, seq_len, SB_W), lambda b: (0, b, 0, 0)),
                  pl.BlockSpec((1, nblk, SB_W, BLK), lambda b: (b, 0, 0, 0))],
        out_specs=row,
        out_shape=jax.ShapeDtypeStruct((batch, seq_len, SB_W), BF16),
        scratch_shapes=[pltpu.VMEM((SB_W, BLK), F32), pltpu.VMEM((N_HEADS, SUBLANES, BLK), F32),
                        pltpu.VMEM((N_HEADS, BLK, BLK), F32), pltpu.VMEM((N_HEADS, BLK, BLK), F32)],
        compiler_params=_params("parallel"), name="sb_prompt",
    )(q3, k4, vt)
    return out.reshape(batch * seq_len, SB_W)


def _mla_scores(q_ref, c_ref, qrows, j, s_out):
    c = c_ref[pl.ds(pl.multiple_of(j * BLK, BLK), BLK), :]
    for h in range(N_HEADS):
        s_out[h] = _dot_nt(c, q_ref[h, qrows, :])


def _mla_block(s_in, ct_ref, j, acc_ref, st_ref, vis):
    ct = ct_ref[0, j]
    for h in range(N_HEADS):
        s = s_in[h]
        if vis is not None:
            s = jnp.where(vis, s, -jnp.inf)
        m_old = st_ref[h, 0:1, :]
        m_new = jnp.maximum(m_old, jnp.max(s, axis=0, keepdims=True))
        p = jnp.exp2((s - m_new) * (MLA_SCALE * LOG2E))
        alpha = jnp.exp2((m_old - m_new) * (MLA_SCALE * LOG2E))
        st_ref[h, 1:2, :] = alpha * st_ref[h, 1:2, :] + jnp.sum(p, axis=0, keepdims=True)
        st_ref[h, 0:1, :] = m_new
        acc_ref[h] = alpha * acc_ref[h] + _dot(ct, p.astype(BF16))


def _mla_kernel(q_ref, c_ref, ct_ref, o_ref, acc_ref, st_ref, s0_ref, s1_ref, *, nblk):
    kidx = lax.broadcasted_iota(jnp.int32, (BLK, BLK), 0)
    qidx = lax.broadcasted_iota(jnp.int32, (BLK, BLK), 1)
    vis = (kidx // CHUNK) <= (qidx // CHUNK)

    def q_block(i, carry):
        acc_ref[...] = jnp.zeros_like(acc_ref)
        for h in range(N_HEADS):
            st_ref[h, 0:1, :] = jnp.full((1, BLK), -jnp.inf, F32)
            st_ref[h, 1:2, :] = jnp.zeros((1, BLK), F32)
        rows = pl.ds(pl.multiple_of(i * BLK, BLK), BLK)
        def step(j, j_next, s_cur, s_next, mask):
            _mla_scores(q_ref, c_ref, rows, j_next, s_next)
            _mla_block(s_cur, ct_ref, j, acc_ref, st_ref, mask)

        nxt = lambda j: jnp.minimum(j + 1, nblk - 1)
        _mla_scores(q_ref, c_ref, rows, i, s0_ref)
        step(i, 0, s0_ref, s1_ref, vis)

        def k_pair(m, c):
            j = 2 * m
            step(j, nxt(j), s1_ref, s0_ref, None)
            step(j + 1, nxt(j + 1), s0_ref, s1_ref, None)
            return c

        lax.fori_loop(0, i // 2, k_pair, 0)

        @pl.when(i % 2 == 1)
        def _():
            step(i - 1, nxt(i - 1), s1_ref, s0_ref, None)

        for h in range(N_HEADS):
            o = acc_ref[h] * (1.0 / st_ref[h, 1:2, :])
            o_ref[rows, h * KV_LORA:(h + 1) * KV_LORA] = o.T.astype(BF16)
        return carry

    lax.fori_loop(0, nblk, q_block, 0)


def _mla_prompt(qm, cm, ct, batch, seq_len):
    nblk = seq_len // BLK
    return pl.pallas_call(
        functools.partial(_mla_kernel, nblk=nblk),
        grid=(batch,),
        in_specs=[pl.BlockSpec((N_HEADS, seq_len, QM_W), lambda b: (0, b, 0)),
                  pl.BlockSpec((seq_len, QM_W), lambda b: (b, 0)),
                  pl.BlockSpec((1, nblk, KV_LORA, BLK), lambda b: (b, 0, 0, 0))],
        out_specs=pl.BlockSpec((seq_len, N_HEADS * KV_LORA), lambda b: (b, 0)),
        out_shape=jax.ShapeDtypeStruct((batch * seq_len, N_HEADS * KV_LORA), BF16),
        scratch_shapes=[pltpu.VMEM((N_HEADS, KV_LORA, BLK), F32), pltpu.VMEM((N_HEADS, SUBLANES, BLK), F32),
                        pltpu.VMEM((N_HEADS, BLK, BLK), F32), pltpu.VMEM((N_HEADS, BLK, BLK), F32)],
        compiler_params=_params("parallel"), name="mla_prompt",
    )(qm, cm, ct)


def _sbdec_blocks(score, av, nblocks, t, acc_ref, car_ref, wn, vis):
    heads = range(N_HEADS)
    zs = [jnp.concatenate([score(b, h) for h in heads], axis=0) * LOG2E
          for b in range(nblocks)]
    mms, lsig, sp0 = [], [], []
    for b in range(nblocks):
        sp = _softplus2(zs[b])
        lsig.append(zs[b] - sp)
        if vis is not None:
            sp = jnp.where(vis, sp, 0.0)
        sp0.append(sp[:, 0:1])
        mms.append(_dot(sp.astype(BF16), wn))
    car = car_ref[:, 0:1]
    for b in range(nblocks):
        a = jnp.exp2(lsig[b] + mms[b] + car)
        if vis is not None:
            a = jnp.where(vis, a, 0.0)
        a = a.astype(BF16)
        for h in heads:
            acc_ref[h] += av(b, h, a[h * t:(h + 1) * t, :])
        car = car + mms[b][:, 0:1] - sp0[b]
    car_ref[...] = jnp.broadcast_to(car, car_ref.shape)


def _sbdec_kernel(q_ref, kn_ref, vn_ref, ck_ref, cv_ref, o_ref, qh_ref, acc_ref, car_ref, *, t_new):
    j = pl.program_id(1)
    hcols = lambda h: slice(h * SB_HEAD_DIM, (h + 1) * SB_HEAD_DIM)

    @pl.when(j == 0)
    def _():
        acc_ref[...] = jnp.zeros_like(acc_ref)
        car_ref[...] = jnp.zeros_like(car_ref)
        for h in range(N_HEADS):
            qh_ref[h] = q_ref[:, hcols(h)]
        kp = lax.broadcasted_iota(jnp.int32, (t_new, t_new), 0)
        kc = lax.broadcasted_iota(jnp.int32, (t_new, t_new), 1)
        wn_new = jnp.where(kp > kc, -1.0, 0.0).astype(BF16)
        qi = lax.rem(lax.broadcasted_iota(jnp.int32, (N_HEADS * t_new, t_new), 0), t_new)
        ki = lax.broadcasted_iota(jnp.int32, (N_HEADS * t_new, t_new), 1)
        _sbdec_blocks(lambda b, h: _dot_nt(qh_ref[h], kn_ref[:, hcols(h)].astype(BF16)),
                      lambda b, h, a: _dot(a, vn_ref[:, hcols(h)].astype(BF16)),
                      1, t_new, acc_ref, car_ref, wn_new, ki < qi)

    kp = lax.broadcasted_iota(jnp.int32, (BLK, BLK), 0)
    kc = lax.broadcasted_iota(jnp.int32, (BLK, BLK), 1)
    wn = jnp.where(kp > kc, -1.0, 0.0).astype(BF16)
    nsub = DEC_KEYS // BLK
    keys = lambda b: slice((nsub - 1 - b) * BLK, (nsub - b) * BLK)
    _sbdec_blocks(lambda b, h: _dot(qh_ref[h], ck_ref[0, 0, h, :, keys(b)].astype(BF16)),
                  lambda b, h, a: _dot_nt(a, cv_ref[0, 0, h, :, keys(b)].astype(BF16)),
                  nsub, t_new, acc_ref, car_ref, wn, None)

    @pl.when(j == pl.num_programs(1) - 1)
    def _():
        for h in range(N_HEADS):
            o_ref[:, hcols(h)] = acc_ref[h].astype(BF16)


def _sb_sample(q, k_new, v_new, cache_k, cache_v, layer, batch, t_new):
    past = cache_k.shape[4]
    assert past % DEC_KEYS == 0
    nkb = past // DEC_KEYS
    tok = lambda: pl.BlockSpec((t_new, SB_W), lambda b, j: (b, 0))
    cache = lambda: pl.BlockSpec((1, 1, N_HEADS, SB_HEAD_DIM, DEC_KEYS), lambda b, j: (layer, b, 0, 0, nkb - 1 - j))
    return pl.pallas_call(
        functools.partial(_sbdec_kernel, t_new=t_new),
        grid=(batch, nkb),
        in_specs=[tok(), tok(), tok(), cache(), cache()],
        out_specs=tok(),
        out_shape=jax.ShapeDtypeStruct((batch * t_new, SB_W), BF16),
        scratch_shapes=[pltpu.VMEM((N_HEADS, t_new, SB_HEAD_DIM), BF16),
                        pltpu.VMEM((N_HEADS, t_new, SB_HEAD_DIM), F32),
                        pltpu.VMEM((N_HEADS * t_new, LANES), F32)],
        compiler_params=_params("parallel", "arbitrary"), name="sb_sample",
    )(q, k_new, v_new, cache_k, cache_v)


def _mladec_kernel(q_ref, cn_ref, ck_ref, kr_ref, o_ref, *, t_new, past, new_visible):
    q = q_ref[...].reshape(N_HEADS * t_new, QM_W)
    q_lat = q[:, :KV_LORA]
    q_rope = q[:, KV_LORA:KV_LORA + MLA_ROPE]
    c2 = MLA_SCALE * LOG2E

    cn = cn_ref[...]
    s = _dot_nt(q, cn)
    if not new_visible:
        qi = past + lax.broadcasted_iota(jnp.int32, s.shape, 0) % t_new
        ki = past + lax.broadcasted_iota(jnp.int32, s.shape, 1)
        s = jnp.where((ki // CHUNK) <= (qi // CHUNK), s, -jnp.inf)
    chunks = [slice(c * MLA_DEC_KEYS, (c + 1) * MLA_DEC_KEYS) for c in range(past // MLA_DEC_KEYS)]
    cks = [ck_ref[0, 0, rows, :].astype(BF16) for rows in chunks]
    ss = [_dot_nt(q_lat, ck) + _dot(q_rope, kr_ref[0, 0, :, rows].astype(BF16))
          for ck, rows in zip(cks, chunks)]
    m = jnp.max(s, axis=1, keepdims=True)
    for sc in ss:
        m = jnp.maximum(m, jnp.max(sc, axis=1, keepdims=True))
    p = jnp.exp2((s - m) * c2)
    l = jnp.sum(p, axis=1, keepdims=True)
    acc = _dot(p.astype(BF16), cn[:, :KV_LORA])
    for sc, ck in zip(ss, cks):
        p = jnp.exp2((sc - m) * c2)
        l = l + jnp.sum(p, axis=1, keepdims=True)
        acc = acc + _dot(p.astype(BF16), ck)
    o = (acc * (1.0 / l)).astype(BF16)
    for h in range(N_HEADS):
        o_ref[:, h * KV_LORA:(h + 1) * KV_LORA] = o[h * t_new:(h + 1) * t_new, :]


def _mla_sample(qm, cm, cache_ckv, cache_kr, layer, batch, t_new):
    past = cache_ckv.shape[2]
    assert past % MLA_DEC_KEYS == 0 and past % CHUNK == 0
    q_pos = past + np.arange(t_new)
    new_visible = bool(((q_pos[None, :] // CHUNK) <= (q_pos[:, None] // CHUNK)).all())
    return pl.pallas_call(
        functools.partial(_mladec_kernel, t_new=t_new, past=past, new_visible=new_visible),
        grid=(batch,),
        in_specs=[pl.BlockSpec((N_HEADS, t_new, QM_W), lambda b: (0, b, 0)),
                  pl.BlockSpec((t_new, QM_W), lambda b: (b, 0)),
                  pl.BlockSpec((1, 1, past, KV_LORA), lambda b: (layer, b, 0, 0)),
                  pl.BlockSpec((1, 1, MLA_ROPE, past), lambda b: (layer, b, 0, 0))],
        out_specs=pl.BlockSpec((t_new, N_HEADS * KV_LORA), lambda b: (b, 0)),
        out_shape=jax.ShapeDtypeStruct((batch * t_new, N_HEADS * KV_LORA), BF16),
        compiler_params=_params("parallel"), name="mla_sample",
    )(qm, cm, cache_ckv, cache_kr)


def _mix_kernel(x_ref, gpre_ref, osb_ref, olat_ref, wg_ref, wpsb_ref, wuv_ref, wpmla_ref, wout_ref,
                gpost_ref, o_ref):
    x = x_ref[...]
    hb = _rmsnorm(x, gpre_ref[...]).astype(BF16)
    gates = _dot(hb, wg_ref[...])
    y_a = _dot(osb_ref[...], wpsb_ref[...])
    o_mla = _dot(olat_ref[...], wuv_ref[...]).astype(BF16)
    y_b = _dot(o_mla, wpmla_ref[...])
    mixin = _sigmoid(gates[:, :D_MODEL]) * y_a + _sigmoid(gates[:, D_MODEL:]) * y_b
    mix = _dot(mixin.astype(BF16), wout_ref[...])
    o_ref[...] = x + _rmsnorm(mix, gpost_ref[...])


def _mix(x2d, o_sb, o_lat, lw):
    n = x2d.shape[0]
    tm = MIX_TILE
    assert n % tm == 0
    tok = lambda w: pl.BlockSpec((tm, w), lambda i: (i, 0))
    return pl.pallas_call(
        _mix_kernel, grid=(n // tm,),
        in_specs=[tok(D_MODEL), _const_spec((1, D_MODEL)), tok(SB_W), tok(N_HEADS * KV_LORA),
                  _const_spec(lw["w_gates"].shape), _const_spec(lw["w_proj_sb"].shape),
                  _const_spec(lw["w_uv_bd"].shape), _const_spec(lw["w_proj_mla"].shape),
                  _const_spec(lw["w_out"].shape), _const_spec((1, D_MODEL))],
        out_specs=tok(D_MODEL), out_shape=jax.ShapeDtypeStruct((n, D_MODEL), F32),
        compiler_params=_params("parallel"), name="mix",
    )(x2d, lw["g_pre_mix"], o_sb, o_lat, lw["w_gates"], lw["w_proj_sb"], lw["w_uv_bd"], lw["w_proj_mla"],
      lw["w_out"], lw["g_post_mix"])


def _ffn_kernel(x_ref, gpre_ref, wup_ref, cw_ref, past_ref, wdown_ref, gpost_ref, wpg_ref, p_ref, wpp_ref,
                o_ref, conv_ref, *u_scr, nseq, rows):
    t = pl.program_id(1)
    span = SUBLANES + rows
    lead = SUBLANES - (CONV_W - 1)

    @pl.when(t == 0)
    def _():
        for c in range(2 * N_FF_CHUNKS):
            for s in range(nseq):
                u_scr[c][s * span + lead:s * span + SUBLANES, :] = past_ref[s, c]

    x = x_ref[...]
    hb = _rmsnorm(x, gpre_ref[...]).astype(BF16)

    def up(c):
        for cc in (c, N_FF_CHUNKS + c):
            u = _dot(hb, wup_ref[:, cc * FF_CHUNK:(cc + 1) * FF_CHUNK])
            for s in range(nseq):
                u_scr[cc][s * span + SUBLANES:(s + 1) * span, :] = u[s * rows:(s + 1) * rows, :]

    def conv(c):
        cw = cw_ref[c]
        u = u_scr[c]
        out = []
        for s in range(nseq):
            full = u[s * span:(s + 1) * span, :]
            uc = cw[3:4, :]
            for i in range(CONV_W):
                back = CONV_W - 1 - i
                shifted = full if back == 0 else pltpu.roll(full, back, axis=0)
                uc = uc + shifted[SUBLANES:, :] * cw[i:i + 1, :]
            conv_ref[s, c] = u[(s + 1) * span - (CONV_W - 1):(s + 1) * span, :]
            u[s * span:s * span + SUBLANES, :] = u[s * span + rows:(s + 1) * span, :]
            out.append(uc)
        return out[0] if nseq == 1 else jnp.concatenate(out, axis=0)

    up(0)
    f = None
    acts = []
    for c in range(N_FF_CHUNKS):
        if c + 1 < N_FF_CHUNKS:
            up(c + 1)
        acts.append((_gelu_tanh(conv(c)) * conv(N_FF_CHUNKS + c)).astype(BF16))
        if len(acts) == FF_DOWN_GROUP or c + 1 == N_FF_CHUNKS:
            k1 = (c + 1) * FF_CHUNK
            k0 = k1 - len(acts) * FF_CHUNK
            part = _dot(jnp.concatenate(acts, axis=1), wdown_ref[k0:k1, :])
            f = part if f is None else f + part
            acts = []
    x2 = x + _rmsnorm(f, gpost_ref[...])
    ple = _sigmoid(_dot(x2.astype(BF16), wpg_ref[...])) * _dot(p_ref[0].astype(BF16), wpp_ref[...])
    o_ref[...] = x2 + ple


def _ffn(x2d, p3d, layer, conv_past, batch, seq_len, lw):
    tm = TOK_TILE
    rows = min(tm, seq_len)
    nseq = tm // rows
    nt = seq_len // rows
    assert batch % nseq == 0 and (nseq == 1 or nt == 1)
    nch = 2 * N_FF_CHUNKS
    tok = lambda w: pl.BlockSpec((tm, w), lambda b, t: (b * nt + t, 0))
    ple = pl.BlockSpec((1, tm, D_PLE), lambda b, t: (layer, b * nt + t, 0))
    state = pl.BlockSpec((nseq, nch, CONV_W - 1, FF_CHUNK), lambda b, t: (b, 0, 0, 0))
    return pl.pallas_call(
        functools.partial(_ffn_kernel, nseq=nseq, rows=rows),
        grid=(batch // nseq, nt),
        in_specs=[tok(D_MODEL), _const_spec((1, D_MODEL)), _const_spec(lw["w_up"].shape),
                  _const_spec(lw["conv"].shape), state, _const_spec(lw["w_down"].shape),
                  _const_spec((1, D_MODEL)), _const_spec(lw["w_ple_gate"].shape), ple,
                  _const_spec(lw["w_ple_proj"].shape)],
        out_specs=[tok(D_MODEL), state],
        out_shape=[jax.ShapeDtypeStruct(x2d.shape, F32), jax.ShapeDtypeStruct(conv_past.shape, F32)],
        scratch_shapes=[pltpu.VMEM((nseq * (SUBLANES + rows), FF_CHUNK), F32) for _ in range(nch)],
        compiler_params=_params("arbitrary", "arbitrary"), name="ffn",
    )(x2d, lw["g_pre_ffn"], lw["w_up"], lw["conv"], conv_past, lw["w_down"], lw["g_post_ffn"],
      lw["w_ple_gate"], p3d, lw["w_ple_proj"])


def _prep_layer(w_in, ckv_gain, w_uk, w_uv, w_proj_sb, w_proj_mla, w_out, g_pre_mix, g_post_mix, g_pre_ffn,
                g_post_ffn, w_up, conv_w, conv_b, w_down, w_ple_gate, w_ple_proj):
    o_qm = 3 * SB_W
    o_ckv = o_qm + N_HEADS * (MLA_NOPE + MLA_ROPE)
    o_kr = o_ckv + KV_LORA
    o_g = o_kr + MLA_ROPE
    wqm = w_in[:, o_qm:o_ckv].reshape(D_MODEL, N_HEADS, MLA_NOPE + MLA_ROPE)
    w_qn = wqm[:, :, :MLA_NOPE].reshape(D_MODEL, N_HEADS * MLA_NOPE)
    w_qr = jnp.concatenate([wqm[:, :, MLA_NOPE:MLA_NOPE + ROPE_HALF].reshape(D_MODEL, LANES),
                            wqm[:, :, MLA_NOPE + ROPE_HALF:].reshape(D_MODEL, LANES)], axis=1)
    w_kr = w_in[:, o_kr:o_g]
    w_kx = jnp.zeros((D_MODEL, 2 * LANES), F32)
    w_kx = w_kx.at[:, :ROPE_HALF].set(w_kr[:, :ROPE_HALF]).at[:, LANES:LANES + ROPE_HALF].set(w_kr[:, ROPE_HALF:])
    wa = jnp.zeros((N_HEADS, LANES, QM_W), F32)
    wb = np.zeros((N_HEADS, 2 * LANES, QM_W), np.float32)
    for h in range(N_HEADS):
        r0 = (h % 2) * MLA_NOPE
        wa = wa.at[h, r0:r0 + MLA_NOPE, :KV_LORA].set(w_uk[:, h, :].T)
        for i in range(ROPE_HALF):
            wb[h, h * ROPE_HALF + i, KV_LORA + i] = 1.0
            wb[h, LANES + h * ROPE_HALF + i, KV_LORA + ROPE_HALF + i] = 1.0
    w_uv_bd = jnp.zeros((N_HEADS * KV_LORA, N_HEADS * MLA_V), F32)
    for h in range(N_HEADS):
        w_uv_bd = w_uv_bd.at[h * KV_LORA:(h + 1) * KV_LORA, h * MLA_V:(h + 1) * MLA_V].set(w_uv[:, h, :])
    nch = 2 * N_FF_CHUNKS
    conv = jnp.concatenate([conv_w, conv_b[None, :], jnp.zeros((SUBLANES - CONV_W - 1, 2 * D_FF), F32)], axis=0)
    row = lambda g: g.reshape(1, -1)
    return dict(
        w_qkv=w_in[:, :o_qm].astype(BF16), w_kvt=w_in[:, SB_W:o_qm].T.astype(BF16),
        w_qn=w_qn.astype(BF16), w_qr=w_qr.astype(BF16),
        w_ckv=w_in[:, o_ckv:o_kr].astype(BF16), w_kx=w_kx.astype(BF16), ckv_gain=row(ckv_gain),
        wa=wa.astype(BF16), wb=jnp.asarray(wb, BF16), eye=jnp.eye(LANES, dtype=BF16),
        w_gates=w_in[:, o_g:].astype(BF16), w_proj_sb=w_proj_sb.astype(BF16), w_uv_bd=w_uv_bd.astype(BF16),
        w_proj_mla=w_proj_mla.astype(BF16), w_out=w_out.astype(BF16),
        g_pre_mix=row(g_pre_mix), g_post_mix=row(g_post_mix), g_pre_ffn=row(g_pre_ffn), g_post_ffn=row(g_post_ffn),
        w_up=w_up.astype(BF16),
        conv=conv.reshape(SUBLANES, nch, FF_CHUNK).transpose(1, 0, 2),
        w_down=w_down.astype(BF16),
        w_ple_gate=w_ple_gate.astype(BF16), w_ple_proj=w_ple_proj.astype(BF16))


def _rope_tables(pos, rows):
    freqs = jnp.power(ROPE_THETA, -jnp.arange(ROPE_HALF, dtype=F32) / ROPE_HALF)
    ang = pos.astype(F32)[:, None] * freqs[None, :]
    reps = max(1, rows // pos.shape[0])
    tile = lambda a: jnp.tile(a, (reps, LANES // ROPE_HALF))
    return tile(jnp.cos(ang)), tile(jnp.sin(ang))


def _conv_to_chunks(state):
    b = state.shape[0]
    return state.reshape(b, CONV_W - 1, 2 * N_FF_CHUNKS, FF_CHUNK).transpose(0, 2, 1, 3)


def _conv_from_chunks(state):
    b = state.shape[0]
    return state.transpose(0, 2, 1, 3).reshape(b, CONV_W - 1, 2 * D_FF)


def kernel(x_prompt, x_sample, p_prompt, p_sample, cache_sb_k, cache_sb_v, cache_mla_ckv, cache_mla_krope,
           state_ffn_conv, w_in, ckv_gain, w_uk, w_uv, w_proj_sb, w_proj_mla, w_out, g_pre_mix, g_post_mix,
           g_pre_ffn, g_post_ffn, w_up, conv_w, conv_b, w_down, w_ple_gate, w_ple_proj):
    bp, tp, _ = x_prompt.shape
    bs, ts, _ = x_sample.shape
    depth = w_in.shape[0]
    past = cache_sb_k.shape[2]
    assert tp % TOK_TILE == 0 and (bs * ts) % TOK_TILE == 0 and TOK_TILE % ts == 0
    tabs_p = _rope_tables(jnp.arange(tp, dtype=jnp.int32), TOK_TILE)
    tabs_s = _rope_tables(past + jnp.arange(ts, dtype=jnp.int32), TOK_TILE)
    xp = x_prompt.reshape(bp * tp, D_MODEL)
    xs = x_sample.reshape(bs * ts, D_MODEL)
    pp = p_prompt.reshape(depth, bp * tp, D_PLE)
    ps = p_sample.reshape(depth, bs * ts, D_PLE)
    ck_t = cache_sb_k.transpose(0, 1, 3, 4, 2)
    cv_t = cache_sb_v.transpose(0, 1, 3, 4, 2)
    kr_t = cache_mla_krope.transpose(0, 1, 3, 2)
    zero_conv = jnp.zeros((bp, 2 * N_FF_CHUNKS, CONV_W - 1, FF_CHUNK), F32)
    st_p, st_s = [], []
    kv_p = None
    for i in range(depth):
        lw = _prep_layer(w_in[i], ckv_gain[i], w_uk[i], w_uv[i], w_proj_sb[i], w_proj_mla[i], w_out[i],
                         g_pre_mix[i], g_post_mix[i], g_pre_ffn[i], g_post_ffn[i], w_up[i], conv_w[i],
                         conv_b[i], w_down[i], w_ple_gate[i], w_ple_proj[i])
        q, k_bf, kt_all, vt_all, vt, qm, cm, ct, c_st, r_st = _proj(xp, tp, tabs_p, lw, kv_states=(i, depth, kv_p))
        kv_p = (kt_all, vt_all)
        o_sb = _sb_prompt(q, k_bf, vt, bp, tp)
        o_lat = _mla_prompt(qm, cm, ct, bp, tp)
        x1 = _mix(xp, o_sb, o_lat, lw)
        xp, conv_p = _ffn(x1, pp, i, zero_conv, bp, tp, lw)
        st_p.append((c_st.reshape(bp, tp, KV_LORA), r_st.reshape(bp, tp, MLA_ROPE), _conv_from_chunks(conv_p)))
        q, k_st, v_st, qm, cm, c_st, r_st = _proj(xs, ts, tabs_s, lw)
        o_sb = _sb_sample(q, k_st, v_st, ck_t, cv_t, i, bs, ts)
        o_lat = _mla_sample(qm, cm, cache_mla_ckv, kr_t, i, bs, ts)
        x1 = _mix(xs, o_sb, o_lat, lw)
        xs, conv_s = _ffn(x1, ps, i, _conv_to_chunks(state_ffn_conv[i]), bs, ts, lw)
        st_s.append((k_st.reshape(bs, ts, N_HEADS, SB_HEAD_DIM), v_st.reshape(bs, ts, N_HEADS, SB_HEAD_DIM),
                     c_st.reshape(bs, ts, KV_LORA), r_st.reshape(bs, ts, MLA_ROPE), _conv_from_chunks(conv_s)))
    stack = lambda sts, k: jnp.stack([s[k] for s in sts])
    heads_last = lambda a: a.reshape(depth, bp, N_HEADS, SB_HEAD_DIM, tp).transpose(0, 1, 4, 2, 3)
    return ((xp.reshape(bp, tp, D_MODEL), xs.reshape(bs, ts, D_MODEL), heads_last(kv_p[0]), heads_last(kv_p[1]))
            + tuple(stack(st_p, k) for k in range(3)) + tuple(stack(st_s, k) for k in range(5)))
```

```python
import functools
import math

import jax
import jax.numpy as jnp
import numpy as np
from jax import lax
from jax.experimental import pallas as pl
from jax.experimental.pallas import tpu as pltpu

D_MODEL = 1024
N_HEADS = 8
SB_HEAD_DIM = 64
SB_W = N_HEADS * SB_HEAD_DIM
MLA_NOPE = 64
MLA_ROPE = 32
ROPE_HALF = MLA_ROPE // 2
KV_LORA = 128
MLA_V = 64
D_FF = 2816
CONV_W = 3
CHUNK = 64
D_PLE = 256
ROPE_THETA = 10000.0
EPS = 1e-6
SB_SCALE = SB_HEAD_DIM ** -0.5
MLA_SCALE = (MLA_NOPE + MLA_ROPE) ** -0.5
LOG2E = math.log2(math.e)

LANES = 128
SUBLANES = 8
MXU_DIM = 256
VMEM_LIMIT_BYTES = 56 * 1024 * 1024

BLK = MXU_DIM
TOK_TILE = 2 * MXU_DIM
MIX_TILE = 4 * MXU_DIM
FF_CHUNK = MXU_DIM
N_FF_CHUNKS = D_FF // FF_CHUNK
FF_DOWN_GROUP = 4
QM_W = 2 * LANES
DEC_KEYS = 2048
MLA_DEC_KEYS = 512

BF16 = jnp.bfloat16
F32 = jnp.float32


def _dot(a, b):
    return jnp.dot(a, b, preferred_element_type=F32)


def _dot_nt(a, b):
    return lax.dot_general(a, b, (((1,), (1,)), ((), ())), preferred_element_type=F32)


def _rmsnorm(x, g):
    ms = jnp.mean(x * x, axis=-1, keepdims=True)
    return x * lax.rsqrt(ms + EPS) * g


def _sigmoid(x):
    return 1.0 / (1.0 + jnp.exp(-x))


def _softplus2(z2):
    return jnp.maximum(z2, 0.0) + jnp.log2(1.0 + jnp.exp2(-jnp.abs(z2)))


def _gelu_tanh(x):
    c = math.sqrt(2.0 / math.pi)
    return 0.5 * x * (1.0 + jnp.tanh(c * (x + 0.044715 * (x * x * x))))


def _params(*sem):
    return pltpu.CompilerParams(dimension_semantics=sem, vmem_limit_bytes=VMEM_LIMIT_BYTES)


def _const_spec(shape):
    return pl.BlockSpec(shape, lambda *_: (0,) * len(shape), pipeline_mode=pl.Buffered(1))


def _proj_kernel(x_ref, g_ref, cos_ref, sin_ref, wqkv_ref, wkvt_ref, wqn_ref, wqr_ref, wckv_ref, wkx_ref,
                 gain_ref, wa_ref, wb_ref, eye_ref, *refs, transposed, n_alias, state_slab=0):
    out_refs = refs[n_alias:]
    if transposed:
        (q_ref, kbf_ref, ktst_ref, vtst_ref, vt_ref, qm_ref, cm_ref, ct_ref, cst_ref, rst_ref) = out_refs
    else:
        (q_ref, kst_ref, vst_ref, qm_ref, cm_ref, cst_ref, rst_ref) = out_refs
    hb = _rmsnorm(x_ref[...], g_ref[...]).astype(BF16)
    nsub = x_ref.shape[0] // BLK

    if transposed:
        qk = _dot(hb, wqkv_ref[:, :2 * SB_W])
        q_ref[...] = (qk[:, :SB_W] * SB_SCALE).astype(BF16)
        kb = qk[:, SB_W:].astype(BF16)
        even = lax.broadcasted_iota(jnp.int32, kb.shape, 1) % LANES < SB_HEAD_DIM
        kbf_ref[0] = jnp.where(even, kb, jnp.zeros_like(kb))
        kbf_ref[1] = jnp.where(even, jnp.zeros_like(kb), kb)
        kt = _dot_nt(wkvt_ref[:SB_W, :], hb)
        vt = _dot_nt(wkvt_ref[SB_W:, :], hb)
        for st_ref, val in ((ktst_ref, kt), (vtst_ref, vt)):
            for d in range(st_ref.shape[0]):
                st_ref[d, 0] = val if d == state_slab else jnp.zeros_like(val)
        for s in range(nsub):
            vt_ref[0, s] = vt[:, s * BLK:(s + 1) * BLK].astype(BF16)
    else:
        qkv = _dot(hb, wqkv_ref[...])
        q_ref[...] = (qkv[:, :SB_W] * SB_SCALE).astype(BF16)
        kst_ref[...] = qkv[:, SB_W:2 * SB_W]
        vst_ref[...] = qkv[:, 2 * SB_W:]

    cos = cos_ref[...]
    sin = sin_ref[...]
    qn = _dot(hb, wqn_ref[...]).astype(BF16)
    qr = _dot(hb, wqr_ref[...])
    x1, x2 = qr[:, :LANES], qr[:, LANES:]
    r12 = jnp.concatenate([x1 * cos - x2 * sin, x1 * sin + x2 * cos], axis=1).astype(BF16)
    for h in range(N_HEADS):
        p = h // 2
        qm_ref[h] = (_dot(qn[:, p * LANES:(p + 1) * LANES], wa_ref[h]) + _dot(r12, wb_ref[h])).astype(BF16)

    ckv = _rmsnorm(_dot(hb, wckv_ref[...]), gain_ref[...])
    cst_ref[...] = ckv
    kx = _dot(hb, wkx_ref[...])
    k1, k2 = kx[:, :LANES], kx[:, LANES:]
    kr1 = k1 * cos - k2 * sin
    kr2 = k1 * sin + k2 * cos
    lane = lax.broadcasted_iota(jnp.int32, kr1.shape, 1)
    kr = jnp.where(lane < ROPE_HALF, kr1, pltpu.roll(kr2, ROPE_HALF, axis=1))
    rst_ref[...] = kr[:, :MLA_ROPE]
    ckv_b = ckv.astype(BF16)
    cm_ref[...] = jnp.concatenate([ckv_b, kr.astype(BF16)], axis=1)
    if transposed:
        for s in range(nsub):
            ct_ref[0, s] = _dot_nt(eye_ref[...], ckv_b[s * BLK:(s + 1) * BLK, :]).astype(BF16)


def _proj(x2d, seq_len, tabs, lw, *, kv_states=None):
    n = x2d.shape[0]
    tm = TOK_TILE
    nt = n // tm
    cos, sin = tabs
    tab_blocks = cos.shape[0] // tm
    transposed = kv_states is not None
    tok = lambda w: pl.BlockSpec((tm, w), lambda i: (i, 0))
    tab = pl.BlockSpec((tm, LANES), lambda i: (i % tab_blocks, 0))
    in_specs = [tok(D_MODEL), _const_spec((1, D_MODEL)), tab, tab,
                _const_spec(lw["w_qkv"].shape), _const_spec(lw["w_kvt"].shape),
                _const_spec(lw["w_qn"].shape), _const_spec(lw["w_qr"].shape),
                _const_spec(lw["w_ckv"].shape), _const_spec(lw["w_kx"].shape), _const_spec((1, KV_LORA)),
                _const_spec(lw["wa"].shape), _const_spec(lw["wb"].shape), _const_spec((LANES, LANES))]
    args = [x2d, lw["g_pre_mix"], cos, sin, lw["w_qkv"], lw["w_kvt"], lw["w_qn"], lw["w_qr"], lw["w_ckv"],
            lw["w_kx"], lw["ckv_gain"], lw["wa"], lw["wb"], lw["eye"]]
    aliases = {}
    if transposed:
        layer, depth, prev = kv_states
        assert seq_len % tm == 0
        batch = n // seq_len
        nb = seq_len // tm
        sub = tm // BLK
        state = jax.ShapeDtypeStruct((depth, batch, SB_W, seq_len), F32)
        if prev is None:
            state_slab = layer
            state_spec = pl.BlockSpec((depth, 1, SB_W, tm), lambda i: (0, i // nb, 0, i % nb))
        else:
            state_slab = 0
            state_spec = pl.BlockSpec((1, 1, SB_W, tm), lambda i: (layer, i // nb, 0, i % nb))
        blocks = lambda w: (jax.ShapeDtypeStruct((batch, nb * sub, w, BLK), BF16),
                            pl.BlockSpec((1, sub, w, BLK), lambda i: (i // nb, i % nb, 0, 0)))
        outs = [(jax.ShapeDtypeStruct((n, SB_W), BF16), tok(SB_W)),
                (jax.ShapeDtypeStruct((2, n, SB_W), BF16),
                 pl.BlockSpec((2, tm, SB_W), lambda i: (0, i, 0))),
                (state, state_spec), (state, state_spec), blocks(SB_W)]
        if prev is not None:
            aliases = {len(args): 2, len(args) + 1: 3}
            args += list(prev)
            in_specs += [pl.BlockSpec(memory_space=pl.ANY)] * 2
    else:
        outs = [(jax.ShapeDtypeStruct((n, SB_W), BF16), tok(SB_W)),
                (jax.ShapeDtypeStruct((n, SB_W), F32), tok(SB_W)), (jax.ShapeDtypeStruct((n, SB_W), F32), tok(SB_W))]
    outs += [(jax.ShapeDtypeStruct((N_HEADS, n, QM_W), BF16), pl.BlockSpec((N_HEADS, tm, QM_W), lambda i: (0, i, 0))),
             (jax.ShapeDtypeStruct((n, QM_W), BF16), tok(QM_W))]
    if transposed:
        outs.append(blocks(KV_LORA))
    outs += [(jax.ShapeDtypeStruct((n, KV_LORA), F32), tok(KV_LORA)),
             (jax.ShapeDtypeStruct((n, MLA_ROPE), F32), tok(MLA_ROPE))]
    return pl.pallas_call(
        functools.partial(_proj_kernel, transposed=transposed, n_alias=len(aliases),
                          state_slab=state_slab if transposed else 0),
        grid=(nt,), in_specs=in_specs, out_specs=[o[1] for o in outs], out_shape=[o[0] for o in outs],
        input_output_aliases=aliases,
        compiler_params=_params("parallel"), name="proj_t" if transposed else "proj",
    )(*args)


def _sb_scores(q_ref, k_ref, qrows, j, z_out):
    krows = pl.ds(pl.multiple_of(j * BLK, BLK), BLK)
    for h in range(N_HEADS):
        p, hh = h // 2, h % 2
        cols = slice(p * LANES, (p + 1) * LANES)
        kh = k_ref[hh, 0, krows, cols]
        z_out[h] = _dot_nt(kh, q_ref[0, qrows, cols]) * LOG2E


def _sb_block(z_in, vt_ref, j, acc_ref, car_ref, umat, vis):
    heads = range(N_HEADS)
    mms, lsig, sp0 = [], [], []
    for h in heads:
        z = z_in[h]
        sp = _softplus2(z)
        lsig.append(z - sp)
        if vis is not None:
            sp = jnp.where(vis, sp, 0.0)
        sp0.append(sp[0:1, :])
        mms.append(_dot(umat, sp.astype(BF16)))
    for h in heads:
        car = car_ref[h, 0:1, :]
        a = jnp.exp2(lsig[h] + mms[h] + car)
        if vis is not None:
            a = jnp.where(vis, a, 0.0)
        rows = slice(h * SB_HEAD_DIM, (h + 1) * SB_HEAD_DIM)
        acc_ref[rows, :] += _dot(vt_ref[0, j, rows, :], a.astype(BF16))
        car_ref[h, 0:1, :] = car + mms[h][0:1, :] - sp0[h]


def _sb_kernel(q_ref, k_ref, vt_ref, o_ref, acc_ref, car_ref, z0_ref, z1_ref, *, nblk):
    kidx = lax.broadcasted_iota(jnp.int32, (BLK, BLK), 0)
    qidx = lax.broadcasted_iota(jnp.int32, (BLK, BLK), 1)
    umat = jnp.where(qidx > kidx, -1.0, 0.0).astype(BF16)
    vis = kidx < qidx

    def q_block(i, carry):
        acc_ref[...] = jnp.zeros_like(acc_ref)
        car_ref[...] = jnp.zeros_like(car_ref)
        qrows = pl.ds(pl.multiple_of(i * BLK, BLK), BLK)

        def step(j, z_cur, z_next, mask):
            _sb_scores(q_ref, k_ref, qrows, jnp.maximum(j - 1, 0), z_next)
            _sb_block(z_cur, vt_ref, j, acc_ref, car_ref, umat, mask)

        _sb_scores(q_ref, k_ref, qrows, i, z0_ref)
        step(i, z0_ref, z1_ref, vis)

        def k_pair(m, c):
            j = i - 1 - 2 * m
            step(j, z1_ref, z0_ref, None)
            step(j - 1, z0_ref, z1_ref, None)
            return c

        lax.fori_loop(0, i // 2, k_pair, 0)

        @pl.when(i % 2 == 1)
        def _():
            step(0, z1_ref, z0_ref, None)

        o_ref[0, qrows, :] = acc_ref[...].T.astype(BF16)
        return carry

    lax.fori_loop(0, nblk, q_block, 0)


def _sb_prompt(q, kbf, vt, batch, seq_len):
    nblk = seq_len // BLK
    q3 = q.reshape(batch, seq_len, SB_W)
    k4 = kbf.reshape(2, batch, seq_len, SB_W)
    row = pl.BlockSpec((1, seq_len, SB_W), lambda b: (b, 0, 0))
    out = pl.pallas_call(
        functools.partial(_sb_kernel, nblk=nblk),
        grid=(batch,),
        in_specs=[row, pl.BlockSpec((2, 1, seq_len, SB_W), lambda b: (0, b, 0, 0)),
                  pl.BlockSpec((1, nblk, SB_W, BLK), lambda b: (b, 0, 0, 0))],
        out_specs=row,
        out_shape=jax.ShapeDtypeStruct((batch, seq_len, SB_W), BF16),
        scratch_shapes=[pltpu.VMEM((SB_W, BLK), F32), pltpu.VMEM((N_HEADS, SUBLANES, BLK), F32),
                        pltpu.VMEM((N_HEADS, BLK, BLK), F32), pltpu.VMEM((N_HEADS, BLK, BLK), F32)],
        compiler_params=_params("parallel"), name="sb_prompt",
    )(q3, k4, vt)
    return out.reshape(batch * seq_len, SB_W)


def _mla_scores(q_ref, c_ref, qrows, j, s_out):
    c = c_ref[pl.ds(pl.multiple_of(j * BLK, BLK), BLK), :]
    for h in range(N_HEADS):
        s_out[h] = _dot_nt(c, q_ref[h, qrows, :])


def _mla_block(s_in, ct_ref, j, acc_ref, st_ref, vis):
    ct = ct_ref[0, j]
    for h in range(N_HEADS):
        s = s_in[h]
        if vis is not None:
            s = jnp.where(vis, s, -jnp.inf)
        m_old = st_ref[h, 0:1, :]
        m_new = jnp.maximum(m_old, jnp.max(s, axis=0, keepdims=True))
        p = jnp.exp2((s - m_new) * (MLA_SCALE * LOG2E))
        alpha = jnp.exp2((m_old - m_new) * (MLA_SCALE * LOG2E))
        st_ref[h, 1:2, :] = alpha * st_ref[h, 1:2, :] + jnp.sum(p, axis=0, keepdims=True)
        st_ref[h, 0:1, :] = m_new
        acc_ref[h] = alpha * acc_ref[h] + _dot(ct, p.astype(BF16))


def _mla_kernel(q_ref, c_ref, ct_ref, o_ref, acc_ref, st_ref, s0_ref, s1_ref, *, nblk):
    kidx = lax.broadcasted_iota(jnp.int32, (BLK, BLK), 0)
    qidx = lax.broadcasted_iota(jnp.int32, (BLK, BLK), 1)
    vis = (kidx // CHUNK) <= (qidx // CHUNK)

    def q_block(i, carry):
        acc_ref[...] = jnp.zeros_like(acc_ref)
        for h in range(N_HEADS):
            st_ref[h, 0:1, :] = jnp.full((1, BLK), -jnp.inf, F32)
            st_ref[h, 1:2, :] = jnp.zeros((1, BLK), F32)
        rows = pl.ds(pl.multiple_of(i * BLK, BLK), BLK)
        def step(j, j_next, s_cur, s_next, mask):
            _mla_scores(q_ref, c_ref, rows, j_next, s_next)
            _mla_block(s_cur, ct_ref, j, acc_ref, st_ref, mask)

        nxt = lambda j: jnp.minimum(j + 1, nblk - 1)
        _mla_scores(q_ref, c_ref, rows, i, s0_ref)
        step(i, 0, s0_ref, s1_ref, vis)

        def k_pair(m, c):
            j = 2 * m
            step(j, nxt(j), s1_ref, s0_ref, None)
            step(j + 1, nxt(j + 1), s0_ref, s1_ref, None)
            return c

        lax.fori_loop(0, i // 2, k_pair, 0)

        @pl.when(i % 2 == 1)
        def _():
            step(i - 1, nxt(i - 1), s1_ref, s0_ref, None)

        for h in range(N_HEADS):
            o = acc_ref[h] * (1.0 / st_ref[h, 1:2, :])
            o_ref[rows, h * KV_LORA:(h + 1) * KV_LORA] = o.T.astype(BF16)
        return carry

    lax.fori_loop(0, nblk, q_block, 0)


def _mla_prompt(qm, cm, ct, batch, seq_len):
    nblk = seq_len // BLK
    return pl.pallas_call(
        functools.partial(_mla_kernel, nblk=nblk),
        grid=(batch,),
        in_specs=[pl.BlockSpec((N_HEADS, seq_len, QM_W), lambda b: (0, b, 0)),
                  pl.BlockSpec((seq_len, QM_W), lambda b: (b, 0)),
                  pl.BlockSpec((1, nblk, KV_LORA, BLK), lambda b: (b, 0, 0, 0))],
        out_specs=pl.BlockSpec((seq_len, N_HEADS * KV_LORA), lambda b: (b, 0)),
        out_shape=jax.ShapeDtypeStruct((batch * seq_len, N_HEADS * KV_LORA), BF16),
        scratch_shapes=[pltpu.VMEM((N_HEADS, KV_LORA, BLK), F32), pltpu.VMEM((N_HEADS, SUBLANES, BLK), F32),
                        pltpu.VMEM((N_HEADS, BLK, BLK), F32), pltpu.VMEM((N_HEADS, BLK, BLK), F32)],
        compiler_params=_params("parallel"), name="mla_prompt",
    )(qm, cm, ct)


def _sbdec_blocks(score, av, nblocks, t, acc_ref, car_ref, wn, vis):
    heads = range(N_HEADS)
    zs = [jnp.concatenate([score(b, h) for h in heads], axis=0) * LOG2E
          for b in range(nblocks)]
    mms, lsig, sp0 = [], [], []
    for b in range(nblocks):
        sp = _softplus2(zs[b])
        lsig.append(zs[b] - sp)
        if vis is not None:
            sp = jnp.where(vis, sp, 0.0)
        sp0.append(sp[:, 0:1])
        mms.append(_dot(sp.astype(BF16), wn))
    car = car_ref[:, 0:1]
    for b in range(nblocks):
        a = jnp.exp2(lsig[b] + mms[b] + car)
        if vis is not None:
            a = jnp.where(vis, a, 0.0)
        a = a.astype(BF16)
        for h in heads:
            acc_ref[h] += av(b, h, a[h * t:(h + 1) * t, :])
        car = car + mms[b][:, 0:1] - sp0[b]
    car_ref[...] = jnp.broadcast_to(car, car_ref.shape)


def _sbdec_kernel(q_ref, kn_ref, vn_ref, ck_ref, cv_ref, o_ref, qh_ref, acc_ref, car_ref, *, t_new):
    j = pl.program_id(1)
    hcols = lambda h: slice(h * SB_HEAD_DIM, (h + 1) * SB_HEAD_DIM)

    @pl.when(j == 0)
    def _():
        acc_ref[...] = jnp.zeros_like(acc_ref)
        car_ref[...] = jnp.zeros_like(car_ref)
        for h in range(N_HEADS):
            qh_ref[h] = q_ref[:, hcols(h)]
        kp = lax.broadcasted_iota(jnp.int32, (t_new, t_new), 0)
        kc = lax.broadcasted_iota(jnp.int32, (t_new, t_new), 1)
        wn_new = jnp.where(kp > kc, -1.0, 0.0).astype(BF16)
        qi = lax.rem(lax.broadcasted_iota(jnp.int32, (N_HEADS * t_new, t_new), 0), t_new)
        ki = lax.broadcasted_iota(jnp.int32, (N_HEADS * t_new, t_new), 1)
        _sbdec_blocks(lambda b, h: _dot_nt(qh_ref[h], kn_ref[:, hcols(h)].astype(BF16)),
                      lambda b, h, a: _dot(a, vn_ref[:, hcols(h)].astype(BF16)),
                      1, t_new, acc_ref, car_ref, wn_new, ki < qi)

    kp = lax.broadcasted_iota(jnp.int32, (BLK, BLK), 0)
    kc = lax.broadcasted_iota(jnp.int32, (BLK, BLK), 1)
    wn = jnp.where(kp > kc, -1.0, 0.0).astype(BF16)
    nsub = DEC_KEYS // BLK
    keys = lambda b: slice((nsub - 1 - b) * BLK, (nsub - b) * BLK)
    _sbdec_blocks(lambda b, h: _dot(qh_ref[h], ck_ref[0, 0, h, :, keys(b)].astype(BF16)),
                  lambda b, h, a: _dot_nt(a, cv_ref[0, 0, h, :, keys(b)].astype(BF16)),
                  nsub, t_new, acc_ref, car_ref, wn, None)

    @pl.when(j == pl.num_programs(1) - 1)
    def _():
        for h in range(N_HEADS):
            o_ref[:, hcols(h)] = acc_ref[h].astype(BF16)


def _sb_sample(q, k_new, v_new, cache_k, cache_v, layer, batch, t_new):
    past = cache_k.shape[4]
    assert past % DEC_KEYS == 0
    nkb = past // DEC_KEYS
    tok = lambda: pl.BlockSpec((t_new, SB_W), lambda b, j: (b, 0))
    cache = lambda: pl.BlockSpec((1, 1, N_HEADS, SB_HEAD_DIM, DEC_KEYS), lambda b, j: (layer, b, 0, 0, nkb - 1 - j))
    return pl.pallas_call(
        functools.partial(_sbdec_kernel, t_new=t_new),
        grid=(batch, nkb),
        in_specs=[tok(), tok(), tok(), cache(), cache()],
        out_specs=tok(),
        out_shape=jax.ShapeDtypeStruct((batch * t_new, SB_W), BF16),
        scratch_shapes=[pltpu.VMEM((N_HEADS, t_new, SB_HEAD_DIM), BF16),
                        pltpu.VMEM((N_HEADS, t_new, SB_HEAD_DIM), F32),
                        pltpu.VMEM((N_HEADS * t_new, LANES), F32)],
        compiler_params=_params("parallel", "arbitrary"), name="sb_sample",
    )(q, k_new, v_new, cache_k, cache_v)


def _mladec_kernel(q_ref, cn_ref, ck_ref, kr_ref, o_ref, *, t_new, past, new_visible):
    q = q_ref[...].reshape(N_HEADS * t_new, QM_W)
    q_lat = q[:, :KV_LORA]
    q_rope = q[:, KV_LORA:KV_LORA + MLA_ROPE]
    c2 = MLA_SCALE * LOG2E

    cn = cn_ref[...]
    s = _dot_nt(q, cn)
    if not new_visible:
        qi = past + lax.broadcasted_iota(jnp.int32, s.shape, 0) % t_new
        ki = past + lax.broadcasted_iota(jnp.int32, s.shape, 1)
        s = jnp.where((ki // CHUNK) <= (qi // CHUNK), s, -jnp.inf)
    chunks = [slice(c * MLA_DEC_KEYS, (c + 1) * MLA_DEC_KEYS) for c in range(past // MLA_DEC_KEYS)]
    cks = [ck_ref[0, 0, rows, :].astype(BF16) for rows in chunks]
    ss = [_dot_nt(q_lat, ck) + _dot(q_rope, kr_ref[0, 0, :, rows].astype(BF16))
          for ck, rows in zip(cks, chunks)]
    m = jnp.max(s, axis=1, keepdims=True)
    for sc in ss:
        m = jnp.maximum(m, jnp.max(sc, axis=1, keepdims=True))
    p = jnp.exp2((s - m) * c2)
    l = jnp.sum(p, axis=1, keepdims=True)
    acc = _dot(p.astype(BF16), cn[:, :KV_LORA])
    for sc, ck in zip(ss, cks):
        p = jnp.exp2((sc - m) * c2)
        l = l + jnp.sum(p, axis=1, keepdims=True)
        acc = acc + _dot(p.astype(BF16), ck)
    o = (acc * (1.0 / l)).astype(BF16)
    for h in range(N_HEADS):
        o_ref[:, h * KV_LORA:(h + 1) * KV_LORA] = o[h * t_new:(h + 1) * t_new, :]


def _mla_sample(qm, cm, cache_ckv, cache_kr, layer, batch, t_new):
    past = cache_ckv.shape[2]
    assert past % MLA_DEC_KEYS == 0 and past % CHUNK == 0
    q_pos = past + np.arange(t_new)
    new_visible = bool(((q_pos[None, :] // CHUNK) <= (q_pos[:, None] // CHUNK)).all())
    return pl.pallas_call(
        functools.partial(_mladec_kernel, t_new=t_new, past=past, new_visible=new_visible),
        grid=(batch,),
        in_specs=[pl.BlockSpec((N_HEADS, t_new, QM_W), lambda b: (0, b, 0)),
                  pl.BlockSpec((t_new, QM_W), lambda b: (b, 0)),
                  pl.BlockSpec((1, 1, past, KV_LORA), lambda b: (layer, b, 0, 0)),
                  pl.BlockSpec((1, 1, MLA_ROPE, past), lambda b: (layer, b, 0, 0))],
        out_specs=pl.BlockSpec((t_new, N_HEADS * KV_LORA), lambda b: (b, 0)),
        out_shape=jax.ShapeDtypeStruct((batch * t_new, N_HEADS * KV_LORA), BF16),
        compiler_params=_params("parallel"), name="mla_sample",
    )(qm, cm, cache_ckv, cache_kr)


def _mix_kernel(x_ref, gpre_ref, osb_ref, olat_ref, wg_ref, wpsb_ref, wuv_ref, wpmla_ref, wout_ref,
                gpost_ref, o_ref):
    x = x_ref[...]
    hb = _rmsnorm(x, gpre_ref[...]).astype(BF16)
    gates = _dot(hb, wg_ref[...])
    y_a = _dot(osb_ref[...], wpsb_ref[...])
    o_mla = _dot(olat_ref[...], wuv_ref[...]).astype(BF16)
    y_b = _dot(o_mla, wpmla_ref[...])
    mixin = _sigmoid(gates[:, :D_MODEL]) * y_a + _sigmoid(gates[:, D_MODEL:]) * y_b
    mix = _dot(mixin.astype(BF16), wout_ref[...])
    o_ref[...] = x + _rmsnorm(mix, gpost_ref[...])


def _mix(x2d, o_sb, o_lat, lw):
    n = x2d.shape[0]
    tm = MIX_TILE
    assert n % tm == 0
    tok = lambda w: pl.BlockSpec((tm, w), lambda i: (i, 0))
    return pl.pallas_call(
        _mix_kernel, grid=(n // tm,),
        in_specs=[tok(D_MODEL), _const_spec((1, D_MODEL)), tok(SB_W), tok(N_HEADS * KV_LORA),
                  _const_spec(lw["w_gates"].shape), _const_spec(lw["w_proj_sb"].shape),
                  _const_spec(lw["w_uv_bd"].shape), _const_spec(lw["w_proj_mla"].shape),
                  _const_spec(lw["w_out"].shape), _const_spec((1, D_MODEL))],
        out_specs=tok(D_MODEL), out_shape=jax.ShapeDtypeStruct((n, D_MODEL), F32),
        compiler_params=_params("parallel"), name="mix",
    )(x2d, lw["g_pre_mix"], o_sb, o_lat, lw["w_gates"], lw["w_proj_sb"], lw["w_uv_bd"], lw["w_proj_mla"],
      lw["w_out"], lw["g_post_mix"])


def _ffn_kernel(x_ref, gpre_ref, wup_ref, cw_ref, past_ref, wdown_ref, gpost_ref, wpg_ref, p_ref, wpp_ref,
                o_ref, conv_ref, *u_scr, nseq, rows):
    t = pl.program_id(1)
    span = SUBLANES + rows
    lead = SUBLANES - (CONV_W - 1)

    @pl.when(t == 0)
    def _():
        for c in range(2 * N_FF_CHUNKS):
            for s in range(nseq):
                u_scr[c][s * span + lead:s * span + SUBLANES, :] = past_ref[s, c]

    x = x_ref[...]
    hb = _rmsnorm(x, gpre_ref[...]).astype(BF16)

    def up(c):
        for cc in (c, N_FF_CHUNKS + c):
            u = _dot(hb, wup_ref[:, cc * FF_CHUNK:(cc + 1) * FF_CHUNK])
            for s in range(nseq):
                u_scr[cc][s * span + SUBLANES:(s + 1) * span, :] = u[s * rows:(s + 1) * rows, :]

    def conv(c):
        cw = cw_ref[c]
        u = u_scr[c]
        out = []
        for s in range(nseq):
            full = u[s * span:(s + 1) * span, :]
            uc = cw[3:4, :]
            for i in range(CONV_W):
                back = CONV_W - 1 - i
                shifted = full if back == 0 else pltpu.roll(full, back, axis=0)
                uc = uc + shifted[SUBLANES:, :] * cw[i:i + 1, :]
            conv_ref[s, c] = u[(s + 1) * span - (CONV_W - 1):(s + 1) * span, :]
            u[s * span:s * span + SUBLANES, :] = u[s * span + rows:(s + 1) * span, :]
            out.append(uc)
        return out[0] if nseq == 1 else jnp.concatenate(out, axis=0)

    up(0)
    f = None
    acts = []
    for c in range(N_FF_CHUNKS):
        if c + 1 < N_FF_CHUNKS:
            up(c + 1)
        acts.append((_gelu_tanh(conv(c)) * conv(N_FF_CHUNKS + c)).astype(BF16))
        if len(acts) == FF_DOWN_GROUP or c + 1 == N_FF_CHUNKS:
            k1 = (c + 1) * FF_CHUNK
            k0 = k1 - len(acts) * FF_CHUNK
            part = _dot(jnp.concatenate(acts, axis=1), wdown_ref[k0:k1, :])
            f = part if f is None else f + part
            acts = []
    half = x.shape[0] // 2
    for r in (slice(0, half), slice(half, 2 * half)):
        x2 = x[r, :] + _rmsnorm(f[r, :], gpost_ref[...])
        ple = _sigmoid(_dot(x2.astype(BF16), wpg_ref[...])) * _dot(p_ref[0, r, :].astype(BF16), wpp_ref[...])
        o_ref[r, :] = x2 + ple


def _ffn(x2d, p3d, layer, conv_past, batch, seq_len, lw):
    tm = TOK_TILE
    rows = min(tm, seq_len)
    nseq = tm // rows
    nt = seq_len // rows
    assert batch % nseq == 0 and (nseq == 1 or nt == 1)
    nch = 2 * N_FF_CHUNKS
    tok = lambda w: pl.BlockSpec((tm, w), lambda b, t: (b * nt + t, 0))
    ple = pl.BlockSpec((1, tm, D_PLE), lambda b, t: (layer, b * nt + t, 0))
    state = pl.BlockSpec((nseq, nch, CONV_W - 1, FF_CHUNK), lambda b, t: (b, 0, 0, 0))
    return pl.pallas_call(
        functools.partial(_ffn_kernel, nseq=nseq, rows=rows),
        grid=(batch // nseq, nt),
        in_specs=[tok(D_MODEL), _const_spec((1, D_MODEL)), _const_spec(lw["w_up"].shape),
                  _const_spec(lw["conv"].shape), state, _const_spec(lw["w_down"].shape),
                  _const_spec((1, D_MODEL)), _const_spec(lw["w_ple_gate"].shape), ple,
                  _const_spec(lw["w_ple_proj"].shape)],
        out_specs=[tok(D_MODEL), state],
        out_shape=[jax.ShapeDtypeStruct(x2d.shape, F32), jax.ShapeDtypeStruct(conv_past.shape, F32)],
        scratch_shapes=[pltpu.VMEM((nseq * (SUBLANES + rows), FF_CHUNK), F32) for _ in range(nch)],
        compiler_params=_params("arbitrary", "arbitrary"), name="ffn",
    )(x2d, lw["g_pre_ffn"], lw["w_up"], lw["conv"], conv_past, lw["w_down"], lw["g_post_ffn"],
      lw["w_ple_gate"], p3d, lw["w_ple_proj"])


def _prep_layer(w_in, ckv_gain, w_uk, w_uv, w_proj_sb, w_proj_mla, w_out, g_pre_mix, g_post_mix, g_pre_ffn,
                g_post_ffn, w_up, conv_w, conv_b, w_down, w_ple_gate, w_ple_proj):
    o_qm = 3 * SB_W
    o_ckv = o_qm + N_HEADS * (MLA_NOPE + MLA_ROPE)
    o_kr = o_ckv + KV_LORA
    o_g = o_kr + MLA_ROPE
    wqm = w_in[:, o_qm:o_ckv].reshape(D_MODEL, N_HEADS, MLA_NOPE + MLA_ROPE)
    w_qn = wqm[:, :, :MLA_NOPE].reshape(D_MODEL, N_HEADS * MLA_NOPE)
    w_qr = jnp.concatenate([wqm[:, :, MLA_NOPE:MLA_NOPE + ROPE_HALF].reshape(D_MODEL, LANES),
                            wqm[:, :, MLA_NOPE + ROPE_HALF:].reshape(D_MODEL, LANES)], axis=1)
    w_kr = w_in[:, o_kr:o_g]
    w_kx = jnp.zeros((D_MODEL, 2 * LANES), F32)
    w_kx = w_kx.at[:, :ROPE_HALF].set(w_kr[:, :ROPE_HALF]).at[:, LANES:LANES + ROPE_HALF].set(w_kr[:, ROPE_HALF:])
    wa = jnp.zeros((N_HEADS, LANES, QM_W), F32)
    wb = np.zeros((N_HEADS, 2 * LANES, QM_W), np.float32)
    for h in range(N_HEADS):
        r0 = (h % 2) * MLA_NOPE
        wa = wa.at[h, r0:r0 + MLA_NOPE, :KV_LORA].set(w_uk[:, h, :].T)
        for i in range(ROPE_HALF):
            wb[h, h * ROPE_HALF + i, KV_LORA + i] = 1.0
            wb[h, LANES + h * ROPE_HALF + i, KV_LORA + ROPE_HALF + i] = 1.0
    w_uv_bd = jnp.zeros((N_HEADS * KV_LORA, N_HEADS * MLA_V), F32)
    for h in range(N_HEADS):
        w_uv_bd = w_uv_bd.at[h * KV_LORA:(h + 1) * KV_LORA, h * MLA_V:(h + 1) * MLA_V].set(w_uv[:, h, :])
    nch = 2 * N_FF_CHUNKS
    conv = jnp.concatenate([conv_w, conv_b[None, :], jnp.zeros((SUBLANES - CONV_W - 1, 2 * D_FF), F32)], axis=0)
    row = lambda g: g.reshape(1, -1)
    return dict(
        w_qkv=w_in[:, :o_qm].astype(BF16), w_kvt=w_in[:, SB_W:o_qm].T.astype(BF16),
        w_qn=w_qn.astype(BF16), w_qr=w_qr.astype(BF16),
        w_ckv=w_in[:, o_ckv:o_kr].astype(BF16), w_kx=w_kx.astype(BF16), ckv_gain=row(ckv_gain),
        wa=wa.astype(BF16), wb=jnp.asarray(wb, BF16), eye=jnp.eye(LANES, dtype=BF16),
        w_gates=w_in[:, o_g:].astype(BF16), w_proj_sb=w_proj_sb.astype(BF16), w_uv_bd=w_uv_bd.astype(BF16),
        w_proj_mla=w_proj_mla.astype(BF16), w_out=w_out.astype(BF16),
        g_pre_mix=row(g_pre_mix), g_post_mix=row(g_post_mix), g_pre_ffn=row(g_pre_ffn), g_post_ffn=row(g_post_ffn),
        w_up=w_up.astype(BF16),
        conv=conv.reshape(SUBLANES, nch, FF_CHUNK).transpose(1, 0, 2),
        w_down=w_down.astype(BF16),
        w_ple_gate=w_ple_gate.astype(BF16), w_ple_proj=w_ple_proj.astype(BF16))


def _rope_tables(pos, rows):
    freqs = jnp.power(ROPE_THETA, -jnp.arange(ROPE_HALF, dtype=F32) / ROPE_HALF)
    ang = pos.astype(F32)[:, None] * freqs[None, :]
    reps = max(1, rows // pos.shape[0])
    tile = lambda a: jnp.tile(a, (reps, LANES // ROPE_HALF))
    return tile(jnp.cos(ang)), tile(jnp.sin(ang))


def _conv_to_chunks(state):
    b = state.shape[0]
    return state.reshape(b, CONV_W - 1, 2 * N_FF_CHUNKS, FF_CHUNK).transpose(0, 2, 1, 3)


def _conv_from_chunks(state):
    b = state.shape[0]
    return state.transpose(0, 2, 1, 3).reshape(b, CONV_W - 1, 2 * D_FF)


def kernel(x_prompt, x_sample, p_prompt, p_sample, cache_sb_k, cache_sb_v, cache_mla_ckv, cache_mla_krope,
           state_ffn_conv, w_in, ckv_gain, w_uk, w_uv, w_proj_sb, w_proj_mla, w_out, g_pre_mix, g_post_mix,
           g_pre_ffn, g_post_ffn, w_up, conv_w, conv_b, w_down, w_ple_gate, w_ple_proj):
    bp, tp, _ = x_prompt.shape
    bs, ts, _ = x_sample.shape
    depth = w_in.shape[0]
    past = cache_sb_k.shape[2]
    assert tp % TOK_TILE == 0 and (bs * ts) % TOK_TILE == 0 and TOK_TILE % ts == 0
    tabs_p = _rope_tables(jnp.arange(tp, dtype=jnp.int32), TOK_TILE)
    tabs_s = _rope_tables(past + jnp.arange(ts, dtype=jnp.int32), TOK_TILE)
    xp = x_prompt.reshape(bp * tp, D_MODEL)
    xs = x_sample.reshape(bs * ts, D_MODEL)
    pp = p_prompt.reshape(depth, bp * tp, D_PLE)
    ps = p_sample.reshape(depth, bs * ts, D_PLE)
    ck_t = cache_sb_k.transpose(0, 1, 3, 4, 2)
    cv_t = cache_sb_v.transpose(0, 1, 3, 4, 2)
    kr_t = cache_mla_krope.transpose(0, 1, 3, 2)
    zero_conv = jnp.zeros((bp, 2 * N_FF_CHUNKS, CONV_W - 1, FF_CHUNK), F32)
    st_p, st_s = [], []
    kv_p = None
    for i in range(depth):
        lw = _prep_layer(w_in[i], ckv_gain[i], w_uk[i], w_uv[i], w_proj_sb[i], w_proj_mla[i], w_out[i],
                         g_pre_mix[i], g_post_mix[i], g_pre_ffn[i], g_post_ffn[i], w_up[i], conv_w[i],
                         conv_b[i], w_down[i], w_ple_gate[i], w_ple_proj[i])
        q, k_bf, kt_all, vt_all, vt, qm, cm, ct, c_st, r_st = _proj(xp, tp, tabs_p, lw, kv_states=(i, depth, kv_p))
        kv_p = (kt_all, vt_all)
        o_sb = _sb_prompt(q, k_bf, vt, bp, tp)
        o_lat = _mla_prompt(qm, cm, ct, bp, tp)
        x1 = _mix(xp, o_sb, o_lat, lw)
        xp, conv_p = _ffn(x1, pp, i, zero_conv, bp, tp, lw)
        st_p.append((c_st.reshape(bp, tp, KV_LORA), r_st.reshape(bp, tp, MLA_ROPE), _conv_from_chunks(conv_p)))
        q, k_st, v_st, qm, cm, c_st, r_st = _proj(xs, ts, tabs_s, lw)
        o_sb = _sb_sample(q, k_st, v_st, ck_t, cv_t, i, bs, ts)
        o_lat = _mla_sample(qm, cm, cache_mla_ckv, kr_t, i, bs, ts)
        x1 = _mix(xs, o_sb, o_lat, lw)
        xs, conv_s = _ffn(x1, ps, i, _conv_to_chunks(state_ffn_conv[i]), bs, ts, lw)
        st_s.append((k_st.reshape(bs, ts, N_HEADS, SB_HEAD_DIM), v_st.reshape(bs, ts, N_HEADS, SB_HEAD_DIM),
                     c_st.reshape(bs, ts, KV_LORA), r_st.reshape(bs, ts, MLA_ROPE), _conv_from_chunks(conv_s)))
    stack = lambda sts, k: jnp.stack([s[k] for s in sts])
    heads_last = lambda a: a.reshape(depth, bp, N_HEADS, SB_HEAD_DIM, tp).transpose(0, 1, 4, 2, 3)
    return ((xp.reshape(bp, tp, D_MODEL), xs.reshape(bs, ts, D_MODEL), heads_last(kv_p[0]), heads_last(kv_p[1]))
            + tuple(stack(st_p, k) for k in range(3)) + tuple(stack(st_s, k) for k in range(5)))
```
